```python
import math
import numpy as np
import jax, jax.numpy as jnp
from jax import lax

D_MODEL = 1024
BATCH = 16
SEQ = 2048
DEPTH = 2

HEAD_DIM = 64
ROT_DIM = HEAD_DIM // 4
ROPE_THETA = 500000.0
EPS = 1e-6
Q_BLOCK = 128
A_HEADS = 4
IDX_HEADS = 8
IDX_DIM = 64
INDEX_TOPK = 256
B_HEADS = 4
MOBA_BLOCK = 256
MOBA_TOPK = 3
MOBA_Q_CHUNK = 16
C_HEADS = 4
C_VDIM = 2 * HEAD_DIM
D_FF = 2816

WIDTH_A = A_HEADS * HEAD_DIM
WIDTH_B = B_HEADS * HEAD_DIM
WIDTH_C = C_HEADS * C_VDIM
MIX_WIDTH = WIDTH_A + WIDTH_B + WIDTH_C
IN_SPLITS = (A_HEADS * HEAD_DIM, HEAD_DIM, HEAD_DIM, IDX_HEADS * IDX_DIM, IDX_DIM, IDX_HEADS,
             B_HEADS * HEAD_DIM, B_HEADS * HEAD_DIM, B_HEADS * HEAD_DIM,
             C_HEADS * 2 * HEAD_DIM, C_HEADS * 2 * HEAD_DIM, C_HEADS * C_VDIM,
             D_MODEL, D_MODEL, D_MODEL)
IN_WIDTH = sum(IN_SPLITS)

kernel_name = 'hybrid_dsa_moba_diffattn_macaron'


def rmsnorm(x, g):
    xf = x.astype(jnp.float32)
    y = xf * lax.rsqrt(jnp.mean(xf * xf, axis=-1, keepdims=True) + EPS)
    return y.astype(x.dtype) * g


def rope_tables(positions):
    inv_freq = jnp.power(ROPE_THETA, -jnp.arange(0, ROT_DIM, 2, dtype=jnp.float32) / ROT_DIM)
    ang = positions.astype(jnp.float32)[..., None] * inv_freq
    return jnp.cos(ang), jnp.sin(ang)


def rope(x, cos, sin):
    shape = cos.shape[:2] + (1,) * (x.ndim - 3) + cos.shape[-1:]
    c = cos.reshape(shape).astype(x.dtype)
    s = sin.reshape(shape).astype(x.dtype)
    half = ROT_DIM // 2
    x1, x2 = x[..., :half], x[..., half:ROT_DIM]
    return jnp.concatenate([x1 * c - x2 * s, x2 * c + x1 * s, x[..., ROT_DIM:]], axis=-1)


def masked_softmax(scores, mask):
    return jax.nn.softmax(jnp.where(mask, scores.astype(jnp.float32), -jnp.inf), axis=-1)


def swiglu(h, w_gate, w_up, w_down):
    return (jax.nn.silu(h @ w_gate) * (h @ w_up)) @ w_down


def dsa_attention(q, k, v, qi, ki, wi):
    B, S = q.shape[:2]
    topk = min(INDEX_TOPK, S // 4)
    nb = S // Q_BLOCK
    s_pos = jnp.arange(S)
    b_idx = jnp.arange(B)[:, None, None]
    scale = HEAD_DIM ** -0.5

    def blk(t):
        return t.reshape((B, nb, Q_BLOCK) + t.shape[2:]).swapaxes(0, 1)

    def body(args):
        qb, qib, wib, start = args
        t = start + jnp.arange(Q_BLOCK)
        causal = s_pos[None, :] <= t[:, None]
        logits = jnp.einsum('bqhd,bsd->bqhs', qib, ki)
        iscore = jnp.einsum('bqhs,bqh->bqs', jax.nn.relu(logits).astype(jnp.float32), wib.astype(jnp.float32))
        iscore = jnp.where(causal[None], iscore, -jnp.inf)
        _, idx = lax.top_k(iscore, topk)
        k_sel = k[b_idx, idx]
        v_sel = v[b_idx, idx]
        sc = jnp.einsum('bqhd,bqkd->bhqk', qb, k_sel) * scale
        valid = (idx <= t[None, :, None])[:, None]
        p = masked_softmax(sc, valid).astype(v.dtype)
        return jnp.einsum('bhqk,bqkd->bqhd', p, v_sel)

    starts = jnp.arange(nb) * Q_BLOCK
    out = lax.map(body, (blk(q), blk(qi), blk(wi), starts))
    return out.swapaxes(0, 1).reshape(B, S, -1)


def moba_attention(q, k, v):
    B, S, H, d = q.shape
    nblk = -(-S // MOBA_BLOCK)
    pad = nblk * MOBA_BLOCK - S
    scale = d ** -0.5

    def to_blocks(t):
        t = jnp.pad(t, ((0, 0), (0, pad), (0, 0), (0, 0)))
        return t.reshape(B, nblk, MOBA_BLOCK, H, d).transpose(0, 3, 1, 2, 4)

    kbt, vbt = to_blocks(k), to_blocks(v)
    kmean = jnp.mean(kbt.astype(jnp.float32), axis=3).astype(k.dtype)
    n_sel = min(MOBA_TOPK, nblk)
    nc = S // MOBA_Q_CHUNK
    q_chunks = q.reshape(B, nc, MOBA_Q_CHUNK, H, d).transpose(1, 0, 3, 2, 4)
    b_idx = jnp.arange(B)[:, None, None, None]
    h_idx = jnp.arange(H)[None, :, None, None]
    blk_ids = jnp.arange(nblk)
    own_offsets = jnp.arange(MOBA_BLOCK)

    def body(args):
        qb, start = args
        t = start + jnp.arange(MOBA_Q_CHUNK)
        j = start // MOBA_BLOCK
        gate = jnp.einsum('bhqd,bhnd->bhqn', qb, kmean).astype(jnp.float32)
        gate = jnp.where(blk_ids < j, gate, -jnp.inf)
        _, sel = lax.top_k(gate, n_sel)
        sel_valid = sel < j
        k_sel = kbt[b_idx, h_idx, sel]
        v_sel = vbt[b_idx, h_idx, sel]
        s_sel = jnp.einsum('bhqd,bhqnkd->bhqnk', qb, k_sel) * scale
        s_sel = jnp.where(sel_valid[..., None], s_sel.astype(jnp.float32), -jnp.inf)
        s_sel = s_sel.reshape(B, H, MOBA_Q_CHUNK, n_sel * MOBA_BLOCK)
        k_own = lax.dynamic_index_in_dim(kbt, j, axis=2, keepdims=False)
        v_own = lax.dynamic_index_in_dim(vbt, j, axis=2, keepdims=False)
        s_own = jnp.einsum('bhqd,bhkd->bhqk', qb, k_own) * scale
        own_causal = (j * MOBA_BLOCK + own_offsets)[None, :] <= t[:, None]
        s_own = jnp.where(own_causal, s_own.astype(jnp.float32), -jnp.inf)
        p = jax.nn.softmax(jnp.concatenate([s_sel, s_own], axis=-1), axis=-1).astype(v.dtype)
        p_sel = p[..., :n_sel * MOBA_BLOCK].reshape(B, H, MOBA_Q_CHUNK, n_sel, MOBA_BLOCK)
        p_own = p[..., n_sel * MOBA_BLOCK:]
        return (jnp.einsum('bhqnk,bhqnkd->bhqd', p_sel, v_sel)
                + jnp.einsum('bhqk,bhkd->bhqd', p_own, v_own))

    starts = jnp.arange(nc) * MOBA_Q_CHUNK
    out = lax.map(body, (q_chunks, starts))
    return out.transpose(1, 0, 3, 2, 4).reshape(B, S, H * d)


def diff_attention(q, k, v, lam, lam_init, subln_g):
    B, S, H = q.shape[:3]
    nb = S // Q_BLOCK
    s_pos = jnp.arange(S)
    scale = HEAD_DIM ** -0.5
    q_blocks = q.reshape(B, nb, Q_BLOCK, H, 2, HEAD_DIM).swapaxes(0, 1)

    def body(args):
        qb, start = args
        t = start + jnp.arange(Q_BLOCK)
        causal = s_pos[None, :] <= t[:, None]
        sc = jnp.einsum('bqhcd,bshcd->bhcqs', qb, k) * scale
        p = masked_softmax(sc, causal)
        attn = (p[:, :, 0] - lam * p[:, :, 1]).astype(v.dtype)
        return jnp.einsum('bhqs,bshe->bqhe', attn, v)

    starts = jnp.arange(nb) * Q_BLOCK
    out = lax.map(body, (q_blocks, starts))
    out = out.swapaxes(0, 1).reshape(B, S, H, C_VDIM)
    out = rmsnorm(out, subln_g) * (1.0 - lam_init)
    return out.reshape(B, S, H * C_VDIM)


def hybrid_mixer(h, cos, sin, w_in, qk_g, lam_p, subln_g, w_branch, w_out, lam_init):
    B, S, _ = h.shape
    offsets = np.cumsum(IN_SPLITS)[:-1].tolist()
    proj = h @ w_in
    (qa, ka, va, qi, ki, wi, qb, kb, vb, qc, kc, vc, ga, gb, gc) = jnp.split(proj, offsets, axis=-1)
    qa = rope(rmsnorm(qa.reshape(B, S, A_HEADS, HEAD_DIM), qk_g[0]), cos, sin)
    ka = rope(rmsnorm(ka, qk_g[1]), cos, sin)
    qi = rope(qi.reshape(B, S, IDX_HEADS, IDX_DIM), cos, sin)
    ki = rope(ki, cos, sin)
    wi = wi * (IDX_HEADS * IDX_DIM) ** -0.5
    o_a = dsa_attention(qa, ka, va, qi, ki, wi)
    qb = rope(rmsnorm(qb.reshape(B, S, B_HEADS, HEAD_DIM), qk_g[2]), cos, sin)
    kb = rope(rmsnorm(kb.reshape(B, S, B_HEADS, HEAD_DIM), qk_g[3]), cos, sin)
    o_b = moba_attention(qb, kb, vb.reshape(B, S, B_HEADS, HEAD_DIM))
    qc = rope(rmsnorm(qc.reshape(B, S, C_HEADS, 2, HEAD_DIM), qk_g[4]), cos, sin)
    kc = rope(rmsnorm(kc.reshape(B, S, C_HEADS, 2, HEAD_DIM), qk_g[5]), cos, sin)
    lp = lam_p.astype(jnp.float32)
    lam = jnp.exp(jnp.sum(lp[0] * lp[1])) - jnp.exp(jnp.sum(lp[2] * lp[3])) + lam_init
    o_c = diff_attention(qc, kc, vc.reshape(B, S, C_HEADS, C_VDIM), lam, lam_init, subln_g)
    y_a = o_a @ w_branch[:WIDTH_A]
    y_b = o_b @ w_branch[WIDTH_A:WIDTH_A + WIDTH_B]
    y_c = o_c @ w_branch[WIDTH_A + WIDTH_B:]
    merged = jax.nn.sigmoid(ga) * y_a + jax.nn.sigmoid(gb) * y_b + jax.nn.sigmoid(gc) * y_c
    return merged @ w_out


def setup_inputs(seed: int = 0) -> dict:
    key = jax.random.key(seed)
    ks = jax.random.split(key, 14)
    x = jax.random.normal(ks[0], (BATCH, SEQ, D_MODEL), jnp.float32)
    offset = jax.random.randint(ks[1], (BATCH, 1), 0, 4096, dtype=jnp.int32)
    positions = (offset + jnp.arange(SEQ, dtype=jnp.int32)[None, :]).astype(jnp.int32)
    norm_g = 1.0 + 0.02 * jax.random.normal(ks[2], (DEPTH, 3, D_MODEL), jnp.float32)
    w_in = jax.random.normal(ks[3], (DEPTH, D_MODEL, IN_WIDTH), jnp.float32) * D_MODEL ** -0.5
    qk_norm_g = 1.0 + 0.02 * jax.random.normal(ks[4], (DEPTH, 6, HEAD_DIM), jnp.float32)
    lambda_params = 0.1 * jax.random.normal(ks[5], (DEPTH, 4, HEAD_DIM), jnp.float32)
    diff_subln_g = 1.0 + 0.02 * jax.random.normal(ks[6], (DEPTH, C_VDIM), jnp.float32)
    w_branch = jnp.concatenate([
        jax.random.normal(ks[7], (DEPTH, WIDTH_A, D_MODEL), jnp.float32) * WIDTH_A ** -0.5,
        jax.random.normal(ks[8], (DEPTH, WIDTH_B, D_MODEL), jnp.float32) * WIDTH_B ** -0.5,
        jax.random.normal(ks[9], (DEPTH, WIDTH_C, D_MODEL), jnp.float32) * WIDTH_C ** -0.5], axis=1)
    w_out = jax.random.normal(ks[10], (DEPTH, D_MODEL, D_MODEL), jnp.float32) * D_MODEL ** -0.5
    ffn_w_gate = jax.random.normal(ks[11], (DEPTH, 2, D_MODEL, D_FF), jnp.float32) * D_MODEL ** -0.5
    ffn_w_up = jax.random.normal(ks[12], (DEPTH, 2, D_MODEL, D_FF), jnp.float32) * D_MODEL ** -0.5
    ffn_w_down = jax.random.normal(ks[13], (DEPTH, 2, D_FF, D_MODEL), jnp.float32) * D_FF ** -0.5
    return {'x': x, 'positions': positions, 'norm_g': norm_g, 'w_in': w_in, 'qk_norm_g': qk_norm_g,
            'lambda_params': lambda_params, 'diff_subln_g': diff_subln_g, 'w_branch': w_branch,
            'w_out': w_out, 'ffn_w_gate': ffn_w_gate, 'ffn_w_up': ffn_w_up, 'ffn_w_down': ffn_w_down}


def reference(x, positions, norm_g, w_in, qk_norm_g, lambda_params, diff_subln_g, w_branch,
              w_out, ffn_w_gate, ffn_w_up, ffn_w_down):
    cos, sin = rope_tables(positions)
    for layer in range(DEPTH):
        lam_init = 0.8 - 0.6 * math.exp(-0.3 * layer)
        x = x + 0.5 * swiglu(rmsnorm(x, norm_g[layer, 0]), ffn_w_gate[layer, 0], ffn_w_up[layer, 0], ffn_w_down[layer, 0])
        x = x + hybrid_mixer(rmsnorm(x, norm_g[layer, 1]), cos, sin, w_in[layer], qk_norm_g[layer],
                             lambda_params[layer], diff_subln_g[layer], w_branch[layer], w_out[layer], lam_init)
        x = x + 0.5 * swiglu(rmsnorm(x, norm_g[layer, 2]), ffn_w_gate[layer, 1], ffn_w_up[layer, 1], ffn_w_down[layer, 1])
    return x
```

```python
import functools
import math

import jax
import jax.numpy as jnp
from jax import lax
from jax.experimental import pallas as pl
from jax.experimental.pallas import tpu as pltpu

F32 = jnp.float32
BF16 = jnp.bfloat16

HEAD_DIM = 64
ROT_DIM = HEAD_DIM // 4
ROT_HALF = ROT_DIM // 2
ROPE_THETA = 500000.0
EPS = 1e-6
A_HEADS = 4
IDX_HEADS = 8
IDX_DIM = 64
INDEX_TOPK = 256
B_HEADS = 4
MOBA_BLOCK = 256
MOBA_TOPK = 3
C_HEADS = 4
C_VDIM = 2 * HEAD_DIM

LANES = 128
MXU_DIM = 256
ATT_TILE = 256
MASKED = -1e30
INT_MIN = -(2 ** 31)
VMEM_LIMIT = 56 * 1024 * 1024


def _nt_dot(a, b):
    return lax.dot_general(a, b, (((1,), (1,)), ((), ())), preferred_element_type=F32)


def _dot(a, b):
    return jnp.dot(a, b, preferred_element_type=F32)


def _split_dot(x, w):
    hi = x.astype(BF16)
    lo = (x - hi.astype(F32)).astype(BF16)
    return _dot(hi, w) + _dot(lo, w)


def _resident(shape):
    nd = len(shape)
    return pl.BlockSpec(shape, lambda *_: (0,) * nd, pipeline_mode=pl.Buffered(1))


def _params(*sem):
    return pltpu.CompilerParams(dimension_semantics=sem, vmem_limit_bytes=VMEM_LIMIT)


def _rope_table_kernel(pos_ref, freq_ref, cos_ref, sin_ref):
    ang = pos_ref[...].astype(F32) * freq_ref[...]
    lane = lax.broadcasted_iota(jnp.int32, ang.shape, 1) % HEAD_DIM
    c = jnp.cos(ang)
    s = jnp.sin(ang)
    cos_ref[...] = jnp.where(lane < ROT_DIM, c, 1.0)
    sin_ref[...] = jnp.where(lane < ROT_HALF, -s, jnp.where(lane < ROT_DIM, s, 0.0))


def _rope_tables(positions, tm):
    n = positions.size
    pos = positions.reshape(n, 1)
    inv_freq = jnp.power(ROPE_THETA, -jnp.arange(0, ROT_DIM, 2, dtype=F32) / ROT_DIM)
    freq = jnp.tile(jnp.concatenate([inv_freq, inv_freq, jnp.zeros((HEAD_DIM - ROT_DIM,), F32)]), 2)[None, :]
    return pl.pallas_call(
        _rope_table_kernel,
        grid=(n // tm,),
        in_specs=[pl.BlockSpec((tm, 1), lambda i: (i, 0)), pl.BlockSpec((1, LANES), lambda i: (0, 0))],
        out_specs=[pl.BlockSpec((tm, LANES), lambda i: (i, 0))] * 2,
        out_shape=[jax.ShapeDtypeStruct((n, LANES), F32)] * 2,
        compiler_params=_params("parallel"),
        name="rope_tables",
    )(pos, freq)


def _rmsnorm_rows(x, g):
    return x * lax.rsqrt(jnp.mean(x * x, axis=-1, keepdims=True) + EPS) * g


def _ffn_kernel(x_ref, g_ref, wg_ref, wu_ref, wd_ref, o_ref, *, chunk):
    x = x_ref[...]
    h = _rmsnorm_rows(x, g_ref[...]).astype(BF16)
    acc = jnp.zeros(x.shape, F32)
    for c in range(0, wg_ref.shape[1], chunk):
        a = _dot(h, wg_ref[:, c:c + chunk])
        b = _dot(h, wu_ref[:, c:c + chunk])
        t = (a * jax.nn.sigmoid(a) * b).astype(BF16)
        acc = acc + _dot(t, wd_ref[c:c + chunk, :])
    o_ref[...] = x + 0.5 * acc


def _ffn(x, g, wg, wu, wd, tm):
    n, d = x.shape
    f = wg.shape[1]
    return pl.pallas_call(
        functools.partial(_ffn_kernel, chunk=MXU_DIM),
        grid=(n // tm,),
        in_specs=[pl.BlockSpec((tm, d), lambda i: (i, 0)), _resident((1, d)),
                  _resident((d, f)), _resident((d, f)), _resident((f, d))],
        out_specs=pl.BlockSpec((tm, d), lambda i: (i, 0)),
        out_shape=jax.ShapeDtypeStruct((n, d), F32),
        compiler_params=_params("parallel"),
        name="ffn",
    )(x, g, wg, wu, wd)


_PROJ_OUTPUTS = (
    ("qa", 256, "norm_rope_scale", 0, BF16),
    ("ka", 256, "norm_rope", 1, BF16),
    ("va", 256, "plain", None, BF16),
    ("qi", 512, "rope", None, BF16),
    ("ki", 256, "rope", None, BF16),
    ("wi", 128, "index_weight", None, F32),
    ("qb", 256, "norm_rope_scale", 2, BF16),
    ("kb", 256, "norm_rope", 3, BF16),
    ("vb", 256, "plain", None, BF16),
    ("qc", 512, "norm_rope_scale", 4, BF16),
    ("kc", 512, "norm_rope", 5, BF16),
    ("vc", 512, "plain", None, BF16),
    ("gate", 3072, "sigmoid", None, BF16),
)
_Q_SCALE = HEAD_DIM ** -0.5


def _proj_kernel(x_ref, g_ref, qkg_ref, cos_ref, sin_ref, *refs):
    n_out = len(_PROJ_OUTPUTS)
    w_refs, o_refs = refs[:n_out], refs[n_out:]
    tm = x_ref.shape[0]
    h = _rmsnorm_rows(x_ref[...], g_ref[...]).astype(BF16)
    cos = cos_ref[...]
    sin = sin_ref[...]
    lane = lax.broadcasted_iota(jnp.int32, (tm, LANES), 1)
    take_upper = (lane % HEAD_DIM) < ROT_HALF
    r = lax.broadcasted_iota(jnp.int32, (LANES, LANES), 0) // HEAD_DIM
    c = lax.broadcasted_iota(jnp.int32, (LANES, LANES), 1) // HEAD_DIM
    same_head = (r == c).astype(BF16)

    def epilogue(y, kind, g_idx):
        if kind in ("norm_rope_scale", "norm_rope"):
            ssq = _split_dot(y * y, same_head)
            y = y * lax.rsqrt(ssq * (1.0 / HEAD_DIM) + EPS) * qkg_ref[g_idx:g_idx + 1, :]
        if kind in ("norm_rope_scale", "norm_rope", "rope"):
            partner = jnp.where(take_upper, pltpu.roll(y, LANES - ROT_HALF, 1), pltpu.roll(y, ROT_HALF, 1))
            y = y * cos + partner * sin
        if kind == "norm_rope_scale":
            y = y * _Q_SCALE
        if kind == "index_weight":
            y = y * ((IDX_HEADS * IDX_DIM) ** -0.5)
        if kind == "sigmoid":
            y = jax.nn.sigmoid(y)
        return y

    for (name, width, kind, g_idx, dtype), w_ref, o_ref in zip(_PROJ_OUTPUTS, w_refs, o_refs):
        step = min(width, MXU_DIM)
        for c0 in range(0, width, step):
            y = _dot(h, w_ref[:, c0:c0 + step])
            for s0 in range(0, step, LANES):
                o_ref[:, c0 + s0:c0 + s0 + LANES] = epilogue(y[:, s0:s0 + LANES], kind, g_idx).astype(dtype)


def _proj_weights(w_in):
    splits = (256, 64, 64, 512, 64, 8, 256, 256, 256, 512, 512, 512, 1024, 1024, 1024)
    offs = [0]
    for s in splits:
        offs.append(offs[-1] + s)
    qa, ka, va, qi, ki, wi, qb, kb, vb, qc, kc, vc, ga, gb, gc = [w_in[:, offs[i]:offs[i + 1]] for i in range(len(splits))]
    d = w_in.shape[0]
    out = dict(qa=qa, ka=jnp.tile(ka, (1, A_HEADS)), va=jnp.tile(va, (1, A_HEADS)), qi=qi,
               ki=jnp.tile(ki, (1, 4)), wi=jnp.concatenate([wi, jnp.zeros((d, LANES - IDX_HEADS), w_in.dtype)], axis=1),
               qb=qb, kb=kb, vb=vb, qc=qc, kc=kc, vc=vc, gate=jnp.concatenate([ga, gb, gc], axis=1))
    return [out[name].astype(BF16) for name, *_ in _PROJ_OUTPUTS]


def _proj(x, g, qkg, cos, sin, weights, tm):
    n, d = x.shape
    row = lambda w: pl.BlockSpec((tm, w), lambda i: (i, 0))
    return pl.pallas_call(
        _proj_kernel,
        grid=(n // tm,),
        in_specs=[row(d), _resident((1, d)), _resident(qkg.shape), row(LANES), row(LANES)]
                 + [_resident(w.shape) for w in weights],
        out_specs=[row(width) for _, width, *_ in _PROJ_OUTPUTS],
        out_shape=[jax.ShapeDtypeStruct((n, width), dtype) for _, width, _, _, dtype in _PROJ_OUTPUTS],
        compiler_params=_params("parallel"),
        name="in_proj",
    )(x, g, qkg, cos, sin, *weights)


def _softmax_step(s, m, l, acc, v, lanes=None):
    m_new = jnp.maximum(m, jnp.max(s, axis=-1, keepdims=True))
    alpha = jnp.exp(m - m_new)
    p = jnp.exp(s - m_new)
    l_new = alpha * l + jnp.sum(p, axis=-1, keepdims=True)
    upd = alpha * acc + _dot(p.astype(BF16), v)
    acc_new = upd if lanes is None else jnp.where(lanes, upd, acc)
    return m_new, l_new, acc_new


def _head_lane_masks(rows, width):
    lane = lax.broadcasted_iota(jnp.int32, (rows, width), 1)
    return [(lane >= h * HEAD_DIM) & (lane < (h + 1) * HEAD_DIM) for h in range(width // HEAD_DIM)]


def _causal_bias(q0, k0, tq, tk):
    row = q0 + lax.broadcasted_iota(jnp.int32, (tq, tk), 0)
    col = k0 + lax.broadcasted_iota(jnp.int32, (tq, tk), 1)
    return col <= row


def _diff_kernel(lam_ref, g_ref, q_ref, k_ref, v_ref, o_ref, *, lam_init, t):
    qi = pl.program_id(2)
    q = q_ref[0]
    lane = lax.broadcasted_iota(jnp.int32, q.shape, 1)
    zero = jnp.zeros_like(q)
    qs = (jnp.where(lane < HEAD_DIM, q, zero), jnp.where(lane >= HEAD_DIM, q, zero))

    def tile(j, carry, masked):
        k = k_ref[0, pl.ds(pl.multiple_of(j * t, t), t), :]
        v = v_ref[0, pl.ds(pl.multiple_of(j * t, t), t), :]
        out = []
        for c in range(2):
            s = _nt_dot(qs[c], k)
            if masked:
                s = jnp.where(_causal_bias(0, 0, t, t), s, MASKED)
            out.extend(_softmax_step(s, *carry[3 * c:3 * c + 3], v))
        return tuple(out)

    init = (jnp.full((t, 1), MASKED, F32), jnp.zeros((t, 1), F32), jnp.zeros((t, C_VDIM), F32)) * 2
    carry = tile(qi, init, True)
    carry = lax.fori_loop(0, qi, lambda j, cr: tile(j, cr, False), carry)
    _, l0, a0, _, l1, a1 = carry

    lp = lam_ref[...]
    lam = (jnp.exp(jnp.sum(lp[0:1] * lp[1:2], axis=-1, keepdims=True))
           - jnp.exp(jnp.sum(lp[2:3] * lp[3:4], axis=-1, keepdims=True)) + lam_init)
    o = a0 / l0 - lam * (a1 / l1)
    o = o * lax.rsqrt(jnp.mean(o * o, axis=-1, keepdims=True) + EPS) * g_ref[...] * (1.0 - lam_init)
    o_ref[0] = o.astype(o_ref.dtype)


def _diff_attention(qc, kc, vc, lam_p, subln_g, lam_init):
    b, s, w = qc.shape
    t = ATT_TILE
    qspec = pl.BlockSpec((1, t, C_VDIM), lambda bi, h, i: (bi, i, h))
    kvspec = pl.BlockSpec((1, s, C_VDIM), lambda bi, h, i: (bi, 0, h))
    return pl.pallas_call(
        functools.partial(_diff_kernel, lam_init=lam_init, t=t),
        grid=(b, C_HEADS, s // t),
        in_specs=[pl.BlockSpec(lam_p.shape, lambda bi, h, i: (0, 0)),
                  pl.BlockSpec((1, C_VDIM), lambda bi, h, i: (0, 0)), qspec, kvspec, kvspec],
        out_specs=qspec,
        out_shape=jax.ShapeDtypeStruct((b, s, w), BF16),
        compiler_params=_params("parallel", "parallel", "arbitrary"),
        name="diff_attention",
    )(lam_p, subln_g, qc, kc, vc)


def _moba_kernel(q_ref, k_ref, v_ref, o_ref, kmean_hi, kmean_lo, *, t):
    qi = pl.program_id(1)
    n_blk = k_ref.shape[1] // t
    width = q_ref.shape[2]

    @pl.when(qi == 0)
    def _():
        kmean_hi[...] = jnp.zeros_like(kmean_hi)
        kmean_lo[...] = jnp.zeros_like(kmean_lo)
        for n in range(n_blk):
            km = jnp.mean(k_ref[0, n * t:(n + 1) * t, :].astype(F32), axis=0, keepdims=True)
            hi = km.astype(BF16)
            kmean_hi[n:n + 1, :] = hi
            kmean_lo[n:n + 1, :] = (km - hi.astype(F32)).astype(BF16)

    q = q_ref[0]
    heads = _head_lane_masks(t, width)
    zero = jnp.zeros_like(q)
    qs = [jnp.where(hm, q, zero) for hm in heads]
    blk = lax.broadcasted_iota(jnp.int32, (t, LANES), 1)
    past = blk < qi

    sel_bias = []
    for h in range(B_HEADS):
        gate = _nt_dot(qs[h], kmean_hi[...]) + _nt_dot(qs[h], kmean_lo[...])
        gate = jnp.where(past, gate, -jnp.inf)
        rank = jnp.zeros((t, LANES), jnp.int32)
        for n in range(n_blk):
            gn = jnp.max(jnp.where(blk == n, gate, -jnp.inf), axis=-1, keepdims=True)
            beats = (gn > gate) | ((gn == gate) & (blk > n))
            rank = rank + beats.astype(jnp.int32)
        sel_bias.append(jnp.where((rank < MOBA_TOPK) & past, 0.0, MASKED))

    def step(j, carry, bias_of):
        k = k_ref[0, pl.ds(pl.multiple_of(j * t, t), t), :]
        v = v_ref[0, pl.ds(pl.multiple_of(j * t, t), t), :]
        ms, ls, acc = carry
        ms, ls = list(ms), list(ls)
        for h in range(B_HEADS):
            s = _nt_dot(qs[h], k) + bias_of(h)
            ms[h], ls[h], acc = _softmax_step(s, ms[h], ls[h], acc, v, heads[h])
        return tuple(ms), tuple(ls), acc

    causal = jnp.where(_causal_bias(0, 0, t, t), 0.0, MASKED)
    init = ((jnp.full((t, 1), MASKED, F32),) * B_HEADS, (jnp.zeros((t, 1), F32),) * B_HEADS,
            jnp.zeros((t, width), F32))
    carry = step(qi, init, lambda h: causal)

    def past_block(j, carry):
        return step(j, carry, lambda h: jnp.sum(jnp.where(blk == j, sel_bias[h], 0.0), axis=-1, keepdims=True))

    _, ls, acc = lax.fori_loop(0, qi, past_block, carry)
    inv = jnp.zeros((t, width), F32)
    for h in range(B_HEADS):
        inv = jnp.where(heads[h], 1.0 / ls[h], inv)
    o_ref[0] = (acc * inv).astype(o_ref.dtype)


def _moba_attention(qb, kb, vb):
    b, s, w = qb.shape
    t = MOBA_BLOCK
    qspec = pl.BlockSpec((1, t, w), lambda bi, i: (bi, i, 0))
    kvspec = pl.BlockSpec((1, s, w), lambda bi, i: (bi, 0, 0))
    return pl.pallas_call(
        functools.partial(_moba_kernel, t=t),
        grid=(b, s // t),
        in_specs=[qspec, kvspec, kvspec],
        out_specs=qspec,
        out_shape=jax.ShapeDtypeStruct((b, s, w), BF16),
        scratch_shapes=[pltpu.VMEM((LANES, w), BF16), pltpu.VMEM((LANES, w), BF16)],
        compiler_params=_params("parallel", "arbitrary"),
        name="moba_attention",
    )(qb, kb, vb)


def _dsa_kernel(q_ref, k_ref, v_ref, qi_ref, ki_ref, wi_ref, o_ref, keys_ref, *, t, topk):
    it = pl.program_id(1)
    width = q_ref.shape[2]
    heads = _head_lane_masks(t, width)
    q0 = it * t

    qidx = qi_ref[0]
    zero = jnp.zeros((t, width), qidx.dtype)
    qih = [jnp.where(heads[h % 4], qidx[:, (h // 4) * width:(h // 4 + 1) * width], zero) for h in range(IDX_HEADS)]
    wi = wi_ref[0]

    def score_tile(j, _):
        kt = ki_ref[0, pl.ds(pl.multiple_of(j * t, t), t), :]
        isc = jnp.zeros((t, t), F32)
        for h in range(IDX_HEADS):
            isc = isc + jnp.maximum(_nt_dot(qih[h], kt), 0.0) * wi[:, h:h + 1]
        isc = jnp.where(_causal_bias(q0, j * t, t, t), isc + 0.0, -jnp.inf)
        bits = lax.bitcast_convert_type(isc, jnp.int32)
        keys_ref[j] = bits ^ ((bits >> 31) & 0x7FFFFFFF)
        return 0

    lax.fori_loop(0, it + 1, score_tile, 0)

    def count_ge(cand):
        def body(j, cnt):
            ge = (keys_ref[j] >= cand).astype(jnp.int32)
            return cnt + ge[:, :LANES] + ge[:, LANES:]
        cnt = lax.fori_loop(0, it + 1, body, jnp.zeros((t, LANES), jnp.int32))
        return jnp.sum(cnt, axis=-1, keepdims=True)

    thr = jnp.full((t, 1), INT_MIN, jnp.int32)
    thr = jnp.where(count_ge(jnp.zeros((t, 1), jnp.int32)) >= topk, 0, thr)

    def bit_step(i, thr):
        cand = thr | (1 << (30 - i))
        return jnp.where(count_ge(cand) >= topk, cand, thr)

    thr = lax.fori_loop(0, 31, bit_step, thr)

    def count_gt(thr):
        def body(j, cnt):
            gt = (keys_ref[j] > thr).astype(jnp.int32)
            return cnt + gt[:, :LANES] + gt[:, LANES:]
        cnt = lax.fori_loop(0, it + 1, body, jnp.zeros((t, LANES), jnp.int32))
        return jnp.sum(cnt, axis=-1, keepdims=True)

    ties_wanted = (topk - count_gt(thr)).astype(F32)

    q = q_ref[0]
    qs = [jnp.where(hm, q, jnp.zeros_like(q)) for hm in heads]
    r = lax.broadcasted_iota(jnp.int32, (t, t), 0)
    c = lax.broadcasted_iota(jnp.int32, (t, t), 1)
    before = (r < c).astype(BF16)

    def attend(j, carry):
        ms, ls, acc, ties_seen = carry
        key = keys_ref[j]
        tie = key == thr
        tie_f = tie.astype(F32)
        tie_rank = ties_seen + _dot(tie_f.astype(BF16), before)
        chosen = ((key > thr) | (tie & (tie_rank < ties_wanted))) & _causal_bias(q0, j * t, t, t)
        bias = jnp.where(chosen, 0.0, MASKED)
        k = k_ref[0, pl.ds(pl.multiple_of(j * t, t), t), :]
        v = v_ref[0, pl.ds(pl.multiple_of(j * t, t), t), :]
        ms, ls = list(ms), list(ls)
        for h in range(A_HEADS):
            s = _nt_dot(qs[h], k) + bias
            ms[h], ls[h], acc = _softmax_step(s, ms[h], ls[h], acc, v, heads[h])
        return tuple(ms), tuple(ls), acc, ties_seen + jnp.sum(tie_f, axis=-1, keepdims=True)

    init = ((jnp.full((t, 1), MASKED, F32),) * A_HEADS, (jnp.zeros((t, 1), F32),) * A_HEADS,
            jnp.zeros((t, width), F32), jnp.zeros((t, 1), F32))
    _, ls, acc, _ = lax.fori_loop(0, it + 1, attend, init)
    inv = jnp.zeros((t, width), F32)
    for h in range(A_HEADS):
        inv = jnp.where(heads[h], 1.0 / ls[h], inv)
    o_ref[0] = (acc * inv).astype(o_ref.dtype)


def _dsa_attention(qa, ka, va, qi, ki, wi):
    b, s, w = qa.shape
    t = ATT_TILE
    topk = min(INDEX_TOPK, s // 4)
    row = lambda width: pl.BlockSpec((1, t, width), lambda bi, i: (bi, i, 0))
    full = pl.BlockSpec((1, s, w), lambda bi, i: (bi, 0, 0))
    return pl.pallas_call(
        functools.partial(_dsa_kernel, t=t, topk=topk),
        grid=(b, s // t),
        in_specs=[row(w), full, full, row(qi.shape[2]), full, row(LANES)],
        out_specs=row(w),
        out_shape=jax.ShapeDtypeStruct((b, s, w), BF16),
        scratch_shapes=[pltpu.VMEM((s // t, t, t), jnp.int32)],
        compiler_params=_params("parallel", "arbitrary"),
        name="dsa_attention",
    )(qa, ka, va, qi, ki, wi)


def _merge_kernel(x_ref, oa_ref, ob_ref, oc_ref, gate_ref, wb_ref, wo_ref, o_ref):
    d = x_ref.shape[1]
    wa, wbw = oa_ref.shape[1], ob_ref.shape[1]
    merged = gate_ref[:, 0:d].astype(F32) * _dot(oa_ref[...], wb_ref[0:wa, :])
    merged = merged + gate_ref[:, d:2 * d].astype(F32) * _dot(ob_ref[...], wb_ref[wa:wa + wbw, :])
    merged = merged + gate_ref[:, 2 * d:3 * d].astype(F32) * _dot(oc_ref[...], wb_ref[wa + wbw:, :])
    o_ref[...] = x_ref[...] + _dot(merged.astype(BF16), wo_ref[...])


def _merge(x, oa, ob, oc, gate, wb, wo, tm):
    n, d = x.shape
    row = lambda w: pl.BlockSpec((tm, w), lambda i: (i, 0))
    return pl.pallas_call(
        _merge_kernel,
        grid=(n // tm,),
        in_specs=[row(d), row(oa.shape[1]), row(ob.shape[1]), row(oc.shape[1]), row(gate.shape[1]),
                  _resident(wb.shape), _resident(wo.shape)],
        out_specs=row(d),
        out_shape=jax.ShapeDtypeStruct((n, d), F32),
        compiler_params=_params("parallel"),
        name="merge_out",
    )(x, oa, ob, oc, gate, wb, wo)


def kernel(x, positions, norm_g, w_in, qk_norm_g, lambda_params, diff_subln_g, w_branch, w_out,
           ffn_w_gate, ffn_w_up, ffn_w_down):
    b, s, d = x.shape
    n = b * s
    depth = norm_g.shape[0]
    tm = min(512, n)
    assert n % tm == 0 and s % ATT_TILE == 0 and ATT_TILE == MOBA_BLOCK

    cos, sin = _rope_tables(positions, tm)
    xf = x.reshape(n, d)
    for layer in range(depth):
        lam_init = 0.8 - 0.6 * math.exp(-0.3 * layer)
        ffn_w = lambda i: (ffn_w_gate[layer, i].astype(BF16), ffn_w_up[layer, i].astype(BF16),
                           ffn_w_down[layer, i].astype(BF16))
        xf = _ffn(xf, norm_g[layer, 0][None, :], *ffn_w(0), tm)

        qkg = jnp.tile(qk_norm_g[layer], (1, LANES // HEAD_DIM))
        outs = _proj(xf, norm_g[layer, 1][None, :], qkg, cos, sin, _proj_weights(w_in[layer]), tm)
        p = {name: o.reshape(b, s, o.shape[1]) for (name, *_), o in zip(_PROJ_OUTPUTS, outs)}
        o_a = _dsa_attention(p["qa"], p["ka"], p["va"], p["qi"], p["ki"], p["wi"])
        o_b = _moba_attention(p["qb"], p["kb"], p["vb"])
        o_c = _diff_attention(p["qc"], p["kc"], p["vc"], lambda_params[layer], diff_subln_g[layer][None, :], lam_init)
        xf = _merge(xf, o_a.reshape(n, -1), o_b.reshape(n, -1), o_c.reshape(n, -1), p["gate"].reshape(n, -1),
                    w_branch[layer].astype(BF16), w_out[layer].astype(BF16), tm)

        xf = _ffn(xf, norm_g[layer, 2][None, :], *ffn_w(1), tm)
    return xf.reshape(b, s, d)
```

```python
import functools
import math

import jax
import jax.numpy as jnp
from jax import lax
from jax.experimental import pallas as pl
from jax.experimental.pallas import tpu as pltpu

F32 = jnp.float32
BF16 = jnp.bfloat16

HEAD_DIM = 64
ROT_DIM = HEAD_DIM // 4
ROT_HALF = ROT_DIM // 2
ROPE_THETA = 500000.0
EPS = 1e-6
A_HEADS = 4
IDX_HEADS = 8
IDX_DIM = 64
INDEX_TOPK = 256
B_HEADS = 4
MOBA_BLOCK = 256
MOBA_TOPK = 3
C_HEADS = 4
C_VDIM = 2 * HEAD_DIM

LANES = 128
SUBLANES = 8
MXU_DIM = 256
ATT_TILE = 256
MASKED = -1e30
INT_MIN = -(2 ** 31)
VMEM_LIMIT = 56 * 1024 * 1024
LOG2E = 1.4426950408889634


def _nt_dot(a, b):
    return lax.dot_general(a, b, (((1,), (1,)), ((), ())), preferred_element_type=F32)


def _dot(a, b):
    return jnp.dot(a, b, preferred_element_type=F32)


def _split_dot(x, w):
    hi = x.astype(BF16)
    lo = (x - hi.astype(F32)).astype(BF16)
    return _dot(hi, w) + _dot(lo, w)


def _resident(shape):
    nd = len(shape)
    return pl.BlockSpec(shape, lambda *_: (0,) * nd, pipeline_mode=pl.Buffered(1))


def _params(*sem):
    return pltpu.CompilerParams(dimension_semantics=sem, vmem_limit_bytes=VMEM_LIMIT)


def _rope_table_kernel(pos_ref, freq_ref, cos_ref, sin_ref):
    ang = pos_ref[...].astype(F32) * freq_ref[...]
    lane = lax.broadcasted_iota(jnp.int32, ang.shape, 1) % HEAD_DIM
    c = jnp.cos(ang)
    s = jnp.sin(ang)
    cos_ref[...] = jnp.where(lane < ROT_DIM, c, 1.0)
    sin_ref[...] = jnp.where(lane < ROT_HALF, -s, jnp.where(lane < ROT_DIM, s, 0.0))


def _rope_tables(positions, tm):
    n = positions.size
    pos = positions.reshape(n, 1)
    inv_freq = jnp.power(ROPE_THETA, -jnp.arange(0, ROT_DIM, 2, dtype=F32) / ROT_DIM)
    freq = jnp.tile(jnp.concatenate([inv_freq, inv_freq, jnp.zeros((HEAD_DIM - ROT_DIM,), F32)]), 2)[None, :]
    return pl.pallas_call(
        _rope_table_kernel,
        grid=(n // tm,),
        in_specs=[pl.BlockSpec((tm, 1), lambda i: (i, 0)), pl.BlockSpec((1, LANES), lambda i: (0, 0))],
        out_specs=[pl.BlockSpec((tm, LANES), lambda i: (i, 0))] * 2,
        out_shape=[jax.ShapeDtypeStruct((n, LANES), F32)] * 2,
        compiler_params=_params("parallel"),
        name="rope_tables",
    )(pos, freq)


def _rmsnorm_rows(x, g):
    return x * lax.rsqrt(jnp.mean(x * x, axis=-1, keepdims=True) + EPS) * g


def _ffn_kernel(x_ref, g_ref, wg_ref, wu_ref, wd_ref, o_ref, *, chunk):
    x = x_ref[...]
    h = _rmsnorm_rows(x, g_ref[...]).astype(BF16)
    acc = jnp.zeros(x.shape, F32)
    for c in range(0, wg_ref.shape[1], chunk):
        a = _dot(h, wg_ref[:, c:c + chunk])
        b = _dot(h, wu_ref[:, c:c + chunk])
        t = (a * jax.nn.sigmoid(a) * b).astype(BF16)
        acc = acc + _dot(t, wd_ref[c:c + chunk, :])
    o_ref[...] = x + 0.5 * acc


def _ffn(x, g, wg, wu, wd, tm):
    n, d = x.shape
    f = wg.shape[1]
    return pl.pallas_call(
        functools.partial(_ffn_kernel, chunk=MXU_DIM),
        grid=(n // tm,),
        in_specs=[pl.BlockSpec((tm, d), lambda i: (i, 0)), _resident((1, d)),
                  _resident((d, f)), _resident((d, f)), _resident((f, d))],
        out_specs=pl.BlockSpec((tm, d), lambda i: (i, 0)),
        out_shape=jax.ShapeDtypeStruct((n, d), F32),
        compiler_params=_params("parallel"),
        name="ffn",
    )(x, g, wg, wu, wd)


_PROJ_OUTPUTS = (
    ("qa", 256, "norm_rope_scale", 0, BF16),
    ("ka", 256, "norm_rope", 1, BF16),
    ("va", 128, "plain", None, BF16),
    ("qi", 512, "rope", None, BF16),
    ("ki", 256, "rope", None, BF16),
    ("wi", 128, "index_weight", None, F32),
    ("qb", 256, "norm_rope_scale", 2, BF16),
    ("kb", 256, "norm_rope", 3, BF16),
    ("vb", 256, "plain", None, BF16),
    ("qc", 512, "norm_rope_scale", 4, BF16),
    ("kc", 512, "norm_rope", 5, BF16),
    ("vc", 512, "plain", None, BF16),
    ("gate", 3072, "sigmoid", None, BF16),
)
_Q_SCALE = HEAD_DIM ** -0.5 * LOG2E


def _proj_kernel(x_ref, g_ref, qkg_ref, cos_ref, sin_ref, *refs):
    n_out = len(_PROJ_OUTPUTS)
    w_refs, o_refs = refs[:n_out], refs[n_out:]
    tm = x_ref.shape[0]
    h = _rmsnorm_rows(x_ref[...], g_ref[...]).astype(BF16)
    cos = cos_ref[...]
    sin = sin_ref[...]
    lane = lax.broadcasted_iota(jnp.int32, (tm, LANES), 1)
    take_upper = (lane % HEAD_DIM) < ROT_HALF
    r = lax.broadcasted_iota(jnp.int32, (LANES, LANES), 0) // HEAD_DIM
    c = lax.broadcasted_iota(jnp.int32, (LANES, LANES), 1) // HEAD_DIM
    same_head = (r == c).astype(BF16)

    def epilogue(y, kind, g_idx):
        if kind in ("norm_rope_scale", "norm_rope"):
            ssq = _split_dot(y * y, same_head)
            y = y * lax.rsqrt(ssq * (1.0 / HEAD_DIM) + EPS) * qkg_ref[g_idx:g_idx + 1, :]
        if kind in ("norm_rope_scale", "norm_rope", "rope"):
            partner = jnp.where(take_upper, pltpu.roll(y, LANES - ROT_HALF, 1), pltpu.roll(y, ROT_HALF, 1))
            y = y * cos + partner * sin
        if kind == "norm_rope_scale":
            y = y * _Q_SCALE
        if kind == "index_weight":
            y = y * ((IDX_HEADS * IDX_DIM) ** -0.5)
        if kind == "sigmoid":
            y = jax.nn.sigmoid(y)
        return y

    for (name, width, kind, g_idx, dtype), w_ref, o_ref in zip(_PROJ_OUTPUTS, w_refs, o_refs):
        step = min(width, MXU_DIM)
        for c0 in range(0, width, step):
            y = _dot(h, w_ref[:, c0:c0 + step])
            for s0 in range(0, step, LANES):
                o_ref[:, c0 + s0:c0 + s0 + LANES] = epilogue(y[:, s0:s0 + LANES], kind, g_idx).astype(dtype)


def _proj_weights(w_in):
    splits = (256, 64, 64, 512, 64, 8, 256, 256, 256, 512, 512, 512, 1024, 1024, 1024)
    offs = [0]
    for s in splits:
        offs.append(offs[-1] + s)
    qa, ka, va, qi, ki, wi, qb, kb, vb, qc, kc, vc, ga, gb, gc = [w_in[:, offs[i]:offs[i + 1]] for i in range(len(splits))]
    d = w_in.shape[0]
    out = dict(qa=qa, ka=jnp.tile(ka, (1, A_HEADS)), va=jnp.tile(va, (1, LANES // HEAD_DIM)), qi=qi,
               ki=jnp.tile(ki, (1, 4)), wi=jnp.concatenate([wi, jnp.zeros((d, LANES - IDX_HEADS), w_in.dtype)], axis=1),
               qb=qb, kb=kb, vb=vb, qc=qc, kc=kc, vc=vc, gate=jnp.concatenate([ga, gb, gc], axis=1))
    return [out[name].astype(BF16) for name, *_ in _PROJ_OUTPUTS]


def _proj(x, g, qkg, cos, sin, weights, tm):
    n, d = x.shape
    row = lambda w: pl.BlockSpec((tm, w), lambda i: (i, 0))
    return pl.pallas_call(
        _proj_kernel,
        grid=(n // tm,),
        in_specs=[row(d), _resident((1, d)), _resident(qkg.shape), row(LANES), row(LANES)]
                 + [_resident(w.shape) for w in weights],
        out_specs=[row(width) for _, width, *_ in _PROJ_OUTPUTS],
        out_shape=[jax.ShapeDtypeStruct((n, width), dtype) for _, width, _, _, dtype in _PROJ_OUTPUTS],
        compiler_params=_params("parallel"),
        name="in_proj",
    )(x, g, qkg, cos, sin, *weights)


def _lane_fold(x, op):
    out = x[:, :LANES]
    for c in range(LANES, x.shape[1], LANES):
        out = op(out, x[:, c:c + LANES])
    return out


def _lane_tile(x, width):
    return jnp.concatenate([x] * (width // LANES), axis=1)


def _reset_stats(max_ref, sum_ref, acc_ref):
    max_ref[...] = jnp.full(max_ref.shape, MASKED, F32)
    sum_ref[...] = jnp.zeros(sum_ref.shape, F32)
    acc_ref[...] = jnp.zeros(acc_ref.shape, F32)


def _record_scores(s, j, r0, s_ref, max_ref):
    n = s.shape[0]
    s_ref[j, r0:r0 + n, :] = s
    max_ref[r0:r0 + n, :] = jnp.maximum(max_ref[r0:r0 + n, :], _lane_fold(s, jnp.maximum))


def _finish_max(max_ref):
    m = max_ref[...]
    max_ref[...] = jnp.broadcast_to(jnp.max(m, axis=-1, keepdims=True), m.shape)


def _accumulate_probs(j, r0, n, v, s_ref, max_ref, sum_ref, acc_ref):
    s = s_ref[j, r0:r0 + n, :]
    p = jnp.exp2(s - _lane_tile(max_ref[r0:r0 + n, :], s.shape[1]))
    sum_ref[r0:r0 + n, :] += _lane_fold(p, jnp.add)
    acc_ref[r0:r0 + n, :] += _dot(p.astype(BF16), v)


def _normalised(r0, n, sum_ref, acc_ref):
    return acc_ref[r0:r0 + n, :] / jnp.sum(sum_ref[r0:r0 + n, :], axis=-1, keepdims=True)


def _head_lane_masks(rows, width):
    lane = lax.broadcasted_iota(jnp.int32, (rows, width), 1)
    return [(lane >= h * HEAD_DIM) & (lane < (h + 1) * HEAD_DIM) for h in range(width // HEAD_DIM)]


def _stack_masked(q, masks):
    zero = jnp.zeros_like(q)
    return jnp.concatenate([jnp.where(m, q, zero) for m in masks], axis=0)


def _local_causal(t):
    row = lax.broadcasted_iota(jnp.int32, (t, t), 0)
    col = lax.broadcasted_iota(jnp.int32, (t, t), 1)
    return col <= row


def _tile_rows(j, t):
    return pl.ds(pl.multiple_of(j * t, t), t)


def _interleave_pairs(o, t):
    lane = lax.broadcasted_iota(jnp.int32, (t, LANES), 1)
    low = lane < HEAD_DIM
    return jnp.concatenate([jnp.where(low, o[0], o[1]), jnp.where(low, o[2], o[3])], axis=1)


_ATT_SCRATCH = lambda n_tiles, rows, t: [
    pltpu.VMEM((n_tiles, rows, t), F32),
    pltpu.VMEM((rows, LANES), F32),
    pltpu.VMEM((rows, LANES), F32),
    pltpu.VMEM((rows, LANES), F32),
]


def _diff_kernel(lam_ref, g_ref, q_ref, k_ref, v_ref, o_ref, s_ref, max_ref, sum_ref, acc_ref, *, lam_init, t):
    it = pl.program_id(1)
    lane = lax.broadcasted_iota(jnp.int32, (t, C_VDIM), 1)
    maps = [lane < HEAD_DIM, lane >= HEAD_DIM]
    q_all = q_ref[0]
    qs = [_stack_masked(q_all[:, h * C_VDIM:(h + 1) * C_VDIM], maps) for h in range(C_HEADS)]
    _reset_stats(max_ref, sum_ref, acc_ref)
    causal = jnp.where(_local_causal(t), 0.0, MASKED)
    causal2 = jnp.concatenate([causal, causal], axis=0)

    def scores(j, bias):
        rows = _tile_rows(j, t)
        for h in range(C_HEADS):
            s = _nt_dot(qs[h], k_ref[0, rows, h * C_VDIM:(h + 1) * C_VDIM])
            if bias is not None:
                s = s + bias
            _record_scores(s, j, h * 2 * t, s_ref, max_ref)

    @pl.loop(0, it)
    def _(j):
        scores(j, None)

    scores(it, causal2)
    _finish_max(max_ref)

    @pl.loop(0, it + 1)
    def _(j):
        rows = _tile_rows(j, t)
        for h in range(C_HEADS):
            _accumulate_probs(j, h * 2 * t, 2 * t, v_ref[0, rows, h * C_VDIM:(h + 1) * C_VDIM],
                              s_ref, max_ref, sum_ref, acc_ref)

    lp = lam_ref[...]
    lam = (jnp.exp(jnp.sum(lp[0:1] * lp[1:2], axis=-1, keepdims=True))
           - jnp.exp(jnp.sum(lp[2:3] * lp[3:4], axis=-1, keepdims=True)) + lam_init)
    for h in range(C_HEADS):
        o = _normalised(h * 2 * t, t, sum_ref, acc_ref) - lam * _normalised(h * 2 * t + t, t, sum_ref, acc_ref)
        o = o * lax.rsqrt(jnp.mean(o * o, axis=-1, keepdims=True) + EPS) * g_ref[...] * (1.0 - lam_init)
        o_ref[0, :, h * C_VDIM:(h + 1) * C_VDIM] = o.astype(o_ref.dtype)


def _diff_attention(qc, kc, vc, lam_p, subln_g, lam_init):
    b, s, w = qc.shape
    t = ATT_TILE
    qspec = pl.BlockSpec((1, t, w), lambda bi, i: (bi, i, 0))
    kvspec = pl.BlockSpec((1, s, w), lambda bi, i: (bi, 0, 0))
    return pl.pallas_call(
        functools.partial(_diff_kernel, lam_init=lam_init, t=t),
        grid=(b, s // t),
        in_specs=[pl.BlockSpec(lam_p.shape, lambda bi, i: (0, 0)),
                  pl.BlockSpec((1, C_VDIM), lambda bi, i: (0, 0)), qspec, kvspec, kvspec],
        out_specs=qspec,
        out_shape=jax.ShapeDtypeStruct((b, s, w), BF16),
        scratch_shapes=_ATT_SCRATCH(s // t, 2 * C_HEADS * t, t),
        compiler_params=_params("parallel", "arbitrary"),
        name="diff_attention",
    )(lam_p, subln_g, qc, kc, vc)


def _moba_kernel(q_ref, k_ref, v_ref, o_ref, kmean_hi, kmean_lo, bias_ref, s_ref, max_ref, sum_ref, acc_ref, *, t):
    it = pl.program_id(1)
    n_blk = k_ref.shape[1] // t
    width = q_ref.shape[2]

    @pl.when(it == 0)
    def _():
        kmean_hi[...] = jnp.zeros_like(kmean_hi)
        kmean_lo[...] = jnp.zeros_like(kmean_lo)
        for n in range(n_blk):
            km = jnp.mean(k_ref[0, n * t:(n + 1) * t, :].astype(F32), axis=0, keepdims=True)
            hi = km.astype(BF16)
            kmean_hi[n:n + 1, :] = hi
            kmean_lo[n:n + 1, :] = (km - hi.astype(F32)).astype(BF16)

    heads = _head_lane_masks(t, width)
    q_stack = _stack_masked(q_ref[0], heads)
    blk = lax.broadcasted_iota(jnp.int32, (t, LANES), 1)
    past = blk < it
    spread = (lax.broadcasted_iota(jnp.int32, (LANES, n_blk * LANES), 1) // LANES
              == lax.broadcasted_iota(jnp.int32, (LANES, n_blk * LANES), 0)).astype(BF16)

    for h in range(B_HEADS):
        qh = q_stack[h * t:(h + 1) * t]
        gate = _nt_dot(qh, kmean_hi[...]) + _nt_dot(qh, kmean_lo[...])
        gate = jnp.where(past, gate, -jnp.inf)
        rank = jnp.zeros((t, LANES), jnp.int32)
        for n in range(n_blk):
            gn = jnp.max(jnp.where(blk == n, gate, -jnp.inf), axis=-1, keepdims=True)
            rank = rank + jnp.where((gn > gate) | ((gn == gate) & (blk > n)), 1, 0)
        bias = jnp.where((rank < MOBA_TOPK) & past, 0.0, MASKED).astype(BF16)
        spread_bias = _dot(bias, spread)
        for n in range(n_blk):
            bias_ref[n, h * t:(h + 1) * t, :] = spread_bias[:, n * LANES:(n + 1) * LANES]

    _reset_stats(max_ref, sum_ref, acc_ref)

    @pl.loop(0, it)
    def _(j):
        s = _nt_dot(q_stack, k_ref[0, _tile_rows(j, t), :]) + _lane_tile(bias_ref[j], t)
        _record_scores(s, j, 0, s_ref, max_ref)

    causal = jnp.where(_local_causal(t), 0.0, MASKED)
    s = _nt_dot(q_stack, k_ref[0, _tile_rows(it, t), :]) + jnp.concatenate([causal] * B_HEADS, axis=0)
    _record_scores(s, it, 0, s_ref, max_ref)
    _finish_max(max_ref)

    @pl.loop(0, it + 1)
    def _(j):
        rows = _tile_rows(j, t)
        for pair in range(B_HEADS // 2):
            _accumulate_probs(j, pair * 2 * t, 2 * t, v_ref[0, rows, pair * LANES:(pair + 1) * LANES],
                              s_ref, max_ref, sum_ref, acc_ref)

    o = [_normalised(h * t, t, sum_ref, acc_ref) for h in range(B_HEADS)]
    o_ref[0] = _interleave_pairs(o, t).astype(o_ref.dtype)


def _moba_attention(qb, kb, vb):
    b, s, w = qb.shape
    t = MOBA_BLOCK
    qspec = pl.BlockSpec((1, t, w), lambda bi, i: (bi, i, 0))
    kvspec = pl.BlockSpec((1, s, w), lambda bi, i: (bi, 0, 0))
    return pl.pallas_call(
        functools.partial(_moba_kernel, t=t),
        grid=(b, s // t),
        in_specs=[qspec, kvspec, kvspec],
        out_specs=qspec,
        out_shape=jax.ShapeDtypeStruct((b, s, w), BF16),
        scratch_shapes=[pltpu.VMEM((LANES, w), BF16), pltpu.VMEM((LANES, w), BF16),
                        pltpu.VMEM((s // t, B_HEADS * t, LANES), F32)] + _ATT_SCRATCH(s // t, B_HEADS * t, t),
        compiler_params=_params("parallel", "arbitrary"),
        name="moba_attention",
    )(qb, kb, vb)


def _sublane_fold(x):
    parts = [x[r:r + SUBLANES] for r in range(0, x.shape[0], SUBLANES)]
    while len(parts) > 1:
        parts = [parts[i] + parts[i + 1] for i in range(0, len(parts), 2)]
    return parts[0]


def _dsa_kernel(q_ref, k_ref, v_ref, qi_ref, ki_ref, wi_ref, o_ref,
                keys_ref, keys_t_ref, s_ref, max_ref, sum_ref, acc_ref, *, t, topk):
    it = pl.program_id(1)
    width = q_ref.shape[2]
    heads = _head_lane_masks(t, width)
    causal = _local_causal(t)

    qidx = qi_ref[0]
    qi_stack = jnp.concatenate(
        [_stack_masked(qidx[:, g * width:(g + 1) * width], heads) for g in range(IDX_HEADS // 4)], axis=0)
    wi = wi_ref[0]
    w_rep = [jnp.broadcast_to(wi[:, h:h + 1], (t, LANES)) for h in range(IDX_HEADS)]

    def score_tile(j, diagonal):
        logit = _nt_dot(qi_stack, ki_ref[0, _tile_rows(j, t), :])
        cols = []
        for c in range(0, t, LANES):
            acc = jnp.zeros((t, LANES), F32)
            for h in range(IDX_HEADS):
                acc = acc + jnp.maximum(logit[h * t:(h + 1) * t, c:c + LANES], 0.0) * w_rep[h]
            cols.append(acc)
        isc = jnp.concatenate(cols, axis=1) + 0.0
        if diagonal:
            isc = jnp.where(causal, isc, -jnp.inf)
        bits = lax.bitcast_convert_type(isc, jnp.int32)
        key = bits ^ ((bits >> 31) & 0x7FFFFFFF)
        keys_ref[j] = key
        keys_t_ref[j] = key.T

    @pl.loop(0, it)
    def _(j):
        score_tile(j, False)

    score_tile(it, True)

    def count(pred):
        def body(j, cnt):
            return cnt + _sublane_fold(jnp.where(pred(keys_t_ref[j]), 1, 0))
        cnt = lax.fori_loop(0, it + 1, body, jnp.zeros((SUBLANES, t), jnp.int32))
        return jnp.sum(cnt, axis=0, keepdims=True)

    thr = jnp.where(count(lambda k: k >= 0) >= topk, 0, jnp.full((1, t), INT_MIN, jnp.int32))

    def bit_step(i, thr):
        cand = thr | (1 << (30 - i))
        return jnp.where(count(lambda k: k >= cand) >= topk, cand, thr)

    thr = lax.fori_loop(0, 31, bit_step, thr)
    wanted = (topk - count(lambda k: k > thr)).astype(F32)
    thr_q = _lane_tile(jnp.broadcast_to(thr, (LANES, t)).T, t)
    wanted_q = _lane_tile(jnp.broadcast_to(wanted, (LANES, t)).T, t)

    q_stack = _stack_masked(q_ref[0], heads)
    r = lax.broadcasted_iota(jnp.int32, (t, t), 0)
    c = lax.broadcasted_iota(jnp.int32, (t, t), 1)
    before = (r < c).astype(BF16)
    ones = jnp.ones((t, LANES), BF16)
    _reset_stats(max_ref, sum_ref, acc_ref)

    def select_tile(j, ties_seen, diagonal):
        key = keys_ref[j]
        tie = key == thr_q
        tie_b = jnp.where(tie, 1.0, 0.0).astype(BF16)
        tie_rank = _lane_tile(ties_seen, t) + _dot(tie_b, before)
        bias = jnp.where(key > thr_q, 0.0, jnp.where(tie, jnp.where(tie_rank < wanted_q, 0.0, MASKED), MASKED))
        if diagonal:
            bias = jnp.where(causal, bias, MASKED)
        s = _nt_dot(q_stack, k_ref[0, _tile_rows(j, t), :]) + jnp.concatenate([bias] * A_HEADS, axis=0)
        _record_scores(s, j, 0, s_ref, max_ref)
        return ties_seen + _dot(tie_b, ones)

    ties_seen = lax.fori_loop(0, it, lambda j, seen: select_tile(j, seen, False), jnp.zeros((t, LANES), F32))
    select_tile(it, ties_seen, True)
    _finish_max(max_ref)

    @pl.loop(0, it + 1)
    def _(j):
        _accumulate_probs(j, 0, A_HEADS * t, v_ref[0, _tile_rows(j, t), :], s_ref, max_ref, sum_ref, acc_ref)

    o = [_normalised(h * t, t, sum_ref, acc_ref) for h in range(A_HEADS)]
    o_ref[0] = _interleave_pairs(o, t).astype(o_ref.dtype)


def _dsa_attention(qa, ka, va, qi, ki, wi):
    b, s, w = qa.shape
    t = ATT_TILE
    topk = min(INDEX_TOPK, s // 4)
    assert t >= topk
    row = lambda width: pl.BlockSpec((1, t, width), lambda bi, i: (bi, i, 0))
    full = lambda width: pl.BlockSpec((1, s, width), lambda bi, i: (bi, 0, 0))
    return pl.pallas_call(
        functools.partial(_dsa_kernel, t=t, topk=topk),
        grid=(b, s // t),
        in_specs=[row(w), full(w), full(va.shape[2]), row(qi.shape[2]), full(w), row(LANES)],
        out_specs=row(w),
        out_shape=jax.ShapeDtypeStruct((b, s, w), BF16),
        scratch_shapes=[pltpu.VMEM((s // t, t, t), jnp.int32), pltpu.VMEM((s // t, t, t), jnp.int32)]
                       + _ATT_SCRATCH(s // t, A_HEADS * t, t),
        compiler_params=_params("parallel", "arbitrary"),
        name="dsa_attention",
    )(qa, ka, va, qi, ki, wi)


def _merge_kernel(x_ref, oa_ref, ob_ref, oc_ref, gate_ref, wb_ref, wo_ref, o_ref):
    d = x_ref.shape[1]
    wa, wbw = oa_ref.shape[1], ob_ref.shape[1]
    merged = gate_ref[:, 0:d].astype(F32) * _dot(oa_ref[...], wb_ref[0:wa, :])
    merged = merged + gate_ref[:, d:2 * d].astype(F32) * _dot(ob_ref[...], wb_ref[wa:wa + wbw, :])
    merged = merged + gate_ref[:, 2 * d:3 * d].astype(F32) * _dot(oc_ref[...], wb_ref[wa + wbw:, :])
    o_ref[...] = x_ref[...] + _dot(merged.astype(BF16), wo_ref[...])


def _merge(x, oa, ob, oc, gate, wb, wo, tm):
    n, d = x.shape
    row = lambda w: pl.BlockSpec((tm, w), lambda i: (i, 0))
    return pl.pallas_call(
        _merge_kernel,
        grid=(n // tm,),
        in_specs=[row(d), row(oa.shape[1]), row(ob.shape[1]), row(oc.shape[1]), row(gate.shape[1]),
                  _resident(wb.shape), _resident(wo.shape)],
        out_specs=row(d),
        out_shape=jax.ShapeDtypeStruct((n, d), F32),
        compiler_params=_params("parallel"),
        name="merge_out",
    )(x, oa, ob, oc, gate, wb, wo)


def kernel(x, positions, norm_g, w_in, qk_norm_g, lambda_params, diff_subln_g, w_branch, w_out,
           ffn_w_gate, ffn_w_up, ffn_w_down):
    b, s, d = x.shape
    n = b * s
    depth = norm_g.shape[0]
    tm = min(512, n)
    assert n % tm == 0 and s % ATT_TILE == 0 and ATT_TILE == MOBA_BLOCK

    cos, sin = _rope_tables(positions, tm)
    xf = x.reshape(n, d)
    for layer in range(depth):
        lam_init = 0.8 - 0.6 * math.exp(-0.3 * layer)
        ffn_w = lambda i: (ffn_w_gate[layer, i].astype(BF16), ffn_w_up[layer, i].astype(BF16),
                           ffn_w_down[layer, i].astype(BF16))
        xf = _ffn(xf, norm_g[layer, 0][None, :], *ffn_w(0), tm)

        qkg = jnp.tile(qk_norm_g[layer], (1, LANES // HEAD_DIM))
        outs = _proj(xf, norm_g[layer, 1][None, :], qkg, cos, sin, _proj_weights(w_in[layer]), tm)
        p = {name: o.reshape(b, s, o.shape[1]) for (name, *_), o in zip(_PROJ_OUTPUTS, outs)}
        o_a = _dsa_attention(p["qa"], p["ka"], p["va"], p["qi"], p["ki"], p["wi"])
        o_b = _moba_attention(p["qb"], p["kb"], p["vb"])
        o_c = _diff_attention(p["qc"], p["kc"], p["vc"], lambda_params[layer], diff_subln_g[layer][None, :], lam_init)
        xf = _merge(xf, o_a.reshape(n, -1), o_b.reshape(n, -1), o_c.reshape(n, -1), p["gate"].reshape(n, -1),
                    w_branch[layer].astype(BF16), w_out[layer].astype(BF16), tm)

        xf = _ffn(xf, norm_g[layer, 2][None, :], *ffn_w(1), tm)
    return xf.reshape(b, s, d)
```

```python
import functools
import math

import jax
import jax.numpy as jnp
from jax import lax
from jax.experimental import pallas as pl
from jax.experimental.pallas import tpu as pltpu

F32 = jnp.float32
BF16 = jnp.bfloat16

HEAD_DIM = 64
ROT_DIM = HEAD_DIM // 4
ROT_HALF = ROT_DIM // 2
ROPE_THETA = 500000.0
EPS = 1e-6
A_HEADS = 4
IDX_HEADS = 8
IDX_DIM = 64
INDEX_TOPK = 256
B_HEADS = 4
MOBA_BLOCK = 256
MOBA_TOPK = 3
C_HEADS = 4
C_VDIM = 2 * HEAD_DIM

LANES = 128
SUBLANES = 8
MXU_DIM = 256
ATT_TILE = 256
MASKED = -1e30
VMEM_LIMIT = 56 * 1024 * 1024
LOG2E = 1.4426950408889634


def _nt_dot(a, b):
    return lax.dot_general(a, b, (((1,), (1,)), ((), ())), preferred_element_type=F32)


def _dot(a, b):
    return jnp.dot(a, b, preferred_element_type=F32)


def _split_dot(x, w):
    hi = x.astype(BF16)
    lo = (x - hi.astype(F32)).astype(BF16)
    return _dot(hi, w) + _dot(lo, w)


def _resident(shape):
    nd = len(shape)
    return pl.BlockSpec(shape, lambda *_: (0,) * nd, pipeline_mode=pl.Buffered(1))


def _params(*sem):
    return pltpu.CompilerParams(dimension_semantics=sem, vmem_limit_bytes=VMEM_LIMIT)


def _rope_table_kernel(pos_ref, freq_ref, cos_ref, sin_ref):
    ang = pos_ref[...].astype(F32) * freq_ref[...]
    lane = lax.broadcasted_iota(jnp.int32, ang.shape, 1) % HEAD_DIM
    c = jnp.cos(ang)
    s = jnp.sin(ang)
    cos_ref[...] = jnp.where(lane < ROT_DIM, c, 1.0)
    sin_ref[...] = jnp.where(lane < ROT_HALF, -s, jnp.where(lane < ROT_DIM, s, 0.0))


def _rope_tables(positions, tm):
    n = positions.size
    pos = positions.reshape(n, 1)
    inv_freq = jnp.power(ROPE_THETA, -jnp.arange(0, ROT_DIM, 2, dtype=F32) / ROT_DIM)
    freq = jnp.tile(jnp.concatenate([inv_freq, inv_freq, jnp.zeros((HEAD_DIM - ROT_DIM,), F32)]), 2)[None, :]
    return pl.pallas_call(
        _rope_table_kernel,
        grid=(n // tm,),
        in_specs=[pl.BlockSpec((tm, 1), lambda i: (i, 0)), pl.BlockSpec((1, LANES), lambda i: (0, 0))],
        out_specs=[pl.BlockSpec((tm, LANES), lambda i: (i, 0))] * 2,
        out_shape=[jax.ShapeDtypeStruct((n, LANES), F32)] * 2,
        compiler_params=_params("parallel"),
        name="rope_tables",
    )(pos, freq)


def _rmsnorm_rows(x, g):
    return x * lax.rsqrt(jnp.mean(x * x, axis=-1, keepdims=True) + EPS) * g


def _ffn_kernel(x_ref, g_ref, wg_ref, wu_ref, wd_ref, o_ref, *, chunk):
    x = x_ref[...]
    h = _rmsnorm_rows(x, g_ref[...]).astype(BF16)
    acc = jnp.zeros(x.shape, F32)
    for c in range(0, wg_ref.shape[1], chunk):
        a = _dot(h, wg_ref[:, c:c + chunk])
        b = _dot(h, wu_ref[:, c:c + chunk])
        t = (a * jax.nn.sigmoid(a) * b).astype(BF16)
        acc = acc + _dot(t, wd_ref[c:c + chunk, :])
    o_ref[...] = x + 0.5 * acc


def _ffn(x, g, wg, wu, wd, tm):
    n, d = x.shape
    f = wg.shape[1]
    return pl.pallas_call(
        functools.partial(_ffn_kernel, chunk=MXU_DIM),
        grid=(n // tm,),
        in_specs=[pl.BlockSpec((tm, d), lambda i: (i, 0)), _resident((1, d)),
                  _resident((d, f)), _resident((d, f)), _resident((f, d))],
        out_specs=pl.BlockSpec((tm, d), lambda i: (i, 0)),
        out_shape=jax.ShapeDtypeStruct((n, d), F32),
        compiler_params=_params("parallel"),
        name="ffn",
    )(x, g, wg, wu, wd)


_PROJ_OUTPUTS = (
    ("qa", 256, "norm_rope_scale", 0, BF16),
    ("ka", 256, "norm_rope", 1, BF16),
    ("va", 128, "plain", None, BF16),
    ("qi", 512, "rope", None, BF16),
    ("ki", 256, "rope", None, BF16),
    ("wi", 128, "index_weight", None, F32),
    ("qb", 256, "norm_rope_scale", 2, BF16),
    ("kb", 256, "norm_rope", 3, BF16),
    ("vb", 256, "plain", None, BF16),
    ("qc", 512, "norm_rope_scale", 4, BF16),
    ("kc", 512, "norm_rope", 5, BF16),
    ("vc", 512, "plain", None, BF16),
    ("gate", 3072, "sigmoid", None, BF16),
)
_Q_SCALE = HEAD_DIM ** -0.5 * LOG2E


def _proj_kernel(x_ref, g_ref, qkg_ref, cos_ref, sin_ref, *refs):
    n_out = len(_PROJ_OUTPUTS)
    w_refs, o_refs = refs[:n_out], refs[n_out:]
    tm = x_ref.shape[0]
    h = _rmsnorm_rows(x_ref[...], g_ref[...]).astype(BF16)
    cos = cos_ref[...]
    sin = sin_ref[...]
    lane = lax.broadcasted_iota(jnp.int32, (tm, LANES), 1)
    take_upper = (lane % HEAD_DIM) < ROT_HALF
    r = lax.broadcasted_iota(jnp.int32, (LANES, LANES), 0) // HEAD_DIM
    c = lax.broadcasted_iota(jnp.int32, (LANES, LANES), 1) // HEAD_DIM
    same_head = (r == c).astype(BF16)

    def epilogue(y, kind, g_idx):
        if kind in ("norm_rope_scale", "norm_rope"):
            ssq = _split_dot(y * y, same_head)
            y = y * lax.rsqrt(ssq * (1.0 / HEAD_DIM) + EPS) * qkg_ref[g_idx:g_idx + 1, :]
        if kind in ("norm_rope_scale", "norm_rope", "rope"):
            partner = jnp.where(take_upper, pltpu.roll(y, LANES - ROT_HALF, 1), pltpu.roll(y, ROT_HALF, 1))
            y = y * cos + partner * sin
        if kind == "norm_rope_scale":
            y = y * _Q_SCALE
        if kind == "index_weight":
            y = y * ((IDX_HEADS * IDX_DIM) ** -0.5)
        if kind == "sigmoid":
            y = jax.nn.sigmoid(y)
        return y

    for (name, width, kind, g_idx, dtype), w_ref, o_ref in zip(_PROJ_OUTPUTS, w_refs, o_refs):
        step = min(width, MXU_DIM)
        for c0 in range(0, width, step):
            y = _dot(h, w_ref[:, c0:c0 + step])
            for s0 in range(0, step, LANES):
                o_ref[:, c0 + s0:c0 + s0 + LANES] = epilogue(y[:, s0:s0 + LANES], kind, g_idx).astype(dtype)


def _proj_weights(w_in):
    splits = (256, 64, 64, 512, 64, 8, 256, 256, 256, 512, 512, 512, 1024, 1024, 1024)
    offs = [0]
    for s in splits:
        offs.append(offs[-1] + s)
    qa, ka, va, qi, ki, wi, qb, kb, vb, qc, kc, vc, ga, gb, gc = [w_in[:, offs[i]:offs[i + 1]] for i in range(len(splits))]
    d = w_in.shape[0]
    out = dict(qa=qa, ka=jnp.tile(ka, (1, A_HEADS)), va=jnp.tile(va, (1, LANES // HEAD_DIM)), qi=qi,
               ki=jnp.tile(ki, (1, 4)), wi=jnp.concatenate([wi, jnp.zeros((d, LANES - IDX_HEADS), w_in.dtype)], axis=1),
               qb=qb, kb=kb, vb=vb, qc=qc, kc=kc, vc=vc, gate=jnp.concatenate([ga, gb, gc], axis=1))
    return [out[name].astype(BF16) for name, *_ in _PROJ_OUTPUTS]


def _proj(x, g, qkg, cos, sin, weights, tm):
    n, d = x.shape
    row = lambda w: pl.BlockSpec((tm, w), lambda i: (i, 0))
    return pl.pallas_call(
        _proj_kernel,
        grid=(n // tm,),
        in_specs=[row(d), _resident((1, d)), _resident(qkg.shape), row(LANES), row(LANES)]
                 + [_resident(w.shape) for w in weights],
        out_specs=[row(width) for _, width, *_ in _PROJ_OUTPUTS],
        out_shape=[jax.ShapeDtypeStruct((n, width), dtype) for _, width, _, _, dtype in _PROJ_OUTPUTS],
        compiler_params=_params("parallel"),
        name="in_proj",
    )(x, g, qkg, cos, sin, *weights)


def _lane_fold(x, op):
    out = x[:, :LANES]
    for c in range(LANES, x.shape[1], LANES):
        out = op(out, x[:, c:c + LANES])
    return out


def _lane_tile(x, width):
    return jnp.concatenate([x] * (width // LANES), axis=1)


def _reset_stats(max_ref, sum_ref, acc_ref):
    max_ref[...] = jnp.full(max_ref.shape, MASKED, F32)
    sum_ref[...] = jnp.zeros(sum_ref.shape, F32)
    acc_ref[...] = jnp.zeros(acc_ref.shape, F32)


def _record_scores(s, j, r0, s_ref, max_ref):
    n = s.shape[0]
    s_ref[j, r0:r0 + n, :] = s
    max_ref[r0:r0 + n, :] = jnp.maximum(max_ref[r0:r0 + n, :], _lane_fold(s, jnp.maximum))


def _finish_max(max_ref):
    m = max_ref[...]
    max_ref[...] = jnp.broadcast_to(jnp.max(m, axis=-1, keepdims=True), m.shape)


def _accumulate_probs(j, r0, n, v, s_ref, max_ref, sum_ref, acc_ref):
    s = s_ref[j, r0:r0 + n, :]
    p = jnp.exp2(s - _lane_tile(max_ref[r0:r0 + n, :], s.shape[1]))
    sum_ref[r0:r0 + n, :] += _lane_fold(p, jnp.add)
    acc_ref[r0:r0 + n, :] += _dot(p.astype(BF16), v)


def _normalised(r0, n, sum_ref, acc_ref):
    return acc_ref[r0:r0 + n, :] / jnp.sum(sum_ref[r0:r0 + n, :], axis=-1, keepdims=True)


def _head_lane_masks(rows, width):
    lane = lax.broadcasted_iota(jnp.int32, (rows, width), 1)
    return [(lane >= h * HEAD_DIM) & (lane < (h + 1) * HEAD_DIM) for h in range(width // HEAD_DIM)]


def _stack_masked(q, masks):
    zero = jnp.zeros_like(q)
    return jnp.concatenate([jnp.where(m, q, zero) for m in masks], axis=0)


def _local_causal(t):
    row = lax.broadcasted_iota(jnp.int32, (t, t), 0)
    col = lax.broadcasted_iota(jnp.int32, (t, t), 1)
    return col <= row


def _tile_rows(j, t):
    return pl.ds(pl.multiple_of(j * t, t), t)


def _interleave_pairs(o, t):
    lane = lax.broadcasted_iota(jnp.int32, (t, LANES), 1)
    low = lane < HEAD_DIM
    return jnp.concatenate([jnp.where(low, o[0], o[1]), jnp.where(low, o[2], o[3])], axis=1)


_ATT_SCRATCH = lambda n_tiles, rows, t: [
    pltpu.VMEM((n_tiles, rows, t), F32),
    pltpu.VMEM((rows, LANES), F32),
    pltpu.VMEM((rows, LANES), F32),
    pltpu.VMEM((rows, LANES), F32),
]


def _diff_kernel(lam_ref, g_ref, q_ref, k_ref, v_ref, o_ref, s_ref, max_ref, sum_ref, acc_ref, *, lam_init, t):
    it = pl.program_id(1)
    lane = lax.broadcasted_iota(jnp.int32, (t, C_VDIM), 1)
    maps = [lane < HEAD_DIM, lane >= HEAD_DIM]
    q_all = q_ref[0]
    qs = [_stack_masked(q_all[:, h * C_VDIM:(h + 1) * C_VDIM], maps) for h in range(C_HEADS)]
    _reset_stats(max_ref, sum_ref, acc_ref)
    causal = jnp.where(_local_causal(t), 0.0, MASKED)
    causal2 = jnp.concatenate([causal, causal], axis=0)

    def scores(j, bias):
        rows = _tile_rows(j, t)
        for h in range(C_HEADS):
            s = _nt_dot(qs[h], k_ref[0, rows, h * C_VDIM:(h + 1) * C_VDIM])
            if bias is not None:
                s = s + bias
            _record_scores(s, j, h * 2 * t, s_ref, max_ref)

    @pl.loop(0, it)
    def _(j):
        scores(j, None)

    scores(it, causal2)
    _finish_max(max_ref)

    @pl.loop(0, it + 1)
    def _(j):
        rows = _tile_rows(j, t)
        for h in range(C_HEADS):
            _accumulate_probs(j, h * 2 * t, 2 * t, v_ref[0, rows, h * C_VDIM:(h + 1) * C_VDIM],
                              s_ref, max_ref, sum_ref, acc_ref)

    lp = lam_ref[...]
    lam = (jnp.exp(jnp.sum(lp[0:1] * lp[1:2], axis=-1, keepdims=True))
           - jnp.exp(jnp.sum(lp[2:3] * lp[3:4], axis=-1, keepdims=True)) + lam_init)
    for h in range(C_HEADS):
        o = _normalised(h * 2 * t, t, sum_ref, acc_ref) - lam * _normalised(h * 2 * t + t, t, sum_ref, acc_ref)
        o = o * lax.rsqrt(jnp.mean(o * o, axis=-1, keepdims=True) + EPS) * g_ref[...] * (1.0 - lam_init)
        o_ref[0, :, h * C_VDIM:(h + 1) * C_VDIM] = o.astype(o_ref.dtype)


def _diff_attention(qc, kc, vc, lam_p, subln_g, lam_init):
    b, s, w = qc.shape
    t = ATT_TILE
    qspec = pl.BlockSpec((1, t, w), lambda bi, i: (bi, i, 0))
    kvspec = pl.BlockSpec((1, s, w), lambda bi, i: (bi, 0, 0))
    return pl.pallas_call(
        functools.partial(_diff_kernel, lam_init=lam_init, t=t),
        grid=(b, s // t),
        in_specs=[pl.BlockSpec(lam_p.shape, lambda bi, i: (0, 0)),
                  pl.BlockSpec((1, C_VDIM), lambda bi, i: (0, 0)), qspec, kvspec, kvspec],
        out_specs=qspec,
        out_shape=jax.ShapeDtypeStruct((b, s, w), BF16),
        scratch_shapes=_ATT_SCRATCH(s // t, 2 * C_HEADS * t, t),
        compiler_params=_params("parallel", "arbitrary"),
        name="diff_attention",
    )(lam_p, subln_g, qc, kc, vc)


def _moba_kernel(q_ref, k_ref, v_ref, o_ref, kmean_hi, kmean_lo, bias_ref, s_ref, max_ref, sum_ref, acc_ref, *, t):
    it = pl.program_id(1)
    n_blk = k_ref.shape[1] // t
    width = q_ref.shape[2]

    @pl.when(it == 0)
    def _():
        kmean_hi[...] = jnp.zeros_like(kmean_hi)
        kmean_lo[...] = jnp.zeros_like(kmean_lo)
        for n in range(n_blk):
            km = jnp.mean(k_ref[0, n * t:(n + 1) * t, :].astype(F32), axis=0, keepdims=True)
            hi = km.astype(BF16)
            kmean_hi[n:n + 1, :] = hi
            kmean_lo[n:n + 1, :] = (km - hi.astype(F32)).astype(BF16)

    heads = _head_lane_masks(t, width)
    q_stack = _stack_masked(q_ref[0], heads)
    blk = lax.broadcasted_iota(jnp.int32, (SUBLANES, t), 0)
    past = blk < it
    pad = jnp.full((LANES - SUBLANES, t), MASKED, F32)

    bias_q = []
    for h in range(B_HEADS):
        qh = q_stack[h * t:(h + 1) * t]
        gate = (_nt_dot(kmean_hi[...], qh) + _nt_dot(kmean_lo[...], qh))[:SUBLANES]
        gate = jnp.where(past, gate, -jnp.inf)
        rank = jnp.zeros((SUBLANES, t), jnp.int32)
        for n in range(n_blk):
            gn = gate[n:n + 1, :]
            rank = rank + jnp.where((gn > gate) | ((gn == gate) & (blk > n)), 1, 0)
        bias = jnp.where((rank < MOBA_TOPK) & past, 0.0, MASKED)
        bias_q.append(jnp.concatenate([bias, pad], axis=0).T.astype(BF16))
    bias_q = jnp.concatenate(bias_q, axis=0)
    lane_of = lax.broadcasted_iota(jnp.int32, (LANES, LANES), 0)
    for n in range(n_blk - 1):
        @pl.when(n < it)
        def _():
            pick = (lane_of == n).astype(BF16)
            bias_ref[n] = _dot(bias_q, pick)

    _reset_stats(max_ref, sum_ref, acc_ref)

    @pl.loop(0, it)
    def _(j):
        s = _nt_dot(q_stack, k_ref[0, _tile_rows(j, t), :]) + _lane_tile(bias_ref[j], t)
        _record_scores(s, j, 0, s_ref, max_ref)

    causal = jnp.where(_local_causal(t), 0.0, MASKED)
    s = _nt_dot(q_stack, k_ref[0, _tile_rows(it, t), :]) + jnp.concatenate([causal] * B_HEADS, axis=0)
    _record_scores(s, it, 0, s_ref, max_ref)
    _finish_max(max_ref)

    @pl.loop(0, it + 1)
    def _(j):
        rows = _tile_rows(j, t)
        for pair in range(B_HEADS // 2):
            _accumulate_probs(j, pair * 2 * t, 2 * t, v_ref[0, rows, pair * LANES:(pair + 1) * LANES],
                              s_ref, max_ref, sum_ref, acc_ref)

    o = [_normalised(h * t, t, sum_ref, acc_ref) for h in range(B_HEADS)]
    o_ref[0] = _interleave_pairs(o, t).astype(o_ref.dtype)


def _moba_attention(qb, kb, vb):
    b, s, w = qb.shape
    t = MOBA_BLOCK
    assert s // t <= SUBLANES
    qspec = pl.BlockSpec((1, t, w), lambda bi, i: (bi, i, 0))
    kvspec = pl.BlockSpec((1, s, w), lambda bi, i: (bi, 0, 0))
    return pl.pallas_call(
        functools.partial(_moba_kernel, t=t),
        grid=(b, s // t),
        in_specs=[qspec, kvspec, kvspec],
        out_specs=qspec,
        out_shape=jax.ShapeDtypeStruct((b, s, w), BF16),
        scratch_shapes=[pltpu.VMEM((LANES, w), BF16), pltpu.VMEM((LANES, w), BF16),
                        pltpu.VMEM((s // t, B_HEADS * t, LANES), F32)] + _ATT_SCRATCH(s // t, B_HEADS * t, t),
        compiler_params=_params("parallel", "arbitrary"),
        name="moba_attention",
    )(qb, kb, vb)


def _sublane_fold(x, rows):
    parts = [x[r:r + rows] for r in range(0, x.shape[0], rows)]
    while len(parts) > 1:
        parts = [parts[i] + parts[i + 1] for i in range(0, len(parts), 2)]
    return parts[0]


I16 = jnp.int16
I16_ROWS = 2 * SUBLANES
I16_MIN = -(2 ** 15)


def _dsa_kernel(q_ref, k_ref, v_ref, qi_ref, ki_ref, wi_ref, o_ref,
                keys_ref, hi_ref, lo_ref, s_ref, max_ref, sum_ref, acc_ref, *, t, topk):
    it = pl.program_id(1)
    width = q_ref.shape[2]
    heads = _head_lane_masks(t, width)
    causal = _local_causal(t)

    qidx = qi_ref[0]
    qi_stack = jnp.concatenate(
        [_stack_masked(qidx[:, g * width:(g + 1) * width], heads) for g in range(IDX_HEADS // 4)], axis=0)
    wi = wi_ref[0]
    w_rep = [jnp.broadcast_to(wi[:, h:h + 1], (t, LANES)) for h in range(IDX_HEADS)]

    def score_tile(j, diagonal):
        logit = _nt_dot(qi_stack, ki_ref[0, _tile_rows(j, t), :])
        cols = []
        for c in range(0, t, LANES):
            acc = jnp.zeros((t, LANES), F32)
            for h in range(IDX_HEADS):
                acc = acc + jnp.maximum(logit[h * t:(h + 1) * t, c:c + LANES], 0.0) * w_rep[h]
            cols.append(acc)
        isc = jnp.concatenate(cols, axis=1) + 0.0
        if diagonal:
            isc = jnp.where(causal, isc, -jnp.inf)
        bits = lax.bitcast_convert_type(isc, jnp.int32)
        key = bits ^ ((bits >> 31) & 0x7FFFFFFF)
        keys_ref[j] = key
        key_t = key.T
        hi_ref[j] = (key_t >> 16).astype(I16)
        lo_ref[j] = ((key_t & 0xFFFF) + I16_MIN).astype(I16)

    @pl.loop(0, it)
    def _(j):
        score_tile(j, False)

    score_tile(it, True)

    def count(ref, pred):
        def body(j, cnt):
            return cnt + _sublane_fold(jnp.where(pred(ref[j]), jnp.int16(1), jnp.int16(0)), I16_ROWS)
        cnt = lax.fori_loop(0, it + 1, body, jnp.zeros((I16_ROWS, t), I16))
        return jnp.sum(cnt.astype(jnp.int32), axis=0, keepdims=True)

    def kth_largest(ref, k):
        thr = jnp.where(count(ref, lambda x: x >= jnp.int16(0)) >= k, 0, jnp.full((1, t), I16_MIN, jnp.int32))

        def bit_step(i, thr):
            cand = thr | (1 << (14 - i))
            return jnp.where(count(ref, lambda x: x >= cand.astype(I16)) >= k, cand, thr)

        return lax.fori_loop(0, 15, bit_step, thr)

    thr_hi = kth_largest(hi_ref, topk)
    thr_hi16 = thr_hi.astype(I16)
    k_lo = topk - count(hi_ref, lambda x: x > thr_hi16)

    @pl.loop(0, it + 1)
    def _(j):
        lo_ref[j] = jnp.where(hi_ref[j] == thr_hi16, lo_ref[j], jnp.int16(I16_MIN))

    thr_lo = kth_largest(lo_ref, k_lo)
    thr_lo16 = thr_lo.astype(I16)
    wanted = (k_lo - count(lo_ref, lambda x: x > thr_lo16)).astype(F32)
    thr = (thr_hi << 16) | (thr_lo - I16_MIN)
    thr_q = _lane_tile(jnp.broadcast_to(thr, (LANES, t)).T, t)
    wanted_q = _lane_tile(jnp.broadcast_to(wanted, (LANES, t)).T, t)

    q_stack = _stack_masked(q_ref[0], heads)
    r = lax.broadcasted_iota(jnp.int32, (t, t), 0)
    c = lax.broadcasted_iota(jnp.int32, (t, t), 1)
    before = (r < c).astype(BF16)
    ones = jnp.ones((t, LANES), BF16)
    _reset_stats(max_ref, sum_ref, acc_ref)

    def select_tile(j, ties_seen, diagonal):
        key = keys_ref[j]
        tie = key == thr_q
        tie_b = jnp.where(tie, 1.0, 0.0).astype(BF16)
        tie_rank = _lane_tile(ties_seen, t) + _dot(tie_b, before)
        bias = jnp.where(key > thr_q, 0.0, jnp.where(tie, jnp.where(tie_rank < wanted_q, 0.0, MASKED), MASKED))
        if diagonal:
            bias = jnp.where(causal, bias, MASKED)
        s = _nt_dot(q_stack, k_ref[0, _tile_rows(j, t), :]) + jnp.concatenate([bias] * A_HEADS, axis=0)
        _record_scores(s, j, 0, s_ref, max_ref)
        return ties_seen + _dot(tie_b, ones)

    ties_seen = lax.fori_loop(0, it, lambda j, seen: select_tile(j, seen, False), jnp.zeros((t, LANES), F32))
    select_tile(it, ties_seen, True)
    _finish_max(max_ref)

    @pl.loop(0, it + 1)
    def _(j):
        _accumulate_probs(j, 0, A_HEADS * t, v_ref[0, _tile_rows(j, t), :], s_ref, max_ref, sum_ref, acc_ref)

    o = [_normalised(h * t, t, sum_ref, acc_ref) for h in range(A_HEADS)]
    o_ref[0] = _interleave_pairs(o, t).astype(o_ref.dtype)


def _dsa_attention(qa, ka, va, qi, ki, wi):
    b, s, w = qa.shape
    t = ATT_TILE
    topk = min(INDEX_TOPK, s // 4)
    assert t >= topk
    row = lambda width: pl.BlockSpec((1, t, width), lambda bi, i: (bi, i, 0))
    full = lambda width: pl.BlockSpec((1, s, width), lambda bi, i: (bi, 0, 0))
    return pl.pallas_call(
        functools.partial(_dsa_kernel, t=t, topk=topk),
        grid=(b, s // t),
        in_specs=[row(w), full(w), full(va.shape[2]), row(qi.shape[2]), full(w), row(LANES)],
        out_specs=row(w),
        out_shape=jax.ShapeDtypeStruct((b, s, w), BF16),
        scratch_shapes=[pltpu.VMEM((s // t, t, t), jnp.int32),
                        pltpu.VMEM((s // t, t, t), I16), pltpu.VMEM((s // t, t, t), I16)]
                       + _ATT_SCRATCH(s // t, A_HEADS * t, t),
        compiler_params=_params("parallel", "arbitrary"),
        name="dsa_attention",
    )(qa, ka, va, qi, ki, wi)


def _merge_kernel(x_ref, oa_ref, ob_ref, oc_ref, gate_ref, wb_ref, wo_ref, o_ref):
    d = x_ref.shape[1]
    wa, wbw = oa_ref.shape[1], ob_ref.shape[1]
    merged = gate_ref[:, 0:d].astype(F32) * _dot(oa_ref[...], wb_ref[0:wa, :])
    merged = merged + gate_ref[:, d:2 * d].astype(F32) * _dot(ob_ref[...], wb_ref[wa:wa + wbw, :])
    merged = merged + gate_ref[:, 2 * d:3 * d].astype(F32) * _dot(oc_ref[...], wb_ref[wa + wbw:, :])
    o_ref[...] = x_ref[...] + _dot(merged.astype(BF16), wo_ref[...])


def _merge(x, oa, ob, oc, gate, wb, wo, tm):
    n, d = x.shape
    row = lambda w: pl.BlockSpec((tm, w), lambda i: (i, 0))
    return pl.pallas_call(
        _merge_kernel,
        grid=(n // tm,),
        in_specs=[row(d), row(oa.shape[1]), row(ob.shape[1]), row(oc.shape[1]), row(gate.shape[1]),
                  _resident(wb.shape), _resident(wo.shape)],
        out_specs=row(d),
        out_shape=jax.ShapeDtypeStruct((n, d), F32),
        compiler_params=_params("parallel"),
        name="merge_out",
    )(x, oa, ob, oc, gate, wb, wo)


def kernel(x, positions, norm_g, w_in, qk_norm_g, lambda_params, diff_subln_g, w_branch, w_out,
           ffn_w_gate, ffn_w_up, ffn_w_down):
    b, s, d = x.shape
    n = b * s
    depth = norm_g.shape[0]
    tm = min(512, n)
    assert n % tm == 0 and s % ATT_TILE == 0 and ATT_TILE == MOBA_BLOCK

    cos, sin = _rope_tables(positions, tm)
    xf = x.reshape(n, d)
    for layer in range(depth):
        lam_init = 0.8 - 0.6 * math.exp(-0.3 * layer)
        ffn_w = lambda i: (ffn_w_gate[layer, i].astype(BF16), ffn_w_up[layer, i].astype(BF16),
                           ffn_w_down[layer, i].astype(BF16))
        xf = _ffn(xf, norm_g[layer, 0][None, :], *ffn_w(0), tm)

        qkg = jnp.tile(qk_norm_g[layer], (1, LANES // HEAD_DIM))
        outs = _proj(xf, norm_g[layer, 1][None, :], qkg, cos, sin, _proj_weights(w_in[layer]), tm)
        p = {name: o.reshape(b, s, o.shape[1]) for (name, *_), o in zip(_PROJ_OUTPUTS, outs)}
        o_a = _dsa_attention(p["qa"], p["ka"], p["va"], p["qi"], p["ki"], p["wi"])
        o_b = _moba_attention(p["qb"], p["kb"], p["vb"])
        o_c = _diff_attention(p["qc"], p["kc"], p["vc"], lambda_params[layer], diff_subln_g[layer][None, :], lam_init)
        xf = _merge(xf, o_a.reshape(n, -1), o_b.reshape(n, -1), o_c.reshape(n, -1), p["gate"].reshape(n, -1),
                    w_branch[layer].astype(BF16), w_out[layer].astype(BF16), tm)

        xf = _ffn(xf, norm_g[layer, 2][None, :], *ffn_w(1), tm)
    return xf.reshape(b, s, d)
```

```python
import functools
import math

import jax
import jax.numpy as jnp
from jax import lax
from jax.experimental import pallas as pl
from jax.experimental.pallas import tpu as pltpu

F32 = jnp.float32
BF16 = jnp.bfloat16

HEAD_DIM = 64
ROT_DIM = HEAD_DIM // 4
ROT_HALF = ROT_DIM // 2
ROPE_THETA = 500000.0
EPS = 1e-6
A_HEADS = 4
IDX_HEADS = 8
IDX_DIM = 64
INDEX_TOPK = 256
B_HEADS = 4
MOBA_BLOCK = 256
MOBA_TOPK = 3
C_HEADS = 4
C_VDIM = 2 * HEAD_DIM

LANES = 128
SUBLANES = 8
MXU_DIM = 256
ATT_TILE = 256
MASKED = -1e30
VMEM_LIMIT = 56 * 1024 * 1024
LOG2E = 1.4426950408889634


def _nt_dot(a, b):
    return lax.dot_general(a, b, (((1,), (1,)), ((), ())), preferred_element_type=F32)


def _dot(a, b):
    return jnp.dot(a, b, preferred_element_type=F32)


def _split_dot(x, w):
    hi = x.astype(BF16)
    lo = (x - hi.astype(F32)).astype(BF16)
    return _dot(hi, w) + _dot(lo, w)


def _resident(shape):
    nd = len(shape)
    return pl.BlockSpec(shape, lambda *_: (0,) * nd, pipeline_mode=pl.Buffered(1))


def _params(*sem):
    return pltpu.CompilerParams(dimension_semantics=sem, vmem_limit_bytes=VMEM_LIMIT)


def _rope_table_kernel(pos_ref, freq_ref, cos_ref, sin_ref):
    ang = pos_ref[...].astype(F32) * freq_ref[...]
    lane = lax.broadcasted_iota(jnp.int32, ang.shape, 1) % HEAD_DIM
    c = jnp.cos(ang)
    s = jnp.sin(ang)
    cos_ref[...] = jnp.where(lane < ROT_DIM, c, 1.0)
    sin_ref[...] = jnp.where(lane < ROT_HALF, -s, jnp.where(lane < ROT_DIM, s, 0.0))


def _rope_tables(positions, tm):
    n = positions.size
    pos = positions.reshape(n, 1)
    inv_freq = jnp.power(ROPE_THETA, -jnp.arange(0, ROT_DIM, 2, dtype=F32) / ROT_DIM)
    freq = jnp.tile(jnp.concatenate([inv_freq, inv_freq, jnp.zeros((HEAD_DIM - ROT_DIM,), F32)]), 2)[None, :]
    return pl.pallas_call(
        _rope_table_kernel,
        grid=(n // tm,),
        in_specs=[pl.BlockSpec((tm, 1), lambda i: (i, 0)), pl.BlockSpec((1, LANES), lambda i: (0, 0))],
        out_specs=[pl.BlockSpec((tm, LANES), lambda i: (i, 0))] * 2,
        out_shape=[jax.ShapeDtypeStruct((n, LANES), F32)] * 2,
        compiler_params=_params("parallel"),
        name="rope_tables",
    )(pos, freq)


def _rmsnorm_rows(x, g):
    return x * lax.rsqrt(jnp.mean(x * x, axis=-1, keepdims=True) + EPS) * g


def _ffn_kernel(x_ref, g_ref, wg_ref, wu_ref, wd_ref, o_ref, *, chunk):
    x = x_ref[...]
    h = _rmsnorm_rows(x, g_ref[...]).astype(BF16)
    acc = jnp.zeros(x.shape, F32)
    for c in range(0, wg_ref.shape[1], chunk):
        a = _dot(h, wg_ref[:, c:c + chunk])
        b = _dot(h, wu_ref[:, c:c + chunk])
        t = (a * jax.nn.sigmoid(a) * b).astype(BF16)
        acc = acc + _dot(t, wd_ref[c:c + chunk, :])
    o_ref[...] = x + 0.5 * acc


def _ffn(x, g, wg, wu, wd, tm):
    n, d = x.shape
    f = wg.shape[1]
    return pl.pallas_call(
        functools.partial(_ffn_kernel, chunk=MXU_DIM),
        grid=(n // tm,),
        in_specs=[pl.BlockSpec((tm, d), lambda i: (i, 0)), _resident((1, d)),
                  _resident((d, f)), _resident((d, f)), _resident((f, d))],
        out_specs=pl.BlockSpec((tm, d), lambda i: (i, 0)),
        out_shape=jax.ShapeDtypeStruct((n, d), F32),
        compiler_params=_params("parallel"),
        name="ffn",
    )(x, g, wg, wu, wd)


_PROJ_OUTPUTS = (
    ("qa", 256, "norm_rope_scale", 0, BF16),
    ("ka", 256, "norm_rope", 1, BF16),
    ("va", 128, "plain", None, BF16),
    ("qi", 512, "rope", None, BF16),
    ("ki", 256, "rope", None, BF16),
    ("wi", 128, "index_weight", None, F32),
    ("qb", 256, "norm_rope_scale", 2, BF16),
    ("kb", 256, "norm_rope", 3, BF16),
    ("vb", 256, "plain", None, BF16),
    ("qc", 512, "norm_rope_scale", 4, BF16),
    ("kc", 512, "norm_rope", 5, BF16),
    ("vc", 512, "plain", None, BF16),
    ("gate", 3072, "sigmoid", None, BF16),
)
_Q_SCALE = HEAD_DIM ** -0.5 * LOG2E


def _proj_kernel(x_ref, g_ref, qkg_ref, cos_ref, sin_ref, *refs):
    n_out = len(_PROJ_OUTPUTS)
    w_refs, o_refs = refs[:n_out], refs[n_out:]
    tm = x_ref.shape[0]
    h = _rmsnorm_rows(x_ref[...], g_ref[...]).astype(BF16)
    cos = cos_ref[...]
    sin = sin_ref[...]
    lane = lax.broadcasted_iota(jnp.int32, (tm, LANES), 1)
    take_upper = (lane % HEAD_DIM) < ROT_HALF
    r = lax.broadcasted_iota(jnp.int32, (LANES, LANES), 0) // HEAD_DIM
    c = lax.broadcasted_iota(jnp.int32, (LANES, LANES), 1) // HEAD_DIM
    same_head = (r == c).astype(BF16)

    def epilogue(y, kind, g_idx):
        if kind in ("norm_rope_scale", "norm_rope"):
            ssq = _split_dot(y * y, same_head)
            y = y * lax.rsqrt(ssq * (1.0 / HEAD_DIM) + EPS) * qkg_ref[g_idx:g_idx + 1, :]
        if kind in ("norm_rope_scale", "norm_rope", "rope"):
            partner = jnp.where(take_upper, pltpu.roll(y, LANES - ROT_HALF, 1), pltpu.roll(y, ROT_HALF, 1))
            y = y * cos + partner * sin
        if kind == "norm_rope_scale":
            y = y * _Q_SCALE
        if kind == "index_weight":
            y = y * ((IDX_HEADS * IDX_DIM) ** -0.5)
        if kind == "sigmoid":
            y = jax.nn.sigmoid(y)
        return y

    for (name, width, kind, g_idx, dtype), w_ref, o_ref in zip(_PROJ_OUTPUTS, w_refs, o_refs):
        step = min(width, MXU_DIM)
        for c0 in range(0, width, step):
            y = _dot(h, w_ref[:, c0:c0 + step])
            for s0 in range(0, step, LANES):
                o_ref[:, c0 + s0:c0 + s0 + LANES] = epilogue(y[:, s0:s0 + LANES], kind, g_idx).astype(dtype)


def _proj_weights(w_in):
    splits = (256, 64, 64, 512, 64, 8, 256, 256, 256, 512, 512, 512, 1024, 1024, 1024)
    offs = [0]
    for s in splits:
        offs.append(offs[-1] + s)
    qa, ka, va, qi, ki, wi, qb, kb, vb, qc, kc, vc, ga, gb, gc = [w_in[:, offs[i]:offs[i + 1]] for i in range(len(splits))]
    d = w_in.shape[0]
    out = dict(qa=qa, ka=jnp.tile(ka, (1, A_HEADS)), va=jnp.tile(va, (1, LANES // HEAD_DIM)), qi=qi,
               ki=jnp.tile(ki, (1, 4)), wi=jnp.concatenate([wi, jnp.zeros((d, LANES - IDX_HEADS), w_in.dtype)], axis=1),
               qb=qb, kb=kb, vb=vb, qc=qc, kc=kc, vc=vc, gate=jnp.concatenate([ga, gb, gc], axis=1))
    return [out[name].astype(BF16) for name, *_ in _PROJ_OUTPUTS]


def _proj(x, g, qkg, cos, sin, weights, tm):
    n, d = x.shape
    row = lambda w: pl.BlockSpec((tm, w), lambda i: (i, 0))
    return pl.pallas_call(
        _proj_kernel,
        grid=(n // tm,),
        in_specs=[row(d), _resident((1, d)), _resident(qkg.shape), row(LANES), row(LANES)]
                 + [_resident(w.shape) for w in weights],
        out_specs=[row(width) for _, width, *_ in _PROJ_OUTPUTS],
        out_shape=[jax.ShapeDtypeStruct((n, width), dtype) for _, width, _, _, dtype in _PROJ_OUTPUTS],
        compiler_params=_params("parallel"),
        name="in_proj",
    )(x, g, qkg, cos, sin, *weights)


def _lane_fold(x, op):
    out = x[:, :LANES]
    for c in range(LANES, x.shape[1], LANES):
        out = op(out, x[:, c:c + LANES])
    return out


def _lane_tile(x, width):
    return jnp.concatenate([x] * (width // LANES), axis=1)


def _reset_stats(max_ref, sum_ref, acc_ref):
    max_ref[...] = jnp.full(max_ref.shape, MASKED, F32)
    sum_ref[...] = jnp.zeros(sum_ref.shape, F32)
    acc_ref[...] = jnp.zeros(acc_ref.shape, F32)


def _record_scores(s, j, r0, s_ref, max_ref):
    n = s.shape[0]
    s_ref[j, r0:r0 + n, :] = s
    max_ref[r0:r0 + n, :] = jnp.maximum(max_ref[r0:r0 + n, :], _lane_fold(s, jnp.maximum))


def _finish_max(max_ref):
    m = max_ref[...]
    max_ref[...] = jnp.broadcast_to(jnp.max(m, axis=-1, keepdims=True), m.shape)


def _accumulate_probs(j, r0, n, v, s_ref, max_ref, sum_ref, acc_ref):
    s = s_ref[j, r0:r0 + n, :]
    p = jnp.exp2(s - _lane_tile(max_ref[r0:r0 + n, :], s.shape[1]))
    sum_ref[r0:r0 + n, :] += _lane_fold(p, jnp.add)
    acc_ref[r0:r0 + n, :] += _dot(p.astype(BF16), v)


def _normalised(r0, n, sum_ref, acc_ref):
    return acc_ref[r0:r0 + n, :] / jnp.sum(sum_ref[r0:r0 + n, :], axis=-1, keepdims=True)


def _head_lane_masks(rows, width):
    lane = lax.broadcasted_iota(jnp.int32, (rows, width), 1)
    return [(lane >= h * HEAD_DIM) & (lane < (h + 1) * HEAD_DIM) for h in range(width // HEAD_DIM)]


def _stack_masked(q, masks):
    zero = jnp.zeros_like(q)
    return jnp.concatenate([jnp.where(m, q, zero) for m in masks], axis=0)


def _local_causal(t):
    row = lax.broadcasted_iota(jnp.int32, (t, t), 0)
    col = lax.broadcasted_iota(jnp.int32, (t, t), 1)
    return col <= row


def _tile_rows(j, t):
    return pl.ds(pl.multiple_of(j * t, t), t)


def _for_each_tile(n, body):
    @pl.loop(0, n // 2)
    def _(i):
        body(2 * i)
        body(2 * i + 1)

    @pl.when(n % 2 == 1)
    def _():
        body(n - 1)


def _interleave_pairs(o, t):
    lane = lax.broadcasted_iota(jnp.int32, (t, LANES), 1)
    low = lane < HEAD_DIM
    return jnp.concatenate([jnp.where(low, o[0], o[1]), jnp.where(low, o[2], o[3])], axis=1)


_ATT_SCRATCH = lambda n_tiles, rows, t: [
    pltpu.VMEM((n_tiles, rows, t), F32),
    pltpu.VMEM((rows, LANES), F32),
    pltpu.VMEM((rows, LANES), F32),
    pltpu.VMEM((rows, LANES), F32),
]


def _diff_kernel(lam_ref, g_ref, q_ref, k_ref, v_ref, o_ref, s_ref, max_ref, sum_ref, acc_ref, *, lam_init, t):
    it = pl.program_id(1)
    lane = lax.broadcasted_iota(jnp.int32, (t, C_VDIM), 1)
    maps = [lane < HEAD_DIM, lane >= HEAD_DIM]
    q_all = q_ref[0]
    qs = [_stack_masked(q_all[:, h * C_VDIM:(h + 1) * C_VDIM], maps) for h in range(C_HEADS)]
    _reset_stats(max_ref, sum_ref, acc_ref)
    causal = jnp.where(_local_causal(t), 0.0, MASKED)
    causal2 = jnp.concatenate([causal, causal], axis=0)

    def scores(j, bias):
        rows = _tile_rows(j, t)
        for h in range(C_HEADS):
            s = _nt_dot(qs[h], k_ref[0, rows, h * C_VDIM:(h + 1) * C_VDIM])
            if bias is not None:
                s = s + bias
            _record_scores(s, j, h * 2 * t, s_ref, max_ref)

    _for_each_tile(it, lambda j: scores(j, None))
    scores(it, causal2)
    _finish_max(max_ref)

    def probs(j):
        rows = _tile_rows(j, t)
        for h in range(C_HEADS):
            _accumulate_probs(j, h * 2 * t, 2 * t, v_ref[0, rows, h * C_VDIM:(h + 1) * C_VDIM],
                              s_ref, max_ref, sum_ref, acc_ref)

    _for_each_tile(it + 1, probs)

    lp = lam_ref[...]
    lam = (jnp.exp(jnp.sum(lp[0:1] * lp[1:2], axis=-1, keepdims=True))
           - jnp.exp(jnp.sum(lp[2:3] * lp[3:4], axis=-1, keepdims=True)) + lam_init)
    for h in range(C_HEADS):
        o = _normalised(h * 2 * t, t, sum_ref, acc_ref) - lam * _normalised(h * 2 * t + t, t, sum_ref, acc_ref)
        o = o * lax.rsqrt(jnp.mean(o * o, axis=-1, keepdims=True) + EPS) * g_ref[...] * (1.0 - lam_init)
        o_ref[0, :, h * C_VDIM:(h + 1) * C_VDIM] = o.astype(o_ref.dtype)


def _diff_attention(qc, kc, vc, lam_p, subln_g, lam_init):
    b, s, w = qc.shape
    t = ATT_TILE
    qspec = pl.BlockSpec((1, t, w), lambda bi, i: (bi, i, 0))
    kvspec = pl.BlockSpec((1, s, w), lambda bi, i: (bi, 0, 0))
    return pl.pallas_call(
        functools.partial(_diff_kernel, lam_init=lam_init, t=t),
        grid=(b, s // t),
        in_specs=[pl.BlockSpec(lam_p.shape, lambda bi, i: (0, 0)),
                  pl.BlockSpec((1, C_VDIM), lambda bi, i: (0, 0)), qspec, kvspec, kvspec],
        out_specs=qspec,
        out_shape=jax.ShapeDtypeStruct((b, s, w), BF16),
        scratch_shapes=_ATT_SCRATCH(s // t, 2 * C_HEADS * t, t),
        compiler_params=_params("parallel", "arbitrary"),
        name="diff_attention",
    )(lam_p, subln_g, qc, kc, vc)


def _moba_kernel(q_ref, k_ref, v_ref, o_ref, kmean_hi, kmean_lo, bias_ref, s_ref, max_ref, sum_ref, acc_ref, *, t):
    it = pl.program_id(1)
    n_blk = k_ref.shape[1] // t
    width = q_ref.shape[2]

    @pl.when(it == 0)
    def _():
        kmean_hi[...] = jnp.zeros_like(kmean_hi)
        kmean_lo[...] = jnp.zeros_like(kmean_lo)
        for n in range(n_blk):
            km = jnp.mean(k_ref[0, n * t:(n + 1) * t, :].astype(F32), axis=0, keepdims=True)
            hi = km.astype(BF16)
            kmean_hi[n:n + 1, :] = hi
            kmean_lo[n:n + 1, :] = (km - hi.astype(F32)).astype(BF16)

    heads = _head_lane_masks(t, width)
    q_stack = _stack_masked(q_ref[0], heads)
    blk = lax.broadcasted_iota(jnp.int32, (SUBLANES, t), 0)
    past = blk < it
    pad = jnp.full((LANES - SUBLANES, t), MASKED, F32)

    bias_q = []
    for h in range(B_HEADS):
        qh = q_stack[h * t:(h + 1) * t]
        gate = (_nt_dot(kmean_hi[...], qh) + _nt_dot(kmean_lo[...], qh))[:SUBLANES]
        gate = jnp.where(past, gate, -jnp.inf)
        rank = jnp.zeros((SUBLANES, t), jnp.int32)
        for n in range(n_blk):
            gn = gate[n:n + 1, :]
            rank = rank + jnp.where((gn > gate) | ((gn == gate) & (blk > n)), 1, 0)
        bias = jnp.where((rank < MOBA_TOPK) & past, 0.0, MASKED)
        bias_q.append(jnp.concatenate([bias, pad], axis=0).T.astype(BF16))
    bias_q = jnp.concatenate(bias_q, axis=0)
    lane_of = lax.broadcasted_iota(jnp.int32, (LANES, LANES), 0)
    for n in range(n_blk - 1):
        @pl.when(n < it)
        def _():
            pick = (lane_of == n).astype(BF16)
            bias_ref[n] = _dot(bias_q, pick)

    _reset_stats(max_ref, sum_ref, acc_ref)

    def past_scores(j):
        s = _nt_dot(q_stack, k_ref[0, _tile_rows(j, t), :]) + _lane_tile(bias_ref[j], t)
        _record_scores(s, j, 0, s_ref, max_ref)

    _for_each_tile(it, past_scores)
    causal = jnp.where(_local_causal(t), 0.0, MASKED)
    s = _nt_dot(q_stack, k_ref[0, _tile_rows(it, t), :]) + jnp.concatenate([causal] * B_HEADS, axis=0)
    _record_scores(s, it, 0, s_ref, max_ref)
    _finish_max(max_ref)

    def probs(j):
        rows = _tile_rows(j, t)
        for pair in range(B_HEADS // 2):
            _accumulate_probs(j, pair * 2 * t, 2 * t, v_ref[0, rows, pair * LANES:(pair + 1) * LANES],
                              s_ref, max_ref, sum_ref, acc_ref)

    _for_each_tile(it + 1, probs)

    o = [_normalised(h * t, t, sum_ref, acc_ref) for h in range(B_HEADS)]
    o_ref[0] = _interleave_pairs(o, t).astype(o_ref.dtype)


def _moba_attention(qb, kb, vb):
    b, s, w = qb.shape
    t = MOBA_BLOCK
    assert s // t <= SUBLANES
    qspec = pl.BlockSpec((1, t, w), lambda bi, i: (bi, i, 0))
    kvspec = pl.BlockSpec((1, s, w), lambda bi, i: (bi, 0, 0))
    return pl.pallas_call(
        functools.partial(_moba_kernel, t=t),
        grid=(b, s // t),
        in_specs=[qspec, kvspec, kvspec],
        out_specs=qspec,
        out_shape=jax.ShapeDtypeStruct((b, s, w), BF16),
        scratch_shapes=[pltpu.VMEM((LANES, w), BF16), pltpu.VMEM((LANES, w), BF16),
                        pltpu.VMEM((s // t, B_HEADS * t, LANES), F32)] + _ATT_SCRATCH(s // t, B_HEADS * t, t),
        compiler_params=_params("parallel", "arbitrary"),
        name="moba_attention",
    )(qb, kb, vb)


def _sublane_fold(x, rows):
    parts = [x[r:r + rows] for r in range(0, x.shape[0], rows)]
    while len(parts) > 1:
        parts = [parts[i] + parts[i + 1] for i in range(0, len(parts), 2)]
    return parts[0]


I16 = jnp.int16
I16_ROWS = 2 * SUBLANES
I16_MIN = -(2 ** 15)


def _dsa_kernel(q_ref, k_ref, v_ref, qi_ref, ki_ref, wi_ref, o_ref,
                keys_ref, hi_ref, lo_ref, ties_ref, s_ref, max_ref, sum_ref, acc_ref, *, t, topk):
    it = pl.program_id(1)
    width = q_ref.shape[2]
    heads = _head_lane_masks(t, width)
    causal = _local_causal(t)

    qidx = qi_ref[0]
    qi_stack = jnp.concatenate(
        [_stack_masked(qidx[:, g * width:(g + 1) * width], heads) for g in range(IDX_HEADS // 4)], axis=0)
    wi = wi_ref[0]
    w_rep = [jnp.broadcast_to(wi[:, h:h + 1], (t, LANES)) for h in range(IDX_HEADS)]

    def score_tile(j, diagonal):
        logit = _nt_dot(qi_stack, ki_ref[0, _tile_rows(j, t), :])
        cols = []
        for c in range(0, t, LANES):
            acc = jnp.zeros((t, LANES), F32)
            for h in range(IDX_HEADS):
                acc = acc + jnp.maximum(logit[h * t:(h + 1) * t, c:c + LANES], 0.0) * w_rep[h]
            cols.append(acc)
        isc = jnp.concatenate(cols, axis=1) + 0.0
        if diagonal:
            isc = jnp.where(causal, isc, -jnp.inf)
        bits = lax.bitcast_convert_type(isc, jnp.int32)
        key = bits ^ ((bits >> 31) & 0x7FFFFFFF)
        keys_ref[j] = key
        key_t = key.T
        hi_ref[j] = (key_t >> 16).astype(I16)
        lo_ref[j] = ((key_t & 0xFFFF) + I16_MIN).astype(I16)

    _for_each_tile(it, lambda j: score_tile(j, False))
    score_tile(it, True)

    def count(ref, pred):
        def one(j, cnt):
            return cnt + _sublane_fold(jnp.where(pred(ref[j]), jnp.int16(1), jnp.int16(0)), I16_ROWS)
        n = it + 1
        cnt = lax.fori_loop(0, n // 2, lambda i, cnt: one(2 * i + 1, one(2 * i, cnt)), jnp.zeros((I16_ROWS, t), I16))
        cnt = lax.cond(n % 2 == 1, lambda cnt: one(n - 1, cnt), lambda cnt: cnt, cnt)
        return jnp.sum(cnt.astype(jnp.int32), axis=0, keepdims=True)

    def kth_largest(ref, k):
        thr = jnp.where(count(ref, lambda x: x >= jnp.int16(0)) >= k, 0, jnp.full((1, t), I16_MIN, jnp.int32))

        def bit_step(i, thr):
            cand = thr | (1 << (14 - i))
            return jnp.where(count(ref, lambda x: x >= cand.astype(I16)) >= k, cand, thr)

        return lax.fori_loop(0, 15, bit_step, thr)

    thr_hi = kth_largest(hi_ref, topk)
    thr_hi16 = thr_hi.astype(I16)
    k_lo = topk - count(hi_ref, lambda x: x > thr_hi16)

    def keep_matching(j):
        lo_ref[j] = jnp.where(hi_ref[j] == thr_hi16, lo_ref[j], jnp.int16(I16_MIN))

    _for_each_tile(it + 1, keep_matching)

    thr_lo = kth_largest(lo_ref, k_lo)
    thr_lo16 = thr_lo.astype(I16)
    wanted = (k_lo - count(lo_ref, lambda x: x > thr_lo16)).astype(F32)
    thr = (thr_hi << 16) | (thr_lo - I16_MIN)
    thr_q = _lane_tile(jnp.broadcast_to(thr, (LANES, t)).T, t)
    wanted_q = _lane_tile(jnp.broadcast_to(wanted, (LANES, t)).T, t)

    q_stack = _stack_masked(q_ref[0], heads)
    r = lax.broadcasted_iota(jnp.int32, (t, t), 0)
    c = lax.broadcasted_iota(jnp.int32, (t, t), 1)
    before = (r < c).astype(BF16)
    ones = jnp.ones((t, LANES), BF16)
    _reset_stats(max_ref, sum_ref, acc_ref)

    ties_ref[...] = jnp.zeros(ties_ref.shape, F32)

    def select_tile(j, diagonal):
        key = keys_ref[j]
        tie = key == thr_q
        tie_b = jnp.where(tie, 1.0, 0.0).astype(BF16)
        tie_rank = _lane_tile(ties_ref[...], t) + _dot(tie_b, before)
        bias = jnp.where(key > thr_q, 0.0, jnp.where(tie, jnp.where(tie_rank < wanted_q, 0.0, MASKED), MASKED))
        if diagonal:
            bias = jnp.where(causal, bias, MASKED)
        s = _nt_dot(q_stack, k_ref[0, _tile_rows(j, t), :]) + jnp.concatenate([bias] * A_HEADS, axis=0)
        _record_scores(s, j, 0, s_ref, max_ref)
        ties_ref[...] += _dot(tie_b, ones)

    _for_each_tile(it, lambda j: select_tile(j, False))
    select_tile(it, True)
    _finish_max(max_ref)

    _for_each_tile(it + 1, lambda j: _accumulate_probs(
        j, 0, A_HEADS * t, v_ref[0, _tile_rows(j, t), :], s_ref, max_ref, sum_ref, acc_ref))

    o = [_normalised(h * t, t, sum_ref, acc_ref) for h in range(A_HEADS)]
    o_ref[0] = _interleave_pairs(o, t).astype(o_ref.dtype)


def _dsa_attention(qa, ka, va, qi, ki, wi):
    b, s, w = qa.shape
    t = ATT_TILE
    topk = min(INDEX_TOPK, s // 4)
    assert t >= topk
    row = lambda width: pl.BlockSpec((1, t, width), lambda bi, i: (bi, i, 0))
    full = lambda width: pl.BlockSpec((1, s, width), lambda bi, i: (bi, 0, 0))
    return pl.pallas_call(
        functools.partial(_dsa_kernel, t=t, topk=topk),
        grid=(b, s // t),
        in_specs=[row(w), full(w), full(va.shape[2]), row(qi.shape[2]), full(w), row(LANES)],
        out_specs=row(w),
        out_shape=jax.ShapeDtypeStruct((b, s, w), BF16),
        scratch_shapes=[pltpu.VMEM((s // t, t, t), jnp.int32),
                        pltpu.VMEM((s // t, t, t), I16), pltpu.VMEM((s // t, t, t), I16),
                        pltpu.VMEM((t, LANES), F32)]
                       + _ATT_SCRATCH(s // t, A_HEADS * t, t),
        compiler_params=_params("parallel", "arbitrary"),
        name="dsa_attention",
    )(qa, ka, va, qi, ki, wi)


def _merge_kernel(x_ref, oa_ref, ob_ref, oc_ref, gate_ref, wb_ref, wo_ref, o_ref):
    d = x_ref.shape[1]
    wa, wbw = oa_ref.shape[1], ob_ref.shape[1]
    merged = gate_ref[:, 0:d].astype(F32) * _dot(oa_ref[...], wb_ref[0:wa, :])
    merged = merged + gate_ref[:, d:2 * d].astype(F32) * _dot(ob_ref[...], wb_ref[wa:wa + wbw, :])
    merged = merged + gate_ref[:, 2 * d:3 * d].astype(F32) * _dot(oc_ref[...], wb_ref[wa + wbw:, :])
    o_ref[...] = x_ref[...] + _dot(merged.astype(BF16), wo_ref[...])


def _merge(x, oa, ob, oc, gate, wb, wo, tm):
    n, d = x.shape
    row = lambda w: pl.BlockSpec((tm, w), lambda i: (i, 0))
    return pl.pallas_call(
        _merge_kernel,
        grid=(n // tm,),
        in_specs=[row(d), row(oa.shape[1]), row(ob.shape[1]), row(oc.shape[1]), row(gate.shape[1]),
                  _resident(wb.shape), _resident(wo.shape)],
        out_specs=row(d),
        out_shape=jax.ShapeDtypeStruct((n, d), F32),
        compiler_params=_params("parallel"),
        name="merge_out",
    )(x, oa, ob, oc, gate, wb, wo)


def kernel(x, positions, norm_g, w_in, qk_norm_g, lambda_params, diff_subln_g, w_branch, w_out,
           ffn_w_gate, ffn_w_up, ffn_w_down):
    b, s, d = x.shape
    n = b * s
    depth = norm_g.shape[0]
    tm = min(512, n)
    assert n % tm == 0 and s % ATT_TILE == 0 and ATT_TILE == MOBA_BLOCK

    cos, sin = _rope_tables(positions, tm)
    xf = x.reshape(n, d)
    for layer in range(depth):
        lam_init = 0.8 - 0.6 * math.exp(-0.3 * layer)
        ffn_w = lambda i: (ffn_w_gate[layer, i].astype(BF16), ffn_w_up[layer, i].astype(BF16),
                           ffn_w_down[layer, i].astype(BF16))
        xf = _ffn(xf, norm_g[layer, 0][None, :], *ffn_w(0), tm)

        qkg = jnp.tile(qk_norm_g[layer], (1, LANES // HEAD_DIM))
        outs = _proj(xf, norm_g[layer, 1][None, :], qkg, cos, sin, _proj_weights(w_in[layer]), tm)
        p = {name: o.reshape(b, s, o.shape[1]) for (name, *_), o in zip(_PROJ_OUTPUTS, outs)}
        o_a = _dsa_attention(p["qa"], p["ka"], p["va"], p["qi"], p["ki"], p["wi"])
        o_b = _moba_attention(p["qb"], p["kb"], p["vb"])
        o_c = _diff_attention(p["qc"], p["kc"], p["vc"], lambda_params[layer], diff_subln_g[layer][None, :], lam_init)
        xf = _merge(xf, o_a.reshape(n, -1), o_b.reshape(n, -1), o_c.reshape(n, -1), p["gate"].reshape(n, -1),
                    w_branch[layer].astype(BF16), w_out[layer].astype(BF16), tm)

        xf = _ffn(xf, norm_g[layer, 2][None, :], *ffn_w(1), tm)
    return xf.reshape(b, s, d)
```

```python
import functools
import math

import jax
import jax.numpy as jnp
from jax import lax
from jax.experimental import pallas as pl
from jax.experimental.pallas import tpu as pltpu

F32 = jnp.float32
BF16 = jnp.bfloat16

HEAD_DIM = 64
ROT_DIM = HEAD_DIM // 4
ROT_HALF = ROT_DIM // 2
ROPE_THETA = 500000.0
EPS = 1e-6
A_HEADS = 4
IDX_HEADS = 8
IDX_DIM = 64
INDEX_TOPK = 256
B_HEADS = 4
MOBA_BLOCK = 256
MOBA_TOPK = 3
C_HEADS = 4
C_VDIM = 2 * HEAD_DIM

LANES = 128
SUBLANES = 8
MXU_DIM = 256
ATT_TILE = 256
MASKED = -1e30
VMEM_LIMIT = 56 * 1024 * 1024
LOG2E = 1.4426950408889634


def _nt_dot(a, b):
    return lax.dot_general(a, b, (((1,), (1,)), ((), ())), preferred_element_type=F32)


def _dot(a, b):
    return jnp.dot(a, b, preferred_element_type=F32)


def _resident(shape):
    nd = len(shape)
    return pl.BlockSpec(shape, lambda *_: (0,) * nd, pipeline_mode=pl.Buffered(1))


def _params(*sem):
    return pltpu.CompilerParams(dimension_semantics=sem, vmem_limit_bytes=VMEM_LIMIT)


def _rope_table_kernel(pos_ref, freq_ref, cos_ref, sin_ref):
    ang = pos_ref[...].astype(F32) * freq_ref[...]
    lane = lax.broadcasted_iota(jnp.int32, ang.shape, 1) % HEAD_DIM
    c = jnp.cos(ang)
    s = jnp.sin(ang)
    cos_ref[...] = jnp.where(lane < ROT_DIM, c, 1.0)
    sin_ref[...] = jnp.where(lane < ROT_HALF, -s, jnp.where(lane < ROT_DIM, s, 0.0))


def _rope_tables(positions, tm):
    n = positions.size
    pos = positions.reshape(n, 1)
    inv_freq = jnp.power(ROPE_THETA, -jnp.arange(0, ROT_DIM, 2, dtype=F32) / ROT_DIM)
    freq = jnp.tile(jnp.concatenate([inv_freq, inv_freq, jnp.zeros((HEAD_DIM - ROT_DIM,), F32)]), 2)[None, :]
    return pl.pallas_call(
        _rope_table_kernel,
        grid=(n // tm,),
        in_specs=[pl.BlockSpec((tm, 1), lambda i: (i, 0)), pl.BlockSpec((1, LANES), lambda i: (0, 0))],
        out_specs=[pl.BlockSpec((tm, LANES), lambda i: (i, 0))] * 2,
        out_shape=[jax.ShapeDtypeStruct((n, LANES), F32)] * 2,
        compiler_params=_params("parallel"),
        name="rope_tables",
    )(pos, freq)


def _rmsnorm_rows(x, g):
    return x * lax.rsqrt(jnp.mean(x * x, axis=-1, keepdims=True) + EPS) * g


def _ffn_kernel(x_ref, g_ref, wg_ref, wu_ref, wd_ref, o_ref, *, chunk):
    x = x_ref[...]
    h = _rmsnorm_rows(x, g_ref[...]).astype(BF16)
    acc = jnp.zeros(x.shape, F32)
    for c in range(0, wg_ref.shape[1], chunk):
        a = _dot(h, wg_ref[:, c:c + chunk])
        b = _dot(h, wu_ref[:, c:c + chunk])
        t = (a * jax.nn.sigmoid(a) * b).astype(BF16)
        acc = acc + _dot(t, wd_ref[c:c + chunk, :])
    o_ref[...] = x + 0.5 * acc


def _ffn(x, g, wg, wu, wd, tm):
    n, d = x.shape
    f = wg.shape[1]
    return pl.pallas_call(
        functools.partial(_ffn_kernel, chunk=MXU_DIM),
        grid=(n // tm,),
        in_specs=[pl.BlockSpec((tm, d), lambda i: (i, 0)), _resident((1, d)),
                  _resident((d, f)), _resident((d, f)), _resident((f, d))],
        out_specs=pl.BlockSpec((tm, d), lambda i: (i, 0)),
        out_shape=jax.ShapeDtypeStruct((n, d), F32),
        compiler_params=_params("parallel"),
        name="ffn",
    )(x, g, wg, wu, wd)


_PROJ_OUTPUTS = (
    ("qa", 256, "norm_rope_scale", 0, BF16),
    ("ka", 256, "norm_rope", 1, BF16),
    ("va", 128, "plain", None, BF16),
    ("qi", 512, "rope", None, BF16),
    ("ki", 256, "rope", None, BF16),
    ("wi", 128, "index_weight", None, F32),
    ("qb", 256, "norm_rope_scale", 2, BF16),
    ("kb", 256, "norm_rope", 3, BF16),
    ("vb", 256, "plain", None, BF16),
    ("qc", 512, "norm_rope_scale", 4, BF16),
    ("kc", 512, "norm_rope", 5, BF16),
    ("vc", 512, "plain", None, BF16),
    ("gate", 3072, "sigmoid", None, BF16),
)
_Q_SCALE = HEAD_DIM ** -0.5 * LOG2E


def _proj_kernel(x_ref, g_ref, qkg_ref, cos_ref, sin_ref, *refs):
    n_out = len(_PROJ_OUTPUTS)
    w_refs, o_refs = refs[:n_out], refs[n_out:]
    tm = x_ref.shape[0]
    h = _rmsnorm_rows(x_ref[...], g_ref[...]).astype(BF16)
    cos = cos_ref[...]
    sin = sin_ref[...]
    lane = lax.broadcasted_iota(jnp.int32, (tm, LANES), 1)
    take_upper = (lane % HEAD_DIM) < ROT_HALF
    r = lax.broadcasted_iota(jnp.int32, (LANES, LANES), 0) // HEAD_DIM
    c = lax.broadcasted_iota(jnp.int32, (LANES, LANES), 1) // HEAD_DIM
    same_head = (r == c).astype(BF16)

    def epilogue(y, kind, g_idx):
        if kind in ("norm_rope_scale", "norm_rope"):
            ssq = _dot((y * y).astype(BF16), same_head)
            y = y * lax.rsqrt(ssq * (1.0 / HEAD_DIM) + EPS) * qkg_ref[g_idx:g_idx + 1, :]
        if kind in ("norm_rope_scale", "norm_rope", "rope"):
            partner = jnp.where(take_upper, pltpu.roll(y, LANES - ROT_HALF, 1), pltpu.roll(y, ROT_HALF, 1))
            y = y * cos + partner * sin
        if kind == "norm_rope_scale":
            y = y * _Q_SCALE
        if kind == "index_weight":
            y = y * ((IDX_HEADS * IDX_DIM) ** -0.5)
        if kind == "sigmoid":
            y = jax.nn.sigmoid(y)
        return y

    for (name, width, kind, g_idx, dtype), w_ref, o_ref in zip(_PROJ_OUTPUTS, w_refs, o_refs):
        step = min(width, MXU_DIM)
        for c0 in range(0, width, step):
            y = _dot(h, w_ref[:, c0:c0 + step])
            for s0 in range(0, step, LANES):
                o_ref[:, c0 + s0:c0 + s0 + LANES] = epilogue(y[:, s0:s0 + LANES], kind, g_idx).astype(dtype)


def _proj_weights(w_in):
    splits = (256, 64, 64, 512, 64, 8, 256, 256, 256, 512, 512, 512, 1024, 1024, 1024)
    offs = [0]
    for s in splits:
        offs.append(offs[-1] + s)
    qa, ka, va, qi, ki, wi, qb, kb, vb, qc, kc, vc, ga, gb, gc = [w_in[:, offs[i]:offs[i + 1]] for i in range(len(splits))]
    d = w_in.shape[0]
    out = dict(qa=qa, ka=jnp.tile(ka, (1, A_HEADS)), va=jnp.tile(va, (1, LANES // HEAD_DIM)), qi=qi,
               ki=jnp.tile(ki, (1, 4)), wi=jnp.concatenate([wi, jnp.zeros((d, LANES - IDX_HEADS), w_in.dtype)], axis=1),
               qb=qb, kb=kb, vb=vb, qc=qc, kc=kc, vc=vc, gate=jnp.concatenate([ga, gb, gc], axis=1))
    return [out[name].astype(BF16) for name, *_ in _PROJ_OUTPUTS]


def _proj(x, g, qkg, cos, sin, weights, tm):
    n, d = x.shape
    row = lambda w: pl.BlockSpec((tm, w), lambda i: (i, 0))
    return pl.pallas_call(
        _proj_kernel,
        grid=(n // tm,),
        in_specs=[row(d), _resident((1, d)), _resident(qkg.shape), row(LANES), row(LANES)]
                 + [_resident(w.shape) for w in weights],
        out_specs=[row(width) for _, width, *_ in _PROJ_OUTPUTS],
        out_shape=[jax.ShapeDtypeStruct((n, width), dtype) for _, width, _, _, dtype in _PROJ_OUTPUTS],
        compiler_params=_params("parallel"),
        name="in_proj",
    )(x, g, qkg, cos, sin, *weights)


def _lane_fold(x, op):
    out = x[:, :LANES]
    for c in range(LANES, x.shape[1], LANES):
        out = op(out, x[:, c:c + LANES])
    return out


def _lane_tile(x, width):
    return jnp.concatenate([x] * (width // LANES), axis=1)


def _reset_stats(max_ref, sum_ref, acc_ref):
    max_ref[...] = jnp.full(max_ref.shape, MASKED, F32)
    sum_ref[...] = jnp.zeros(sum_ref.shape, F32)
    acc_ref[...] = jnp.zeros(acc_ref.shape, F32)


def _record_scores(s, j, r0, s_ref, max_ref):
    n = s.shape[0]
    s_ref[j, r0:r0 + n, :] = s
    max_ref[r0:r0 + n, :] = jnp.maximum(max_ref[r0:r0 + n, :], _lane_fold(s, jnp.maximum))


def _finish_max(max_ref):
    m = max_ref[...]
    max_ref[...] = jnp.broadcast_to(jnp.max(m, axis=-1, keepdims=True), m.shape)


def _accumulate_probs(j, r0, n, v, s_ref, max_ref, sum_ref, acc_ref):
    s = s_ref[j, r0:r0 + n, :]
    p = jnp.exp2(s - _lane_tile(max_ref[r0:r0 + n, :], s.shape[1]))
    sum_ref[r0:r0 + n, :] += _lane_fold(p, jnp.add)
    acc_ref[r0:r0 + n, :] += _dot(p.astype(BF16), v)


def _normalised(r0, n, sum_ref, acc_ref):
    return acc_ref[r0:r0 + n, :] / jnp.sum(sum_ref[r0:r0 + n, :], axis=-1, keepdims=True)


def _head_lane_masks(rows, width):
    lane = lax.broadcasted_iota(jnp.int32, (rows, width), 1)
    return [(lane >= h * HEAD_DIM) & (lane < (h + 1) * HEAD_DIM) for h in range(width // HEAD_DIM)]


def _stack_masked(q, masks):
    zero = jnp.zeros_like(q)
    return jnp.concatenate([jnp.where(m, q, zero) for m in masks], axis=0)


def _local_causal(t):
    row = lax.broadcasted_iota(jnp.int32, (t, t), 0)
    col = lax.broadcasted_iota(jnp.int32, (t, t), 1)
    return col <= row


def _tile_rows(j, t):
    return pl.ds(pl.multiple_of(j * t, t), t)


def _for_each_tile(n, body, group=2):
    @pl.loop(0, n // group)
    def _(i):
        for k in range(group):
            body(group * i + k)

    part = group // 2
    while part >= 1:
        @pl.when(n % (2 * part) >= part)
        def _():
            for k in range(part):
                body(n - n % (2 * part) + k)
        part //= 2


def _interleave_pairs(o, t):
    lane = lax.broadcasted_iota(jnp.int32, (t, LANES), 1)
    low = lane < HEAD_DIM
    return jnp.concatenate([jnp.where(low, o[0], o[1]), jnp.where(low, o[2], o[3])], axis=1)


_ATT_SCRATCH = lambda n_tiles, rows, t: [
    pltpu.VMEM((n_tiles, rows, t), F32),
    pltpu.VMEM((rows, LANES), F32),
    pltpu.VMEM((rows, LANES), F32),
    pltpu.VMEM((rows, LANES), F32),
]


def _diff_kernel(lam_ref, g_ref, q_ref, k_ref, v_ref, o_ref, s_ref, max_ref, sum_ref, acc_ref, *, lam_init, t):
    it = pl.program_id(1)
    lane = lax.broadcasted_iota(jnp.int32, (t, C_VDIM), 1)
    maps = [lane < HEAD_DIM, lane >= HEAD_DIM]
    q_all = q_ref[0]
    qs = [_stack_masked(q_all[:, h * C_VDIM:(h + 1) * C_VDIM], maps) for h in range(C_HEADS)]
    _reset_stats(max_ref, sum_ref, acc_ref)
    causal = jnp.where(_local_causal(t), 0.0, MASKED)
    causal2 = jnp.concatenate([causal, causal], axis=0)

    def scores(j):
        rows = _tile_rows(j, t)
        bias = jnp.where(j == it, causal2, 0.0)
        for h in range(C_HEADS):
            s = _nt_dot(qs[h], k_ref[0, rows, h * C_VDIM:(h + 1) * C_VDIM]) + bias
            _record_scores(s, j, h * 2 * t, s_ref, max_ref)

    _for_each_tile(it + 1, scores)
    _finish_max(max_ref)

    def probs(j):
        rows = _tile_rows(j, t)
        for h in range(C_HEADS):
            _accumulate_probs(j, h * 2 * t, 2 * t, v_ref[0, rows, h * C_VDIM:(h + 1) * C_VDIM],
                              s_ref, max_ref, sum_ref, acc_ref)

    _for_each_tile(it + 1, probs)

    lp = lam_ref[...]
    lam = (jnp.exp(jnp.sum(lp[0:1] * lp[1:2], axis=-1, keepdims=True))
           - jnp.exp(jnp.sum(lp[2:3] * lp[3:4], axis=-1, keepdims=True)) + lam_init)
    for h in range(C_HEADS):
        o = _normalised(h * 2 * t, t, sum_ref, acc_ref) - lam * _normalised(h * 2 * t + t, t, sum_ref, acc_ref)
        o = o * lax.rsqrt(jnp.mean(o * o, axis=-1, keepdims=True) + EPS) * g_ref[...] * (1.0 - lam_init)
        o_ref[0, :, h * C_VDIM:(h + 1) * C_VDIM] = o.astype(o_ref.dtype)


def _diff_attention(qc, kc, vc, lam_p, subln_g, lam_init):
    b, s, w = qc.shape
    t = ATT_TILE
    qspec = pl.BlockSpec((1, t, w), lambda bi, i: (bi, i, 0))
    kvspec = pl.BlockSpec((1, s, w), lambda bi, i: (bi, 0, 0))
    return pl.pallas_call(
        functools.partial(_diff_kernel, lam_init=lam_init, t=t),
        grid=(b, s // t),
        in_specs=[pl.BlockSpec(lam_p.shape, lambda bi, i: (0, 0)),
                  pl.BlockSpec((1, C_VDIM), lambda bi, i: (0, 0)), qspec, kvspec, kvspec],
        out_specs=qspec,
        out_shape=jax.ShapeDtypeStruct((b, s, w), BF16),
        scratch_shapes=_ATT_SCRATCH(s // t, 2 * C_HEADS * t, t),
        compiler_params=_params("parallel", "arbitrary"),
        name="diff_attention",
    )(lam_p, subln_g, qc, kc, vc)


def _moba_kernel(q_ref, k_ref, v_ref, o_ref, kmean_hi, kmean_lo, bias_ref, s_ref, max_ref, sum_ref, acc_ref, *, t):
    it = pl.program_id(1)
    n_blk = k_ref.shape[1] // t
    width = q_ref.shape[2]

    @pl.when(it == 0)
    def _():
        kmean_hi[...] = jnp.zeros_like(kmean_hi)
        kmean_lo[...] = jnp.zeros_like(kmean_lo)
        for n in range(n_blk):
            km = jnp.mean(k_ref[0, n * t:(n + 1) * t, :].astype(F32), axis=0, keepdims=True)
            hi = km.astype(BF16)
            kmean_hi[n:n + 1, :] = hi
            kmean_lo[n:n + 1, :] = (km - hi.astype(F32)).astype(BF16)

    heads = _head_lane_masks(t, width)
    q_stack = _stack_masked(q_ref[0], heads)

    no_bias = jnp.zeros(bias_ref.shape[1:], F32)
    bias_ref[it] = no_bias

    @pl.when(it <= MOBA_TOPK)
    def _():
        for n in range(min(MOBA_TOPK, n_blk)):
            @pl.when(n < it)
            def _():
                bias_ref[n] = no_bias

    @pl.when(it > MOBA_TOPK)
    def _():
        blk = lax.broadcasted_iota(jnp.int32, (SUBLANES, t), 0)
        past = blk < it
        pad = jnp.full((LANES - SUBLANES, t), MASKED, F32)
        bias_q = []
        for h in range(B_HEADS):
            qh = q_stack[h * t:(h + 1) * t]
            gate = (_nt_dot(kmean_hi[...], qh) + _nt_dot(kmean_lo[...], qh))[:SUBLANES]
            gate = jnp.where(past, gate, -jnp.inf)
            rank = jnp.zeros((SUBLANES, t), jnp.int32)
            for n in range(n_blk):
                gn = gate[n:n + 1, :]
                rank = rank + jnp.where((gn > gate) | ((gn == gate) & (blk > n)), 1, 0)
            bias = jnp.where((rank < MOBA_TOPK) & past, 0.0, MASKED)
            bias_q.append(jnp.concatenate([bias, pad], axis=0).T.astype(BF16))
        bias_q = jnp.concatenate(bias_q, axis=0)
        lane_of = lax.broadcasted_iota(jnp.int32, (LANES, LANES), 0)
        for n in range(n_blk - 1):
            @pl.when(n < it)
            def _():
                pick = (lane_of == n).astype(BF16)
                bias_ref[n] = _dot(bias_q, pick)

    _reset_stats(max_ref, sum_ref, acc_ref)
    causal = jnp.where(_local_causal(t), 0.0, MASKED)
    causal4 = jnp.concatenate([causal] * B_HEADS, axis=0)

    def scores(j):
        s = _nt_dot(q_stack, k_ref[0, _tile_rows(j, t), :]) + _lane_tile(bias_ref[j], t)
        _record_scores(s + jnp.where(j == it, causal4, 0.0), j, 0, s_ref, max_ref)

    _for_each_tile(it + 1, scores, group=4)
    _finish_max(max_ref)

    def probs(j):
        rows = _tile_rows(j, t)
        for pair in range(B_HEADS // 2):
            _accumulate_probs(j, pair * 2 * t, 2 * t, v_ref[0, rows, pair * LANES:(pair + 1) * LANES],
                              s_ref, max_ref, sum_ref, acc_ref)

    _for_each_tile(it + 1, probs, group=4)

    o = [_normalised(h * t, t, sum_ref, acc_ref) for h in range(B_HEADS)]
    o_ref[0] = _interleave_pairs(o, t).astype(o_ref.dtype)


def _moba_attention(qb, kb, vb):
    b, s, w = qb.shape
    t = MOBA_BLOCK
    assert s // t <= SUBLANES
    qspec = pl.BlockSpec((1, t, w), lambda bi, i: (bi, i, 0))
    kvspec = pl.BlockSpec((1, s, w), lambda bi, i: (bi, 0, 0))
    return pl.pallas_call(
        functools.partial(_moba_kernel, t=t),
        grid=(b, s // t),
        in_specs=[qspec, kvspec, kvspec],
        out_specs=qspec,
        out_shape=jax.ShapeDtypeStruct((b, s, w), BF16),
        scratch_shapes=[pltpu.VMEM((LANES, w), BF16), pltpu.VMEM((LANES, w), BF16),
                        pltpu.VMEM((s // t, B_HEADS * t, LANES), F32)] + _ATT_SCRATCH(s // t, B_HEADS * t, t),
        compiler_params=_params("parallel", "arbitrary"),
        name="moba_attention",
    )(qb, kb, vb)


def _sublane_fold(x, rows):
    parts = [x[r:r + rows] for r in range(0, x.shape[0], rows)]
    while len(parts) > 1:
        parts = [parts[i] + parts[i + 1] for i in range(0, len(parts), 2)]
    return parts[0]


I16 = jnp.int16
I16_ROWS = 2 * SUBLANES
I16_MIN = -(2 ** 15)


def _dsa_kernel(q_ref, k_ref, v_ref, qi_ref, ki_ref, wi_ref, o_ref,
                keys_ref, hi_ref, lo_ref, ties_ref, s_ref, max_ref, sum_ref, acc_ref, *, t, topk):
    it = pl.program_id(1)
    width = q_ref.shape[2]
    heads = _head_lane_masks(t, width)
    causal = _local_causal(t)

    qidx = qi_ref[0]
    qi_stack = jnp.concatenate(
        [_stack_masked(qidx[:, g * width:(g + 1) * width], heads) for g in range(IDX_HEADS // 4)], axis=0)
    wi = wi_ref[0]
    w_rep = [jnp.broadcast_to(wi[:, h:h + 1], (t, LANES)) for h in range(IDX_HEADS)]

    def score_tile(j):
        logit = _nt_dot(qi_stack, ki_ref[0, _tile_rows(j, t), :])
        cols = []
        for c in range(0, t, LANES):
            acc = jnp.zeros((t, LANES), F32)
            for h in range(IDX_HEADS):
                acc = acc + jnp.maximum(logit[h * t:(h + 1) * t, c:c + LANES], 0.0) * w_rep[h]
            cols.append(acc)
        isc = jnp.concatenate(cols, axis=1) + 0.0
        isc = jnp.where(j == it, jnp.where(causal, isc, -jnp.inf), isc)
        bits = lax.bitcast_convert_type(isc, jnp.int32)
        key = bits ^ ((bits >> 31) & 0x7FFFFFFF)
        keys_ref[j] = key
        key_t = key.T
        hi_ref[j] = (key_t >> 16).astype(I16)
        lo_ref[j] = ((key_t & 0xFFFF) + I16_MIN).astype(I16)

    _for_each_tile(it + 1, score_tile, group=4)
    n_pairs = it // 2 + 1

    @pl.when(it % 2 == 0)
    def _():
        hi_ref[it + 1] = jnp.full((t, t), I16_MIN, I16)
        lo_ref[it + 1] = jnp.full((t, t), I16_MIN, I16)

    def count(ref, pred):
        def one(j, cnt):
            return cnt + _sublane_fold(jnp.where(pred(ref[j]), jnp.int16(1), jnp.int16(0)), I16_ROWS)
        cnt = lax.fori_loop(0, n_pairs, lambda i, cnt: one(2 * i + 1, one(2 * i, cnt)), jnp.zeros((I16_ROWS, t), I16))
        return jnp.sum(cnt.astype(jnp.int32), axis=0, keepdims=True)

    def kth_largest(ref, k):
        thr = jnp.where(count(ref, lambda x: x >= jnp.int16(0)) >= k, 0, jnp.full((1, t), I16_MIN, jnp.int32))

        def bit_step(i, thr):
            cand = thr | (1 << (14 - i))
            return jnp.where(count(ref, lambda x: x >= cand.astype(I16)) >= k, cand, thr)

        return lax.fori_loop(0, 15, bit_step, thr)

    thr_hi = kth_largest(hi_ref, topk)
    thr_hi16 = thr_hi.astype(I16)
    k_lo = topk - count(hi_ref, lambda x: x > thr_hi16)

    def keep_matching(j):
        lo_ref[j] = jnp.where(hi_ref[j] == thr_hi16, lo_ref[j], jnp.int16(I16_MIN))

    @pl.loop(0, n_pairs)
    def _(i):
        keep_matching(2 * i)
        keep_matching(2 * i + 1)

    thr_lo = kth_largest(lo_ref, k_lo)
    thr_lo16 = thr_lo.astype(I16)
    wanted = (k_lo - count(lo_ref, lambda x: x > thr_lo16)).astype(F32)
    thr = (thr_hi << 16) | (thr_lo - I16_MIN)
    thr_q = _lane_tile(jnp.broadcast_to(thr, (LANES, t)).T, t)
    wanted_q = _lane_tile(jnp.broadcast_to(wanted, (LANES, t)).T, t)

    q_stack = _stack_masked(q_ref[0], heads)
    r = lax.broadcasted_iota(jnp.int32, (t, t), 0)
    c = lax.broadcasted_iota(jnp.int32, (t, t), 1)
    before = (r < c).astype(BF16)
    ones = jnp.ones((t, LANES), BF16)
    _reset_stats(max_ref, sum_ref, acc_ref)

    ties_ref[...] = jnp.zeros(ties_ref.shape, F32)

    not_causal = jnp.where(causal, 0.0, MASKED)

    def select_tile(j):
        key = keys_ref[j]
        tie = key == thr_q
        tie_b = jnp.where(tie, 1.0, 0.0).astype(BF16)
        tie_rank = _lane_tile(ties_ref[...], t) + _dot(tie_b, before)
        bias = jnp.where(key > thr_q, 0.0, jnp.where(tie, jnp.where(tie_rank < wanted_q, 0.0, MASKED), MASKED))
        bias = bias + jnp.where(j == it, not_causal, 0.0)
        s = _nt_dot(q_stack, k_ref[0, _tile_rows(j, t), :]) + jnp.concatenate([bias] * A_HEADS, axis=0)
        _record_scores(s, j, 0, s_ref, max_ref)
        ties_ref[...] += _dot(tie_b, ones)

    _for_each_tile(it + 1, select_tile)
    _finish_max(max_ref)

    _for_each_tile(it + 1, lambda j: _accumulate_probs(
        j, 0, A_HEADS * t, v_ref[0, _tile_rows(j, t), :], s_ref, max_ref, sum_ref, acc_ref), group=4)

    o = [_normalised(h * t, t, sum_ref, acc_ref) for h in range(A_HEADS)]
    o_ref[0] = _interleave_pairs(o, t).astype(o_ref.dtype)


def _dsa_attention(qa, ka, va, qi, ki, wi):
    b, s, w = qa.shape
    t = ATT_TILE
    topk = min(INDEX_TOPK, s // 4)
    assert t >= topk
    row = lambda width: pl.BlockSpec((1, t, width), lambda bi, i: (bi, i, 0))
    full = lambda width: pl.BlockSpec((1, s, width), lambda bi, i: (bi, 0, 0))
    return pl.pallas_call(
        functools.partial(_dsa_kernel, t=t, topk=topk),
        grid=(b, s // t),
        in_specs=[row(w), full(w), full(va.shape[2]), row(qi.shape[2]), full(w), row(LANES)],
        out_specs=row(w),
        out_shape=jax.ShapeDtypeStruct((b, s, w), BF16),
        scratch_shapes=[pltpu.VMEM((s // t, t, t), jnp.int32),
                        pltpu.VMEM((s // t + s // t % 2, t, t), I16), pltpu.VMEM((s // t + s // t % 2, t, t), I16),
                        pltpu.VMEM((t, LANES), F32)]
                       + _ATT_SCRATCH(s // t, A_HEADS * t, t),
        compiler_params=_params("parallel", "arbitrary"),
        name="dsa_attention",
    )(qa, ka, va, qi, ki, wi)


def _merge_kernel(x_ref, oa_ref, ob_ref, oc_ref, gate_ref, wb_ref, wo_ref, o_ref):
    d = x_ref.shape[1]
    wa, wbw = oa_ref.shape[1], ob_ref.shape[1]
    merged = gate_ref[:, 0:d].astype(F32) * _dot(oa_ref[...], wb_ref[0:wa, :])
    merged = merged + gate_ref[:, d:2 * d].astype(F32) * _dot(ob_ref[...], wb_ref[wa:wa + wbw, :])
    merged = merged + gate_ref[:, 2 * d:3 * d].astype(F32) * _dot(oc_ref[...], wb_ref[wa + wbw:, :])
    o_ref[...] = x_ref[...] + _dot(merged.astype(BF16), wo_ref[...])


def _merge(x, oa, ob, oc, gate, wb, wo, tm):
    n, d = x.shape
    row = lambda w: pl.BlockSpec((tm, w), lambda i: (i, 0))
    return pl.pallas_call(
        _merge_kernel,
        grid=(n // tm,),
        in_specs=[row(d), row(oa.shape[1]), row(ob.shape[1]), row(oc.shape[1]), row(gate.shape[1]),
                  _resident(wb.shape), _resident(wo.shape)],
        out_specs=row(d),
        out_shape=jax.ShapeDtypeStruct((n, d), F32),
        compiler_params=_params("parallel"),
        name="merge_out",
    )(x, oa, ob, oc, gate, wb, wo)


def kernel(x, positions, norm_g, w_in, qk_norm_g, lambda_params, diff_subln_g, w_branch, w_out,
           ffn_w_gate, ffn_w_up, ffn_w_down):
    b, s, d = x.shape
    n = b * s
    depth = norm_g.shape[0]
    tm = min(512, n)
    assert n % tm == 0 and s % ATT_TILE == 0 and ATT_TILE == MOBA_BLOCK

    cos, sin = _rope_tables(positions, tm)
    xf = x.reshape(n, d)
    for layer in range(depth):
        lam_init = 0.8 - 0.6 * math.exp(-0.3 * layer)
        ffn_w = lambda i: (ffn_w_gate[layer, i].astype(BF16), ffn_w_up[layer, i].astype(BF16),
                           ffn_w_down[layer, i].astype(BF16))
        xf = _ffn(xf, norm_g[layer, 0][None, :], *ffn_w(0), tm)

        qkg = jnp.tile(qk_norm_g[layer], (1, LANES // HEAD_DIM))
        outs = _proj(xf, norm_g[layer, 1][None, :], qkg, cos, sin, _proj_weights(w_in[layer]), tm)
        p = {name: o.reshape(b, s, o.shape[1]) for (name, *_), o in zip(_PROJ_OUTPUTS, outs)}
        o_a = _dsa_attention(p["qa"], p["ka"], p["va"], p["qi"], p["ki"], p["wi"])
        o_b = _moba_attention(p["qb"], p["kb"], p["vb"])
        o_c = _diff_attention(p["qc"], p["kc"], p["vc"], lambda_params[layer], diff_subln_g[layer][None, :], lam_init)
        xf = _merge(xf, o_a.reshape(n, -1), o_b.reshape(n, -1), o_c.reshape(n, -1), p["gate"].reshape(n, -1),
                    w_branch[layer].astype(BF16), w_out[layer].astype(BF16), tm)

        xf = _ffn(xf, norm_g[layer, 2][None, :], *ffn_w(1), tm)
    return xf.reshape(b, s, d)
```

```python
import functools
import math

import jax
import jax.numpy as jnp
from jax import lax
from jax.experimental import pallas as pl
from jax.experimental.pallas import tpu as pltpu

F32 = jnp.float32
BF16 = jnp.bfloat16

HEAD_DIM = 64
ROT_DIM = HEAD_DIM // 4
ROT_HALF = ROT_DIM // 2
ROPE_THETA = 500000.0
EPS = 1e-6
A_HEADS = 4
IDX_HEADS = 8
IDX_DIM = 64
INDEX_TOPK = 256
B_HEADS = 4
MOBA_BLOCK = 256
MOBA_TOPK = 3
C_HEADS = 4
C_VDIM = 2 * HEAD_DIM

LANES = 128
SUBLANES = 8
MXU_DIM = 256
ATT_TILE = 256
MASKED = -1e30
VMEM_LIMIT = 56 * 1024 * 1024
LOG2E = 1.4426950408889634


def _nt_dot(a, b):
    return lax.dot_general(a, b, (((1,), (1,)), ((), ())), preferred_element_type=F32)


def _dot(a, b):
    return jnp.dot(a, b, preferred_element_type=F32)


def _resident(shape):
    nd = len(shape)
    return pl.BlockSpec(shape, lambda *_: (0,) * nd, pipeline_mode=pl.Buffered(1))


def _params(*sem):
    return pltpu.CompilerParams(dimension_semantics=sem, vmem_limit_bytes=VMEM_LIMIT)


def _rope_table_kernel(pos_ref, freq_ref, cos_ref, sin_ref):
    ang = pos_ref[...].astype(F32) * freq_ref[...]
    lane = lax.broadcasted_iota(jnp.int32, ang.shape, 1) % HEAD_DIM
    c = jnp.cos(ang)
    s = jnp.sin(ang)
    cos_ref[...] = jnp.where(lane < ROT_DIM, c, 1.0)
    sin_ref[...] = jnp.where(lane < ROT_HALF, -s, jnp.where(lane < ROT_DIM, s, 0.0))


def _rope_tables(positions, tm):
    n = positions.size
    pos = positions.reshape(n, 1)
    inv_freq = jnp.power(ROPE_THETA, -jnp.arange(0, ROT_DIM, 2, dtype=F32) / ROT_DIM)
    freq = jnp.tile(jnp.concatenate([inv_freq, inv_freq, jnp.zeros((HEAD_DIM - ROT_DIM,), F32)]), 2)[None, :]
    return pl.pallas_call(
        _rope_table_kernel,
        grid=(n // tm,),
        in_specs=[pl.BlockSpec((tm, 1), lambda i: (i, 0)), pl.BlockSpec((1, LANES), lambda i: (0, 0))],
        out_specs=[pl.BlockSpec((tm, LANES), lambda i: (i, 0))] * 2,
        out_shape=[jax.ShapeDtypeStruct((n, LANES), F32)] * 2,
        compiler_params=_params("parallel"),
        name="rope_tables",
    )(pos, freq)


def _rmsnorm_rows(x, g):
    return x * lax.rsqrt(jnp.mean(x * x, axis=-1, keepdims=True) + EPS) * g


def _ffn_kernel(x_ref, g_ref, wg_ref, wu_ref, wd_ref, o_ref, *, chunk):
    x = x_ref[...]
    h = _rmsnorm_rows(x, g_ref[...]).astype(BF16)
    acc = jnp.zeros(x.shape, F32)
    for c in range(0, wg_ref.shape[1], chunk):
        a = _dot(h, wg_ref[:, c:c + chunk])
        b = _dot(h, wu_ref[:, c:c + chunk])
        t = (a * jax.nn.sigmoid(a) * b).astype(BF16)
        acc = acc + _dot(t, wd_ref[c:c + chunk, :])
    o_ref[...] = x + 0.5 * acc


def _ffn(x, g, wg, wu, wd, tm):
    n, d = x.shape
    f = wg.shape[1]
    return pl.pallas_call(
        functools.partial(_ffn_kernel, chunk=MXU_DIM),
        grid=(n // tm,),
        in_specs=[pl.BlockSpec((tm, d), lambda i: (i, 0)), _resident((1, d)),
                  _resident((d, f)), _resident((d, f)), _resident((f, d))],
        out_specs=pl.BlockSpec((tm, d), lambda i: (i, 0)),
        out_shape=jax.ShapeDtypeStruct((n, d), F32),
        compiler_params=_params("parallel"),
        name="ffn",
    )(x, g, wg, wu, wd)


_PROJ_COLUMNS = (
    ("qa", 256, "norm_rope_scale", 0),
    ("shared", 256, "shared", 1),
    ("qi", 512, "rope", None),
    ("wi", 128, "index_weight", None),
    ("qb", 256, "norm_rope_scale", 2),
    ("kb", 256, "norm_rope", 3),
    ("vb", 256, "plain", None),
    ("qc", 512, "norm_rope_scale", 4),
    ("kc", 512, "norm_rope", 5),
    ("vc", 512, "plain", None),
    ("gate", 3072, "sigmoid", None),
)
_PROJ_OUTPUTS = (
    ("qa", 256, BF16), ("ka", 256, BF16), ("va", 128, BF16), ("qi", 512, BF16), ("ki", 256, BF16), ("wi", 128, F32),
    ("qb", 256, BF16), ("kb", 256, BF16), ("vb", 256, BF16), ("qc", 512, BF16), ("kc", 512, BF16), ("vc", 512, BF16),
    ("gate", 3072, BF16),
)
_Q_SCALE = HEAD_DIM ** -0.5 * LOG2E


def _proj_kernel(x_ref, g_ref, qkg_ref, cos_ref, sin_ref, w_ref, *o_refs):
    out = {name: ref for (name, _, _), ref in zip(_PROJ_OUTPUTS, o_refs)}
    tm = x_ref.shape[0]
    h = _rmsnorm_rows(x_ref[...], g_ref[...]).astype(BF16)
    cos = cos_ref[...]
    sin = sin_ref[...]
    lane = lax.broadcasted_iota(jnp.int32, (tm, LANES), 1)
    take_upper = (lane % HEAD_DIM) < ROT_HALF
    low_half = lane < HEAD_DIM
    r = lax.broadcasted_iota(jnp.int32, (LANES, LANES), 0) // HEAD_DIM
    c = lax.broadcasted_iota(jnp.int32, (LANES, LANES), 1) // HEAD_DIM
    same_head = (r == c).astype(BF16)

    def epilogue(y, kind, g_idx):
        if kind in ("norm_rope_scale", "norm_rope"):
            ssq = _dot((y * y).astype(BF16), same_head)
            y = y * lax.rsqrt(ssq * (1.0 / HEAD_DIM) + EPS) * qkg_ref[g_idx:g_idx + 1, :]
        if kind in ("norm_rope_scale", "norm_rope", "rope"):
            partner = jnp.where(take_upper, pltpu.roll(y, LANES - ROT_HALF, 1), pltpu.roll(y, ROT_HALF, 1))
            y = y * cos + partner * sin
        if kind == "norm_rope_scale":
            y = y * _Q_SCALE
        if kind == "index_weight":
            y = y * ((IDX_HEADS * IDX_DIM) ** -0.5)
        if kind == "sigmoid":
            y = jax.nn.sigmoid(y)
        return y

    def on_both_halves(y, half):
        swapped = pltpu.roll(y, HEAD_DIM, 1)
        return jnp.where(low_half, y, swapped) if half == 0 else jnp.where(low_half, swapped, y)

    col = 0
    for name, width, kind, g_idx in _PROJ_COLUMNS:
        step = min(width, MXU_DIM)
        for c0 in range(0, width, step):
            y = _dot(h, w_ref[:, col + c0:col + c0 + step])
            if kind == "shared":
                key_value, index_key = y[:, :LANES], y[:, LANES:]
                ka = on_both_halves(epilogue(key_value, "norm_rope", g_idx), 0).astype(BF16)
                ki = on_both_halves(epilogue(index_key, "rope", None), 0).astype(BF16)
                for s0 in range(0, out["ka"].shape[1], LANES):
                    out["ka"][:, s0:s0 + LANES] = ka
                    out["ki"][:, s0:s0 + LANES] = ki
                out["va"][...] = on_both_halves(key_value, 1).astype(BF16)
                continue
            o_ref = out[name]
            for s0 in range(0, step, LANES):
                o_ref[:, c0 + s0:c0 + s0 + LANES] = epilogue(y[:, s0:s0 + LANES], kind, g_idx).astype(o_ref.dtype)
        col += width


def _proj_weights(w_in):
    splits = (256, 64, 64, 512, 64, 8, 256, 256, 256, 512, 512, 512, 1024, 1024, 1024)
    offs = [0]
    for s in splits:
        offs.append(offs[-1] + s)
    qa, ka, va, qi, ki, wi, qb, kb, vb, qc, kc, vc, ga, gb, gc = [w_in[:, offs[i]:offs[i + 1]] for i in range(len(splits))]
    zeros = lambda n: jnp.zeros((w_in.shape[0], n), w_in.dtype)
    groups = dict(qa=qa, shared=jnp.concatenate([ka, va, ki, zeros(HEAD_DIM)], axis=1), qi=qi,
                  wi=jnp.concatenate([wi, zeros(LANES - IDX_HEADS)], axis=1),
                  qb=qb, kb=kb, vb=vb, qc=qc, kc=kc, vc=vc, gate=jnp.concatenate([ga, gb, gc], axis=1))
    return jnp.concatenate([groups[name] for name, *_ in _PROJ_COLUMNS], axis=1).astype(BF16)


def _proj(x, g, qkg, cos, sin, w, tm):
    n, d = x.shape
    row = lambda width: pl.BlockSpec((tm, width), lambda i: (i, 0))
    return pl.pallas_call(
        _proj_kernel,
        grid=(n // tm,),
        in_specs=[row(d), _resident((1, d)), _resident(qkg.shape), row(LANES), row(LANES), _resident(w.shape)],
        out_specs=[row(width) for _, width, _ in _PROJ_OUTPUTS],
        out_shape=[jax.ShapeDtypeStruct((n, width), dtype) for _, width, dtype in _PROJ_OUTPUTS],
        compiler_params=_params("parallel"),
        name="in_proj",
    )(x, g, qkg, cos, sin, w)


def _lane_fold(x, op):
    out = x[:, :LANES]
    for c in range(LANES, x.shape[1], LANES):
        out = op(out, x[:, c:c + LANES])
    return out


def _lane_tile(x, width):
    return jnp.concatenate([x] * (width // LANES), axis=1)


def _reset_stats(max_ref, sum_ref, acc_ref):
    max_ref[...] = jnp.full(max_ref.shape, MASKED, F32)
    sum_ref[...] = jnp.zeros(sum_ref.shape, F32)
    acc_ref[...] = jnp.zeros(acc_ref.shape, F32)


def _record_scores(s, j, r0, s_ref, max_ref):
    n = s.shape[0]
    s_ref[j, r0:r0 + n, :] = s
    max_ref[r0:r0 + n, :] = jnp.maximum(max_ref[r0:r0 + n, :], _lane_fold(s, jnp.maximum))


def _finish_max(max_ref):
    m = max_ref[...]
    max_ref[...] = jnp.broadcast_to(jnp.max(m, axis=-1, keepdims=True), m.shape)


def _accumulate_probs(j, r0, n, v, s_ref, max_ref, sum_ref, acc_ref):
    s = s_ref[j, r0:r0 + n, :]
    p = jnp.exp2(s - _lane_tile(max_ref[r0:r0 + n, :], s.shape[1]))
    sum_ref[r0:r0 + n, :] += _lane_fold(p, jnp.add)
    acc_ref[r0:r0 + n, :] += _dot(p.astype(BF16), v)


def _normalised(r0, n, sum_ref, acc_ref):
    return acc_ref[r0:r0 + n, :] / jnp.sum(sum_ref[r0:r0 + n, :], axis=-1, keepdims=True)


def _head_lane_masks(rows, width):
    lane = lax.broadcasted_iota(jnp.int32, (rows, width), 1)
    return [(lane >= h * HEAD_DIM) & (lane < (h + 1) * HEAD_DIM) for h in range(width // HEAD_DIM)]


def _stack_masked(q, masks):
    zero = jnp.zeros_like(q)
    return jnp.concatenate([jnp.where(m, q, zero) for m in masks], axis=0)


def _local_causal(t):
    row = lax.broadcasted_iota(jnp.int32, (t, t), 0)
    col = lax.broadcasted_iota(jnp.int32, (t, t), 1)
    return col <= row


def _tile_rows(j, t):
    return pl.ds(pl.multiple_of(j * t, t), t)


def _for_each_tile(n, body, group=2):
    @pl.loop(0, n // group)
    def _(i):
        for k in range(group):
            body(group * i + k)

    part = group // 2
    while part >= 1:
        @pl.when(n % (2 * part) >= part)
        def _():
            for k in range(part):
                body(n - n % (2 * part) + k)
        part //= 2


def _interleave_pairs(o, t):
    lane = lax.broadcasted_iota(jnp.int32, (t, LANES), 1)
    low = lane < HEAD_DIM
    return jnp.concatenate([jnp.where(low, o[0], o[1]), jnp.where(low, o[2], o[3])], axis=1)


_ATT_SCRATCH = lambda n_tiles, rows, t: [
    pltpu.VMEM((n_tiles, rows, t), F32),
    pltpu.VMEM((rows, LANES), F32),
    pltpu.VMEM((rows, LANES), F32),
    pltpu.VMEM((rows, LANES), F32),
]


def _diff_kernel(lam_ref, g_ref, q_ref, k_ref, v_ref, o_ref, s_ref, max_ref, sum_ref, acc_ref, *, lam_init, t):
    it = pl.program_id(1)
    lane = lax.broadcasted_iota(jnp.int32, (t, C_VDIM), 1)
    maps = [lane < HEAD_DIM, lane >= HEAD_DIM]
    q_all = q_ref[0]
    qs = [_stack_masked(q_all[:, h * C_VDIM:(h + 1) * C_VDIM], maps) for h in range(C_HEADS)]
    _reset_stats(max_ref, sum_ref, acc_ref)
    causal = jnp.where(_local_causal(t), 0.0, MASKED)
    causal2 = jnp.concatenate([causal, causal], axis=0)

    def scores(j):
        rows = _tile_rows(j, t)
        bias = jnp.where(j == it, causal2, 0.0)
        for h in range(C_HEADS):
            s = _nt_dot(qs[h], k_ref[0, rows, h * C_VDIM:(h + 1) * C_VDIM]) + bias
            _record_scores(s, j, h * 2 * t, s_ref, max_ref)

    _for_each_tile(it + 1, scores, group=4)
    _finish_max(max_ref)

    def probs(j):
        rows = _tile_rows(j, t)
        for h in range(C_HEADS):
            _accumulate_probs(j, h * 2 * t, 2 * t, v_ref[0, rows, h * C_VDIM:(h + 1) * C_VDIM],
                              s_ref, max_ref, sum_ref, acc_ref)

    _for_each_tile(it + 1, probs, group=4)

    lp = lam_ref[...]
    lam = (jnp.exp(jnp.sum(lp[0:1] * lp[1:2], axis=-1, keepdims=True))
           - jnp.exp(jnp.sum(lp[2:3] * lp[3:4], axis=-1, keepdims=True)) + lam_init)
    for h in range(C_HEADS):
        o = _normalised(h * 2 * t, t, sum_ref, acc_ref) - lam * _normalised(h * 2 * t + t, t, sum_ref, acc_ref)
        o = o * lax.rsqrt(jnp.mean(o * o, axis=-1, keepdims=True) + EPS) * g_ref[...] * (1.0 - lam_init)
        o_ref[0, :, h * C_VDIM:(h + 1) * C_VDIM] = o.astype(o_ref.dtype)


def _diff_attention(qc, kc, vc, lam_p, subln_g, lam_init):
    b, s, w = qc.shape
    t = ATT_TILE
    qspec = pl.BlockSpec((1, t, w), lambda bi, i: (bi, i, 0))
    kvspec = pl.BlockSpec((1, s, w), lambda bi, i: (bi, 0, 0))
    return pl.pallas_call(
        functools.partial(_diff_kernel, lam_init=lam_init, t=t),
        grid=(b, s // t),
        in_specs=[pl.BlockSpec(lam_p.shape, lambda bi, i: (0, 0)),
                  pl.BlockSpec((1, C_VDIM), lambda bi, i: (0, 0)), qspec, kvspec, kvspec],
        out_specs=qspec,
        out_shape=jax.ShapeDtypeStruct((b, s, w), BF16),
        scratch_shapes=_ATT_SCRATCH(s // t, 2 * C_HEADS * t, t),
        compiler_params=_params("parallel", "arbitrary"),
        name="diff_attention",
    )(lam_p, subln_g, qc, kc, vc)


def _moba_kernel(q_ref, k_ref, v_ref, o_ref, kmean_hi, kmean_lo, bias_ref, s_ref, max_ref, sum_ref, acc_ref, *, t):
    it = pl.program_id(1)
    n_blk = k_ref.shape[1] // t
    width = q_ref.shape[2]

    @pl.when(it == 0)
    def _():
        kmean_hi[...] = jnp.zeros_like(kmean_hi)
        kmean_lo[...] = jnp.zeros_like(kmean_lo)
        for n in range(n_blk):
            km = jnp.mean(k_ref[0, n * t:(n + 1) * t, :].astype(F32), axis=0, keepdims=True)
            hi = km.astype(BF16)
            kmean_hi[n:n + 1, :] = hi
            kmean_lo[n:n + 1, :] = (km - hi.astype(F32)).astype(BF16)

    heads = _head_lane_masks(t, width)
    q_stack = _stack_masked(q_ref[0], heads)

    no_bias = jnp.zeros(bias_ref.shape[1:], F32)
    bias_ref[it] = no_bias

    @pl.when(it <= MOBA_TOPK)
    def _():
        for n in range(min(MOBA_TOPK, n_blk)):
            @pl.when(n < it)
            def _():
                bias_ref[n] = no_bias

    @pl.when(it > MOBA_TOPK)
    def _():
        blk = lax.broadcasted_iota(jnp.int32, (SUBLANES, t), 0)
        past = blk < it
        pad = jnp.full((LANES - SUBLANES, t), MASKED, F32)
        bias_q = []
        for h in range(B_HEADS):
            qh = q_stack[h * t:(h + 1) * t]
            gate = (_nt_dot(kmean_hi[...], qh) + _nt_dot(kmean_lo[...], qh))[:SUBLANES]
            gate = jnp.where(past, gate, -jnp.inf)
            rank = jnp.zeros((SUBLANES, t), jnp.int32)
            for n in range(n_blk):
                gn = gate[n:n + 1, :]
                rank = rank + jnp.where((gn > gate) | ((gn == gate) & (blk > n)), 1, 0)
            bias = jnp.where((rank < MOBA_TOPK) & past, 0.0, MASKED)
            bias_q.append(jnp.concatenate([bias, pad], axis=0).T.astype(BF16))
        bias_q = jnp.concatenate(bias_q, axis=0)
        lane_of = lax.broadcasted_iota(jnp.int32, (LANES, LANES), 0)
        for n in range(n_blk - 1):
            @pl.when(n < it)
            def _():
                pick = (lane_of == n).astype(BF16)
                bias_ref[n] = _dot(bias_q, pick)

    _reset_stats(max_ref, sum_ref, acc_ref)
    causal = jnp.where(_local_causal(t), 0.0, MASKED)
    causal4 = jnp.concatenate([causal] * B_HEADS, axis=0)

    def scores(j):
        s = _nt_dot(q_stack, k_ref[0, _tile_rows(j, t), :]) + _lane_tile(bias_ref[j], t)
        _record_scores(s + jnp.where(j == it, causal4, 0.0), j, 0, s_ref, max_ref)

    _for_each_tile(it + 1, scores, group=4)
    _finish_max(max_ref)

    def probs(j):
        rows = _tile_rows(j, t)
        for pair in range(B_HEADS // 2):
            _accumulate_probs(j, pair * 2 * t, 2 * t, v_ref[0, rows, pair * LANES:(pair + 1) * LANES],
                              s_ref, max_ref, sum_ref, acc_ref)

    _for_each_tile(it + 1, probs, group=4)

    o = [_normalised(h * t, t, sum_ref, acc_ref) for h in range(B_HEADS)]
    o_ref[0] = _interleave_pairs(o, t).astype(o_ref.dtype)


def _moba_attention(qb, kb, vb):
    b, s, w = qb.shape
    t = MOBA_BLOCK
    assert s // t <= SUBLANES
    qspec = pl.BlockSpec((1, t, w), lambda bi, i: (bi, i, 0))
    kvspec = pl.BlockSpec((1, s, w), lambda bi, i: (bi, 0, 0))
    return pl.pallas_call(
        functools.partial(_moba_kernel, t=t),
        grid=(b, s // t),
        in_specs=[qspec, kvspec, kvspec],
        out_specs=qspec,
        out_shape=jax.ShapeDtypeStruct((b, s, w), BF16),
        scratch_shapes=[pltpu.VMEM((LANES, w), BF16), pltpu.VMEM((LANES, w), BF16),
                        pltpu.VMEM((s // t, B_HEADS * t, LANES), F32)] + _ATT_SCRATCH(s // t, B_HEADS * t, t),
        compiler_params=_params("parallel", "arbitrary"),
        name="moba_attention",
    )(qb, kb, vb)


def _sublane_fold(x, rows):
    parts = [x[r:r + rows] for r in range(0, x.shape[0], rows)]
    while len(parts) > 1:
        parts = [parts[i] + parts[i + 1] for i in range(0, len(parts), 2)]
    return parts[0]


I16 = jnp.int16
I16_ROWS = 2 * SUBLANES
I16_MIN = -(2 ** 15)


def _dsa_kernel(q_ref, k_ref, v_ref, qi_ref, ki_ref, wi_ref, o_ref,
                keys_ref, hi_ref, lo_ref, ties_ref, s_ref, max_ref, sum_ref, acc_ref, *, t, topk):
    it = pl.program_id(1)
    width = q_ref.shape[2]
    heads = _head_lane_masks(t, width)
    causal = _local_causal(t)

    qidx = qi_ref[0]
    qi_stack = jnp.concatenate(
        [_stack_masked(qidx[:, g * width:(g + 1) * width], heads) for g in range(IDX_HEADS // 4)], axis=0)
    wi = wi_ref[0]
    w_rep = [jnp.broadcast_to(wi[:, h:h + 1], (t, LANES)) for h in range(IDX_HEADS)]

    def score_tile(j):
        logit = _nt_dot(qi_stack, ki_ref[0, _tile_rows(j, t), :])
        cols = []
        for c in range(0, t, LANES):
            acc = jnp.zeros((t, LANES), F32)
            for h in range(IDX_HEADS):
                acc = acc + jnp.maximum(logit[h * t:(h + 1) * t, c:c + LANES], 0.0) * w_rep[h]
            cols.append(acc)
        isc = jnp.concatenate(cols, axis=1) + 0.0
        isc = jnp.where(j == it, jnp.where(causal, isc, -jnp.inf), isc)
        bits = lax.bitcast_convert_type(isc, jnp.int32)
        key = bits ^ ((bits >> 31) & 0x7FFFFFFF)
        keys_ref[j] = key
        key_t = key.T
        hi_ref[j] = (key_t >> 16).astype(I16)
        lo_ref[j] = ((key_t & 0xFFFF) + I16_MIN).astype(I16)

    _for_each_tile(it + 1, score_tile, group=4)
    n_pairs = it // 2 + 1

    @pl.when(it % 2 == 0)
    def _():
        hi_ref[it + 1] = jnp.full((t, t), I16_MIN, I16)
        lo_ref[it + 1] = jnp.full((t, t), I16_MIN, I16)

    def count(ref, pred):
        def one(j, cnt):
            return cnt + _sublane_fold(jnp.where(pred(ref[j]), jnp.int16(1), jnp.int16(0)), I16_ROWS)
        cnt = lax.fori_loop(0, n_pairs, lambda i, cnt: one(2 * i + 1, one(2 * i, cnt)), jnp.zeros((I16_ROWS, t), I16))
        return jnp.sum(cnt.astype(jnp.int32), axis=0, keepdims=True)

    def kth_largest(ref, k):
        thr = jnp.where(count(ref, lambda x: x >= jnp.int16(0)) >= k, 0, jnp.full((1, t), I16_MIN, jnp.int32))

        def bit_step(i, thr):
            cand = thr | (1 << (14 - i))
            return jnp.where(count(ref, lambda x: x >= cand.astype(I16)) >= k, cand, thr)

        return lax.fori_loop(0, 15, bit_step, thr)

    thr_hi = kth_largest(hi_ref, topk)
    thr_hi16 = thr_hi.astype(I16)
    k_lo = topk - count(hi_ref, lambda x: x > thr_hi16)

    def keep_matching(j):
        lo_ref[j] = jnp.where(hi_ref[j] == thr_hi16, lo_ref[j], jnp.int16(I16_MIN))

    @pl.loop(0, n_pairs)
    def _(i):
        keep_matching(2 * i)
        keep_matching(2 * i + 1)

    thr_lo = kth_largest(lo_ref, k_lo)
    thr_lo16 = thr_lo.astype(I16)
    wanted = (k_lo - count(lo_ref, lambda x: x > thr_lo16)).astype(F32)
    thr = (thr_hi << 16) | (thr_lo - I16_MIN)
    thr_q = _lane_tile(jnp.broadcast_to(thr, (LANES, t)).T, t)
    wanted_q = _lane_tile(jnp.broadcast_to(wanted, (LANES, t)).T, t)

    q_stack = _stack_masked(q_ref[0], heads)
    r = lax.broadcasted_iota(jnp.int32, (t, t), 0)
    c = lax.broadcasted_iota(jnp.int32, (t, t), 1)
    before = (r < c).astype(BF16)
    ones = jnp.ones((t, LANES), BF16)
    _reset_stats(max_ref, sum_ref, acc_ref)

    ties_ref[...] = jnp.zeros(ties_ref.shape, F32)

    not_causal = jnp.where(causal, 0.0, MASKED)

    def select_tile(j):
        key = keys_ref[j]
        tie = key == thr_q
        tie_b = jnp.where(tie, 1.0, 0.0).astype(BF16)
        tie_rank = _lane_tile(ties_ref[...], t) + _dot(tie_b, before)
        bias = jnp.where(key > thr_q, 0.0, jnp.where(tie, jnp.where(tie_rank < wanted_q, 0.0, MASKED), MASKED))
        bias = bias + jnp.where(j == it, not_causal, 0.0)
        s = _nt_dot(q_stack, k_ref[0, _tile_rows(j, t), :]) + jnp.concatenate([bias] * A_HEADS, axis=0)
        _record_scores(s, j, 0, s_ref, max_ref)
        ties_ref[...] += _dot(tie_b, ones)

    _for_each_tile(it + 1, select_tile)
    _finish_max(max_ref)

    _for_each_tile(it + 1, lambda j: _accumulate_probs(
        j, 0, A_HEADS * t, v_ref[0, _tile_rows(j, t), :], s_ref, max_ref, sum_ref, acc_ref), group=4)

    o = [_normalised(h * t, t, sum_ref, acc_ref) for h in range(A_HEADS)]
    o_ref[0] = _interleave_pairs(o, t).astype(o_ref.dtype)


def _dsa_attention(qa, ka, va, qi, ki, wi):
    b, s, w = qa.shape
    t = ATT_TILE
    topk = min(INDEX_TOPK, s // 4)
    assert t >= topk
    row = lambda width: pl.BlockSpec((1, t, width), lambda bi, i: (bi, i, 0))
    full = lambda width: pl.BlockSpec((1, s, width), lambda bi, i: (bi, 0, 0))
    return pl.pallas_call(
        functools.partial(_dsa_kernel, t=t, topk=topk),
        grid=(b, s // t),
        in_specs=[row(w), full(w), full(va.shape[2]), row(qi.shape[2]), full(w), row(LANES)],
        out_specs=row(w),
        out_shape=jax.ShapeDtypeStruct((b, s, w), BF16),
        scratch_shapes=[pltpu.VMEM((s // t, t, t), jnp.int32),
                        pltpu.VMEM((s // t + s // t % 2, t, t), I16), pltpu.VMEM((s // t + s // t % 2, t, t), I16),
                        pltpu.VMEM((t, LANES), F32)]
                       + _ATT_SCRATCH(s // t, A_HEADS * t, t),
        compiler_params=_params("parallel", "arbitrary"),
        name="dsa_attention",
    )(qa, ka, va, qi, ki, wi)


def _merge_kernel(x_ref, oa_ref, ob_ref, oc_ref, gate_ref, wb_ref, wo_ref, o_ref):
    d = x_ref.shape[1]
    wa, wbw = oa_ref.shape[1], ob_ref.shape[1]
    merged = gate_ref[:, 0:d].astype(F32) * _dot(oa_ref[...], wb_ref[0:wa, :])
    merged = merged + gate_ref[:, d:2 * d].astype(F32) * _dot(ob_ref[...], wb_ref[wa:wa + wbw, :])
    merged = merged + gate_ref[:, 2 * d:3 * d].astype(F32) * _dot(oc_ref[...], wb_ref[wa + wbw:, :])
    o_ref[...] = x_ref[...] + _dot(merged.astype(BF16), wo_ref[...])


def _merge(x, oa, ob, oc, gate, wb, wo, tm):
    n, d = x.shape
    row = lambda w: pl.BlockSpec((tm, w), lambda i: (i, 0))
    return pl.pallas_call(
        _merge_kernel,
        grid=(n // tm,),
        in_specs=[row(d), row(oa.shape[1]), row(ob.shape[1]), row(oc.shape[1]), row(gate.shape[1]),
                  _resident(wb.shape), _resident(wo.shape)],
        out_specs=row(d),
        out_shape=jax.ShapeDtypeStruct((n, d), F32),
        compiler_params=_params("parallel"),
        name="merge_out",
    )(x, oa, ob, oc, gate, wb, wo)


def kernel(x, positions, norm_g, w_in, qk_norm_g, lambda_params, diff_subln_g, w_branch, w_out,
           ffn_w_gate, ffn_w_up, ffn_w_down):
    b, s, d = x.shape
    n = b * s
    depth = norm_g.shape[0]
    tm = min(512, n)
    assert n % tm == 0 and s % ATT_TILE == 0 and ATT_TILE == MOBA_BLOCK

    cos, sin = _rope_tables(positions, min(4 * tm, n))
    xf = x.reshape(n, d)
    for layer in range(depth):
        lam_init = 0.8 - 0.6 * math.exp(-0.3 * layer)
        ffn_w = lambda i: (ffn_w_gate[layer, i].astype(BF16), ffn_w_up[layer, i].astype(BF16),
                           ffn_w_down[layer, i].astype(BF16))
        xf = _ffn(xf, norm_g[layer, 0][None, :], *ffn_w(0), tm)

        qkg = jnp.tile(qk_norm_g[layer], (1, LANES // HEAD_DIM))
        outs = _proj(xf, norm_g[layer, 1][None, :], qkg, cos, sin, _proj_weights(w_in[layer]), tm)
        p = {name: o.reshape(b, s, o.shape[1]) for (name, *_), o in zip(_PROJ_OUTPUTS, outs)}
        o_a = _dsa_attention(p["qa"], p["ka"], p["va"], p["qi"], p["ki"], p["wi"])
        o_b = _moba_attention(p["qb"], p["kb"], p["vb"])
        o_c = _diff_attention(p["qc"], p["kc"], p["vc"], lambda_params[layer], diff_subln_g[layer][None, :], lam_init)
        xf = _merge(xf, o_a.reshape(n, -1), o_b.reshape(n, -1), o_c.reshape(n, -1), p["gate"].reshape(n, -1),
                    w_branch[layer].astype(BF16), w_out[layer].astype(BF16), tm)

        xf = _ffn(xf, norm_g[layer, 2][None, :], *ffn_w(1), tm)
    return xf.reshape(b, s, d)
```

```python
import functools
import math

import jax
import jax.numpy as jnp
from jax import lax
from jax.experimental import pallas as pl
from jax.experimental.pallas import tpu as pltpu

F32 = jnp.float32
BF16 = jnp.bfloat16

HEAD_DIM = 64
ROT_DIM = HEAD_DIM // 4
ROT_HALF = ROT_DIM // 2
ROPE_THETA = 500000.0
EPS = 1e-6
A_HEADS = 4
IDX_HEADS = 8
IDX_DIM = 64
INDEX_TOPK = 256
B_HEADS = 4
MOBA_BLOCK = 256
MOBA_TOPK = 3
C_HEADS = 4
C_VDIM = 2 * HEAD_DIM

LANES = 128
SUBLANES = 8
MXU_DIM = 256
ATT_TILE = 256
MASKED = -1e30
VMEM_LIMIT = 56 * 1024 * 1024
LOG2E = 1.4426950408889634


def _nt_dot(a, b):
    return lax.dot_general(a, b, (((1,), (1,)), ((), ())), preferred_element_type=F32)


def _dot(a, b):
    return jnp.dot(a, b, preferred_element_type=F32)


def _resident(shape):
    nd = len(shape)
    return pl.BlockSpec(shape, lambda *_: (0,) * nd, pipeline_mode=pl.Buffered(1))


def _params(*sem):
    return pltpu.CompilerParams(dimension_semantics=sem, vmem_limit_bytes=VMEM_LIMIT)


def _rope_table_kernel(pos_ref, freq_ref, cos_ref, sin_ref):
    ang = pos_ref[...].astype(F32) * freq_ref[...]
    lane = lax.broadcasted_iota(jnp.int32, ang.shape, 1) % HEAD_DIM
    c = jnp.cos(ang)
    s = jnp.sin(ang)
    cos_ref[...] = jnp.where(lane < ROT_DIM, c, 1.0)
    sin_ref[...] = jnp.where(lane < ROT_HALF, -s, jnp.where(lane < ROT_DIM, s, 0.0))


def _rope_tables(positions, tm):
    n = positions.size
    pos = positions.reshape(n, 1)
    inv_freq = jnp.power(ROPE_THETA, -jnp.arange(0, ROT_DIM, 2, dtype=F32) / ROT_DIM)
    freq = jnp.tile(jnp.concatenate([inv_freq, inv_freq, jnp.zeros((HEAD_DIM - ROT_DIM,), F32)]), 2)[None, :]
    return pl.pallas_call(
        _rope_table_kernel,
        grid=(n // tm,),
        in_specs=[pl.BlockSpec((tm, 1), lambda i: (i, 0)), pl.BlockSpec((1, LANES), lambda i: (0, 0))],
        out_specs=[pl.BlockSpec((tm, LANES), lambda i: (i, 0))] * 2,
        out_shape=[jax.ShapeDtypeStruct((n, LANES), F32)] * 2,
        compiler_params=_params("parallel"),
        name="rope_tables",
    )(pos, freq)


def _rmsnorm_rows(x, g):
    return x * lax.rsqrt(jnp.mean(x * x, axis=-1, keepdims=True) + EPS) * g


def _ffn_kernel(x_ref, g_ref, wg_ref, wu_ref, wd_ref, o_ref, *, chunk):
    x = x_ref[...]
    h = _rmsnorm_rows(x, g_ref[...]).astype(BF16)
    acc = jnp.zeros(x.shape, F32)
    for c in range(0, wg_ref.shape[1], chunk):
        a = _dot(h, wg_ref[:, c:c + chunk])
        b = _dot(h, wu_ref[:, c:c + chunk])
        t = (a * jax.nn.sigmoid(a) * b).astype(BF16)
        acc = acc + _dot(t, wd_ref[c:c + chunk, :])
    o_ref[...] = x + 0.5 * acc


def _ffn(x, g, wg, wu, wd, tm):
    n, d = x.shape
    f = wg.shape[1]
    return pl.pallas_call(
        functools.partial(_ffn_kernel, chunk=MXU_DIM),
        grid=(n // tm,),
        in_specs=[pl.BlockSpec((tm, d), lambda i: (i, 0)), _resident((1, d)),
                  _resident((d, f)), _resident((d, f)), _resident((f, d))],
        out_specs=pl.BlockSpec((tm, d), lambda i: (i, 0)),
        out_shape=jax.ShapeDtypeStruct((n, d), F32),
        compiler_params=_params("parallel"),
        name="ffn",
    )(x, g, wg, wu, wd)


_PROJ_COLUMNS = (
    ("qa", 256, "norm_rope_scale", 0),
    ("shared", 256, "shared", 1),
    ("qi", 512, "rope", None),
    ("wi", 128, "index_weight", None),
    ("qb", 256, "norm_rope_scale", 2),
    ("kb", 256, "norm_rope", 3),
    ("vb", 256, "plain", None),
    ("qc", 512, "norm_rope_scale", 4),
    ("kc", 512, "norm_rope", 5),
    ("vc", 512, "plain", None),
    ("gate", 3072, "sigmoid", None),
)
_PROJ_OUTPUTS = (
    ("qa", 256, BF16), ("ka", 256, BF16), ("va", 128, BF16), ("qi", 512, BF16), ("ki", 256, BF16), ("wi", 128, F32),
    ("qb", 256, BF16), ("kb", 256, BF16), ("vb", 256, BF16), ("qc", 512, BF16), ("kc", 512, BF16), ("vc", 512, BF16),
    ("gate", 3072, BF16),
)
_Q_SCALE = HEAD_DIM ** -0.5 * LOG2E


def _proj_kernel(x_ref, g_ref, qkg_ref, cos_ref, sin_ref, w_ref, *o_refs):
    out = {name: ref for (name, _, _), ref in zip(_PROJ_OUTPUTS, o_refs)}
    tm = x_ref.shape[0]
    h = _rmsnorm_rows(x_ref[...], g_ref[...]).astype(BF16)
    cos = cos_ref[...]
    sin = sin_ref[...]
    lane = lax.broadcasted_iota(jnp.int32, (tm, LANES), 1)
    take_upper = (lane % HEAD_DIM) < ROT_HALF
    low_half = lane < HEAD_DIM
    r = lax.broadcasted_iota(jnp.int32, (LANES, LANES), 0) // HEAD_DIM
    c = lax.broadcasted_iota(jnp.int32, (LANES, LANES), 1) // HEAD_DIM
    same_head = (r == c).astype(BF16)

    def epilogue(y, kind, g_idx):
        if kind in ("norm_rope_scale", "norm_rope"):
            ssq = _dot((y * y).astype(BF16), same_head)
            y = y * lax.rsqrt(ssq * (1.0 / HEAD_DIM) + EPS) * qkg_ref[g_idx:g_idx + 1, :]
        if kind in ("norm_rope_scale", "norm_rope", "rope"):
            partner = jnp.where(take_upper, pltpu.roll(y, LANES - ROT_HALF, 1), pltpu.roll(y, ROT_HALF, 1))
            y = y * cos + partner * sin
        if kind == "norm_rope_scale":
            y = y * _Q_SCALE
        if kind == "index_weight":
            y = y * ((IDX_HEADS * IDX_DIM) ** -0.5)
        if kind == "sigmoid":
            y = jax.nn.sigmoid(y)
        return y

    def on_both_halves(y, half):
        swapped = pltpu.roll(y, HEAD_DIM, 1)
        return jnp.where(low_half, y, swapped) if half == 0 else jnp.where(low_half, swapped, y)

    col = 0
    for name, width, kind, g_idx in _PROJ_COLUMNS:
        step = min(width, MXU_DIM)
        for c0 in range(0, width, step):
            y = _dot(h, w_ref[:, col + c0:col + c0 + step])
            if kind == "shared":
                key_value, index_key = y[:, :LANES], y[:, LANES:]
                ka = on_both_halves(epilogue(key_value, "norm_rope", g_idx), 0).astype(BF16)
                ki = on_both_halves(epilogue(index_key, "rope", None), 0).astype(BF16)
                for s0 in range(0, out["ka"].shape[1], LANES):
                    out["ka"][:, s0:s0 + LANES] = ka
                    out["ki"][:, s0:s0 + LANES] = ki
                out["va"][...] = on_both_halves(key_value, 1).astype(BF16)
                continue
            o_ref = out[name]
            for s0 in range(0, step, LANES):
                o_ref[:, c0 + s0:c0 + s0 + LANES] = epilogue(y[:, s0:s0 + LANES], kind, g_idx).astype(o_ref.dtype)
        col += width


def _proj_weights(w_in):
    splits = (256, 64, 64, 512, 64, 8, 256, 256, 256, 512, 512, 512, 1024, 1024, 1024)
    offs = [0]
    for s in splits:
        offs.append(offs[-1] + s)
    qa, ka, va, qi, ki, wi, qb, kb, vb, qc, kc, vc, ga, gb, gc = [w_in[:, offs[i]:offs[i + 1]] for i in range(len(splits))]
    zeros = lambda n: jnp.zeros((w_in.shape[0], n), w_in.dtype)
    groups = dict(qa=qa, shared=jnp.concatenate([ka, va, ki, zeros(HEAD_DIM)], axis=1), qi=qi,
                  wi=jnp.concatenate([wi, zeros(LANES - IDX_HEADS)], axis=1),
                  qb=qb, kb=kb, vb=vb, qc=qc, kc=kc, vc=vc, gate=jnp.concatenate([ga, gb, gc], axis=1))
    return jnp.concatenate([groups[name] for name, *_ in _PROJ_COLUMNS], axis=1).astype(BF16)


def _proj(x, g, qkg, cos, sin, w, tm):
    n, d = x.shape
    row = lambda width: pl.BlockSpec((tm, width), lambda i: (i, 0))
    return pl.pallas_call(
        _proj_kernel,
        grid=(n // tm,),
        in_specs=[row(d), _resident((1, d)), _resident(qkg.shape), row(LANES), row(LANES), _resident(w.shape)],
        out_specs=[row(width) for _, width, _ in _PROJ_OUTPUTS],
        out_shape=[jax.ShapeDtypeStruct((n, width), dtype) for _, width, dtype in _PROJ_OUTPUTS],
        compiler_params=_params("parallel"),
        name="in_proj",
    )(x, g, qkg, cos, sin, w)


def _lane_fold(x, op):
    out = x[:, :LANES]
    for c in range(LANES, x.shape[1], LANES):
        out = op(out, x[:, c:c + LANES])
    return out


def _lane_tile(x, width):
    return jnp.concatenate([x] * (width // LANES), axis=1)


def _reset_stats(max_ref, sum_ref, acc_ref):
    max_ref[...] = jnp.full(max_ref.shape, MASKED, F32)
    sum_ref[...] = jnp.zeros(sum_ref.shape, F32)
    acc_ref[...] = jnp.zeros(acc_ref.shape, F32)


def _record_scores(s, j, r0, s_ref, max_ref):
    n = s.shape[0]
    s_ref[j, r0:r0 + n, :] = s
    max_ref[r0:r0 + n, :] = jnp.maximum(max_ref[r0:r0 + n, :], _lane_fold(s, jnp.maximum))


def _finish_max(max_ref):
    m = max_ref[...]
    max_ref[...] = jnp.broadcast_to(jnp.max(m, axis=-1, keepdims=True), m.shape)


def _accumulate_probs(j, r0, n, v, s_ref, max_ref, sum_ref, acc_ref):
    s = s_ref[j, r0:r0 + n, :]
    p = jnp.exp2(s - _lane_tile(max_ref[r0:r0 + n, :], s.shape[1]))
    sum_ref[r0:r0 + n, :] += _lane_fold(p, jnp.add)
    acc_ref[r0:r0 + n, :] += _dot(p.astype(BF16), v)


def _normalised(r0, n, sum_ref, acc_ref):
    return acc_ref[r0:r0 + n, :] / jnp.sum(sum_ref[r0:r0 + n, :], axis=-1, keepdims=True)


def _head_lane_masks(rows, width):
    lane = lax.broadcasted_iota(jnp.int32, (rows, width), 1)
    return [(lane >= h * HEAD_DIM) & (lane < (h + 1) * HEAD_DIM) for h in range(width // HEAD_DIM)]


def _stack_masked(q, masks):
    zero = jnp.zeros_like(q)
    return jnp.concatenate([jnp.where(m, q, zero) for m in masks], axis=0)


def _local_causal(t):
    row = lax.broadcasted_iota(jnp.int32, (t, t), 0)
    col = lax.broadcasted_iota(jnp.int32, (t, t), 1)
    return col <= row


def _tile_rows(j, t):
    return pl.ds(pl.multiple_of(j * t, t), t)


def _for_each_tile(n, body, group=2):
    @pl.loop(0, n // group)
    def _(i):
        for k in range(group):
            body(group * i + k)

    part = group // 2
    while part >= 1:
        @pl.when(n % (2 * part) >= part)
        def _():
            for k in range(part):
                body(n - n % (2 * part) + k)
        part //= 2


def _interleave_pairs(o, t):
    lane = lax.broadcasted_iota(jnp.int32, (t, LANES), 1)
    low = lane < HEAD_DIM
    return jnp.concatenate([jnp.where(low, o[0], o[1]), jnp.where(low, o[2], o[3])], axis=1)


_ATT_SCRATCH = lambda n_tiles, rows, t: [
    pltpu.VMEM((n_tiles, rows, t), F32),
    pltpu.VMEM((rows, LANES), F32),
    pltpu.VMEM((rows, LANES), F32),
    pltpu.VMEM((rows, LANES), F32),
]


def _diff_kernel(lam_ref, g_ref, q_ref, k_ref, v_ref, o_ref, s_ref, max_ref, sum_ref, acc_ref, *, lam_init, t):
    it = pl.program_id(1)
    lane = lax.broadcasted_iota(jnp.int32, (t, C_VDIM), 1)
    maps = [lane < HEAD_DIM, lane >= HEAD_DIM]
    q_all = q_ref[0]
    qs = [_stack_masked(q_all[:, h * C_VDIM:(h + 1) * C_VDIM], maps) for h in range(C_HEADS)]
    _reset_stats(max_ref, sum_ref, acc_ref)
    causal = jnp.where(_local_causal(t), 0.0, MASKED)
    causal2 = jnp.concatenate([causal, causal], axis=0)

    def scores(j):
        rows = _tile_rows(j, t)
        bias = jnp.where(j == it, causal2, 0.0)
        for h in range(C_HEADS):
            s = _nt_dot(qs[h], k_ref[0, rows, h * C_VDIM:(h + 1) * C_VDIM]) + bias
            _record_scores(s, j, h * 2 * t, s_ref, max_ref)

    _for_each_tile(it + 1, scores, group=4)
    _finish_max(max_ref)

    def probs(j):
        rows = _tile_rows(j, t)
        for h in range(C_HEADS):
            _accumulate_probs(j, h * 2 * t, 2 * t, v_ref[0, rows, h * C_VDIM:(h + 1) * C_VDIM],
                              s_ref, max_ref, sum_ref, acc_ref)

    _for_each_tile(it + 1, probs, group=4)

    lp = lam_ref[...]
    lam = (jnp.exp(jnp.sum(lp[0:1] * lp[1:2], axis=-1, keepdims=True))
           - jnp.exp(jnp.sum(lp[2:3] * lp[3:4], axis=-1, keepdims=True)) + lam_init)
    for h in range(C_HEADS):
        o = _normalised(h * 2 * t, t, sum_ref, acc_ref) - lam * _normalised(h * 2 * t + t, t, sum_ref, acc_ref)
        o = o * lax.rsqrt(jnp.mean(o * o, axis=-1, keepdims=True) + EPS) * g_ref[...] * (1.0 - lam_init)
        o_ref[0, :, h * C_VDIM:(h + 1) * C_VDIM] = o.astype(o_ref.dtype)


def _diff_attention(qc, kc, vc, lam_p, subln_g, lam_init):
    b, s, w = qc.shape
    t = ATT_TILE
    qspec = pl.BlockSpec((1, t, w), lambda bi, i: (bi, i, 0))
    kvspec = pl.BlockSpec((1, s, w), lambda bi, i: (bi, 0, 0))
    return pl.pallas_call(
        functools.partial(_diff_kernel, lam_init=lam_init, t=t),
        grid=(b, s // t),
        in_specs=[pl.BlockSpec(lam_p.shape, lambda bi, i: (0, 0)),
                  pl.BlockSpec((1, C_VDIM), lambda bi, i: (0, 0)), qspec, kvspec, kvspec],
        out_specs=qspec,
        out_shape=jax.ShapeDtypeStruct((b, s, w), BF16),
        scratch_shapes=_ATT_SCRATCH(s // t, 2 * C_HEADS * t, t),
        compiler_params=_params("parallel", "arbitrary"),
        name="diff_attention",
    )(lam_p, subln_g, qc, kc, vc)


def _moba_kernel(q_ref, k_ref, v_ref, o_ref, kmean_hi, kmean_lo, bias_ref, s_ref, max_ref, sum_ref, acc_ref, *, t):
    it = pl.program_id(1)
    n_blk = k_ref.shape[1] // t
    width = q_ref.shape[2]

    @pl.when(it == 0)
    def _():
        kmean_hi[...] = jnp.zeros_like(kmean_hi)
        kmean_lo[...] = jnp.zeros_like(kmean_lo)
        for n in range(n_blk):
            km = jnp.mean(k_ref[0, n * t:(n + 1) * t, :].astype(F32), axis=0, keepdims=True)
            hi = km.astype(BF16)
            kmean_hi[n:n + 1, :] = hi
            kmean_lo[n:n + 1, :] = (km - hi.astype(F32)).astype(BF16)

    heads = _head_lane_masks(t, width)
    q_stack = _stack_masked(q_ref[0], heads)

    no_bias = jnp.zeros(bias_ref.shape[1:], F32)
    bias_ref[it] = no_bias

    @pl.when(it <= MOBA_TOPK)
    def _():
        for n in range(min(MOBA_TOPK, n_blk)):
            @pl.when(n < it)
            def _():
                bias_ref[n] = no_bias

    @pl.when(it > MOBA_TOPK)
    def _():
        blk = lax.broadcasted_iota(jnp.int32, (SUBLANES, t), 0)
        past = blk < it
        pad = jnp.full((LANES - SUBLANES, t), MASKED, F32)
        bias_q = []
        for h in range(B_HEADS):
            qh = q_stack[h * t:(h + 1) * t]
            gate = (_nt_dot(kmean_hi[...], qh) + _nt_dot(kmean_lo[...], qh))[:SUBLANES]
            gate = jnp.where(past, gate, -jnp.inf)
            rank = jnp.zeros((SUBLANES, t), jnp.int32)
            for n in range(n_blk):
                gn = gate[n:n + 1, :]
                rank = rank + jnp.where((gn > gate) | ((gn == gate) & (blk > n)), 1, 0)
            bias = jnp.where((rank < MOBA_TOPK) & past, 0.0, MASKED)
            bias_q.append(jnp.concatenate([bias, pad], axis=0).T.astype(BF16))
        bias_q = jnp.concatenate(bias_q, axis=0)
        lane_of = lax.broadcasted_iota(jnp.int32, (LANES, LANES), 0)
        for n in range(n_blk - 1):
            @pl.when(n < it)
            def _():
                pick = (lane_of == n).astype(BF16)
                bias_ref[n] = _dot(bias_q, pick)

    _reset_stats(max_ref, sum_ref, acc_ref)
    causal = jnp.where(_local_causal(t), 0.0, MASKED)
    causal4 = jnp.concatenate([causal] * B_HEADS, axis=0)

    def scores(j):
        s = _nt_dot(q_stack, k_ref[0, _tile_rows(j, t), :]) + _lane_tile(bias_ref[j], t)
        _record_scores(s + jnp.where(j == it, causal4, 0.0), j, 0, s_ref, max_ref)

    _for_each_tile(it + 1, scores, group=4)
    _finish_max(max_ref)

    def probs(j):
        rows = _tile_rows(j, t)
        for pair in range(B_HEADS // 2):
            _accumulate_probs(j, pair * 2 * t, 2 * t, v_ref[0, rows, pair * LANES:(pair + 1) * LANES],
                              s_ref, max_ref, sum_ref, acc_ref)

    _for_each_tile(it + 1, probs, group=4)

    o = [_normalised(h * t, t, sum_ref, acc_ref) for h in range(B_HEADS)]
    o_ref[0] = _interleave_pairs(o, t).astype(o_ref.dtype)


def _moba_attention(qb, kb, vb):
    b, s, w = qb.shape
    t = MOBA_BLOCK
    assert s // t <= SUBLANES
    qspec = pl.BlockSpec((1, t, w), lambda bi, i: (bi, i, 0))
    kvspec = pl.BlockSpec((1, s, w), lambda bi, i: (bi, 0, 0))
    return pl.pallas_call(
        functools.partial(_moba_kernel, t=t),
        grid=(b, s // t),
        in_specs=[qspec, kvspec, kvspec],
        out_specs=qspec,
        out_shape=jax.ShapeDtypeStruct((b, s, w), BF16),
        scratch_shapes=[pltpu.VMEM((LANES, w), BF16), pltpu.VMEM((LANES, w), BF16),
                        pltpu.VMEM((s // t, B_HEADS * t, LANES), F32)] + _ATT_SCRATCH(s // t, B_HEADS * t, t),
        compiler_params=_params("parallel", "arbitrary"),
        name="moba_attention",
    )(qb, kb, vb)


def _sublane_fold(x, rows):
    parts = [x[r:r + rows] for r in range(0, x.shape[0], rows)]
    while len(parts) > 1:
        parts = [parts[i] + parts[i + 1] for i in range(0, len(parts), 2)]
    return parts[0]


I16 = jnp.int16
I16_ROWS = 2 * SUBLANES
I16_MIN = -(2 ** 15)


def _dsa_kernel(q_ref, k_ref, v_ref, qi_ref, ki_ref, wi_ref, o_ref,
                keys_ref, hi_ref, lo_ref, ties_ref, s_ref, max_ref, sum_ref, acc_ref, *, t, topk):
    it = pl.program_id(1)
    width = q_ref.shape[2]
    heads = _head_lane_masks(t, width)
    causal = _local_causal(t)

    qidx = qi_ref[0]
    qi_stack = jnp.concatenate(
        [_stack_masked(qidx[:, g * width:(g + 1) * width], heads) for g in range(IDX_HEADS // 4)], axis=0)
    wi = wi_ref[0]
    w_rep = [jnp.broadcast_to(wi[:, h:h + 1], (t, LANES)) for h in range(IDX_HEADS)]

    def score_tile(j):
        logit = _nt_dot(qi_stack, ki_ref[0, _tile_rows(j, t), :])
        cols = []
        for c in range(0, t, LANES):
            acc = jnp.zeros((t, LANES), F32)
            for h in range(IDX_HEADS):
                acc = acc + jnp.maximum(logit[h * t:(h + 1) * t, c:c + LANES], 0.0) * w_rep[h]
            cols.append(acc)
        isc = jnp.concatenate(cols, axis=1) + 0.0
        isc = jnp.where(j == it, jnp.where(causal, isc, -jnp.inf), isc)
        bits = lax.bitcast_convert_type(isc, jnp.int32)
        key = bits ^ ((bits >> 31) & 0x7FFFFFFF)
        keys_ref[j] = key
        key_t = key.T
        hi_ref[j] = (key_t >> 16).astype(I16)
        lo_ref[j] = ((key_t & 0xFFFF) + I16_MIN).astype(I16)

    _for_each_tile(it + 1, score_tile, group=4)
    n_pairs = it // 2 + 1

    @pl.when(it % 2 == 0)
    def _():
        hi_ref[it + 1] = jnp.full((t, t), I16_MIN, I16)
        lo_ref[it + 1] = jnp.full((t, t), I16_MIN, I16)

    def count(ref, pred):
        def one(j, cnt):
            return cnt + _sublane_fold(jnp.where(pred(ref[j]), jnp.int16(1), jnp.int16(0)), I16_ROWS)
        cnt = lax.fori_loop(0, n_pairs, lambda i, cnt: one(2 * i + 1, one(2 * i, cnt)), jnp.zeros((I16_ROWS, t), I16))
        return jnp.sum(cnt.astype(jnp.int32), axis=0, keepdims=True)

    def kth_largest(ref, k):
        thr = jnp.where(count(ref, lambda x: x >= jnp.int16(0)) >= k, 0, jnp.full((1, t), I16_MIN, jnp.int32))

        def bit_step(i, thr):
            cand = thr | (1 << (14 - i))
            return jnp.where(count(ref, lambda x: x >= cand.astype(I16)) >= k, cand, thr)

        return lax.fori_loop(0, 15, bit_step, thr)

    thr_hi = kth_largest(hi_ref, topk)
    thr_hi16 = thr_hi.astype(I16)
    k_lo = topk - count(hi_ref, lambda x: x > thr_hi16)

    def keep_matching(j):
        lo_ref[j] = jnp.where(hi_ref[j] == thr_hi16, lo_ref[j], jnp.int16(I16_MIN))

    @pl.loop(0, n_pairs)
    def _(i):
        keep_matching(2 * i)
        keep_matching(2 * i + 1)

    thr_lo = kth_largest(lo_ref, k_lo)
    thr_lo16 = thr_lo.astype(I16)
    wanted = k_lo - count(lo_ref, lambda x: x > thr_lo16)
    ties = count(lo_ref, lambda x: x == thr_lo16)
    must_rank_ties = jnp.max(jnp.where(ties > wanted, 1, 0))
    thr = (thr_hi << 16) | (thr_lo - I16_MIN)

    def query_major(row):
        return _lane_tile(jnp.broadcast_to(row, (LANES, t)).T, t)

    thr_q = query_major(thr)

    q_stack = _stack_masked(q_ref[0], heads)
    not_causal = jnp.where(causal, 0.0, MASKED)
    _reset_stats(max_ref, sum_ref, acc_ref)

    def masked_scores(j, bias):
        bias = bias + jnp.where(j == it, not_causal, 0.0)
        s = _nt_dot(q_stack, k_ref[0, _tile_rows(j, t), :]) + jnp.concatenate([bias] * A_HEADS, axis=0)
        _record_scores(s, j, 0, s_ref, max_ref)

    @pl.when(must_rank_ties == 0)
    def _():
        _for_each_tile(it + 1, lambda j: masked_scores(j, jnp.where(keys_ref[j] >= thr_q, 0.0, MASKED)), group=4)

    @pl.when(must_rank_ties != 0)
    def _():
        wanted_q = query_major(wanted.astype(F32))
        r = lax.broadcasted_iota(jnp.int32, (t, t), 0)
        c = lax.broadcasted_iota(jnp.int32, (t, t), 1)
        before = (r < c).astype(BF16)
        ones = jnp.ones((t, LANES), BF16)
        ties_ref[...] = jnp.zeros(ties_ref.shape, F32)

        def select_tile(j):
            key = keys_ref[j]
            tie = key == thr_q
            tie_b = jnp.where(tie, 1.0, 0.0).astype(BF16)
            tie_rank = _lane_tile(ties_ref[...], t) + _dot(tie_b, before)
            masked_scores(j, jnp.where(key > thr_q, 0.0,
                                       jnp.where(tie, jnp.where(tie_rank < wanted_q, 0.0, MASKED), MASKED)))
            ties_ref[...] += _dot(tie_b, ones)

        _for_each_tile(it + 1, select_tile)

    _finish_max(max_ref)

    _for_each_tile(it + 1, lambda j: _accumulate_probs(
        j, 0, A_HEADS * t, v_ref[0, _tile_rows(j, t), :], s_ref, max_ref, sum_ref, acc_ref), group=4)

    o = [_normalised(h * t, t, sum_ref, acc_ref) for h in range(A_HEADS)]
    o_ref[0] = _interleave_pairs(o, t).astype(o_ref.dtype)


def _dsa_attention(qa, ka, va, qi, ki, wi):
    b, s, w = qa.shape
    t = ATT_TILE
    topk = min(INDEX_TOPK, s // 4)
    assert t >= topk
    row = lambda width: pl.BlockSpec((1, t, width), lambda bi, i: (bi, i, 0))
    full = lambda width: pl.BlockSpec((1, s, width), lambda bi, i: (bi, 0, 0))
    return pl.pallas_call(
        functools.partial(_dsa_kernel, t=t, topk=topk),
        grid=(b, s // t),
        in_specs=[row(w), full(w), full(va.shape[2]), row(qi.shape[2]), full(w), row(LANES)],
        out_specs=row(w),
        out_shape=jax.ShapeDtypeStruct((b, s, w), BF16),
        scratch_shapes=[pltpu.VMEM((s // t, t, t), jnp.int32),
                        pltpu.VMEM((s // t + s // t % 2, t, t), I16), pltpu.VMEM((s // t + s // t % 2, t, t), I16),
                        pltpu.VMEM((t, LANES), F32)]
                       + _ATT_SCRATCH(s // t, A_HEADS * t, t),
        compiler_params=_params("parallel", "arbitrary"),
        name="dsa_attention",
    )(qa, ka, va, qi, ki, wi)


def _merge_kernel(x_ref, oa_ref, ob_ref, oc_ref, gate_ref, wb_ref, wo_ref, o_ref):
    d = x_ref.shape[1]
    wa, wbw = oa_ref.shape[1], ob_ref.shape[1]
    merged = gate_ref[:, 0:d].astype(F32) * _dot(oa_ref[...], wb_ref[0:wa, :])
    merged = merged + gate_ref[:, d:2 * d].astype(F32) * _dot(ob_ref[...], wb_ref[wa:wa + wbw, :])
    merged = merged + gate_ref[:, 2 * d:3 * d].astype(F32) * _dot(oc_ref[...], wb_ref[wa + wbw:, :])
    o_ref[...] = x_ref[...] + _dot(merged.astype(BF16), wo_ref[...])


def _merge(x, oa, ob, oc, gate, wb, wo, tm):
    n, d = x.shape
    row = lambda w: pl.BlockSpec((tm, w), lambda i: (i, 0))
    return pl.pallas_call(
        _merge_kernel,
        grid=(n // tm,),
        in_specs=[row(d), row(oa.shape[1]), row(ob.shape[1]), row(oc.shape[1]), row(gate.shape[1]),
                  _resident(wb.shape), _resident(wo.shape)],
        out_specs=row(d),
        out_shape=jax.ShapeDtypeStruct((n, d), F32),
        compiler_params=_params("parallel"),
        name="merge_out",
    )(x, oa, ob, oc, gate, wb, wo)


def kernel(x, positions, norm_g, w_in, qk_norm_g, lambda_params, diff_subln_g, w_branch, w_out,
           ffn_w_gate, ffn_w_up, ffn_w_down):
    b, s, d = x.shape
    n = b * s
    depth = norm_g.shape[0]
    tm = min(512, n)
    assert n % tm == 0 and s % ATT_TILE == 0 and ATT_TILE == MOBA_BLOCK

    cos, sin = _rope_tables(positions, min(4 * tm, n))
    xf = x.reshape(n, d)
    for layer in range(depth):
        lam_init = 0.8 - 0.6 * math.exp(-0.3 * layer)
        ffn_w = lambda i: (ffn_w_gate[layer, i].astype(BF16), ffn_w_up[layer, i].astype(BF16),
                           ffn_w_down[layer, i].astype(BF16))
        xf = _ffn(xf, norm_g[layer, 0][None, :], *ffn_w(0), tm)

        qkg = jnp.tile(qk_norm_g[layer], (1, LANES // HEAD_DIM))
        outs = _proj(xf, norm_g[layer, 1][None, :], qkg, cos, sin, _proj_weights(w_in[layer]), tm)
        p = {name: o.reshape(b, s, o.shape[1]) for (name, *_), o in zip(_PROJ_OUTPUTS, outs)}
        o_a = _dsa_attention(p["qa"], p["ka"], p["va"], p["qi"], p["ki"], p["wi"])
        o_b = _moba_attention(p["qb"], p["kb"], p["vb"])
        o_c = _diff_attention(p["qc"], p["kc"], p["vc"], lambda_params[layer], diff_subln_g[layer][None, :], lam_init)
        xf = _merge(xf, o_a.reshape(n, -1), o_b.reshape(n, -1), o_c.reshape(n, -1), p["gate"].reshape(n, -1),
                    w_branch[layer].astype(BF16), w_out[layer].astype(BF16), tm)

        xf = _ffn(xf, norm_g[layer, 2][None, :], *ffn_w(1), tm)
    return xf.reshape(b, s, d)
```

```python
import functools
import math

import jax
import jax.numpy as jnp
from jax import lax
from jax.experimental import pallas as pl
from jax.experimental.pallas import tpu as pltpu

F32 = jnp.float32
BF16 = jnp.bfloat16

HEAD_DIM = 64
ROT_DIM = HEAD_DIM // 4
ROT_HALF = ROT_DIM // 2
ROPE_THETA = 500000.0
EPS = 1e-6
A_HEADS = 4
IDX_HEADS = 8
IDX_DIM = 64
INDEX_TOPK = 256
B_HEADS = 4
MOBA_BLOCK = 256
MOBA_TOPK = 3
C_HEADS = 4
C_VDIM = 2 * HEAD_DIM

LANES = 128
SUBLANES = 8
MXU_DIM = 256
ATT_TILE = 256
MASKED = -1e30
VMEM_LIMIT = 56 * 1024 * 1024
LOG2E = 1.4426950408889634


def _nt_dot(a, b):
    return lax.dot_general(a, b, (((1,), (1,)), ((), ())), preferred_element_type=F32)


def _dot(a, b):
    return jnp.dot(a, b, preferred_element_type=F32)


def _resident(shape):
    nd = len(shape)
    return pl.BlockSpec(shape, lambda *_: (0,) * nd, pipeline_mode=pl.Buffered(1))


def _params(*sem):
    return pltpu.CompilerParams(dimension_semantics=sem, vmem_limit_bytes=VMEM_LIMIT)


def _rope_table_kernel(pos_ref, freq_ref, cos_ref, sin_ref):
    ang = pos_ref[...].astype(F32) * freq_ref[...]
    lane = lax.broadcasted_iota(jnp.int32, ang.shape, 1) % HEAD_DIM
    c = jnp.cos(ang)
    s = jnp.sin(ang)
    cos_ref[...] = jnp.where(lane < ROT_DIM, c, 1.0)
    sin_ref[...] = jnp.where(lane < ROT_HALF, -s, jnp.where(lane < ROT_DIM, s, 0.0))


def _rope_tables(positions, tm):
    n = positions.size
    pos = positions.reshape(n, 1)
    inv_freq = jnp.power(ROPE_THETA, -jnp.arange(0, ROT_DIM, 2, dtype=F32) / ROT_DIM)
    freq = jnp.tile(jnp.concatenate([inv_freq, inv_freq, jnp.zeros((HEAD_DIM - ROT_DIM,), F32)]), 2)[None, :]
    return pl.pallas_call(
        _rope_table_kernel,
        grid=(n // tm,),
        in_specs=[pl.BlockSpec((tm, 1), lambda i: (i, 0)), pl.BlockSpec((1, LANES), lambda i: (0, 0))],
        out_specs=[pl.BlockSpec((tm, LANES), lambda i: (i, 0))] * 2,
        out_shape=[jax.ShapeDtypeStruct((n, LANES), F32)] * 2,
        compiler_params=_params("parallel"),
        name="rope_tables",
    )(pos, freq)


def _rmsnorm_rows(x, g):
    return x * lax.rsqrt(jnp.mean(x * x, axis=-1, keepdims=True) + EPS) * g


def _ffn_kernel(x_ref, g_ref, wg_ref, wu_ref, wd_ref, o_ref, *, chunk):
    x = x_ref[...]
    h = _rmsnorm_rows(x, g_ref[...]).astype(BF16)
    acc = jnp.zeros(x.shape, F32)
    for c in range(0, wg_ref.shape[1], chunk):
        a = _dot(h, wg_ref[:, c:c + chunk])
        b = _dot(h, wu_ref[:, c:c + chunk])
        t = (a * jax.nn.sigmoid(a) * b).astype(BF16)
        acc = acc + _dot(t, wd_ref[c:c + chunk, :])
    o_ref[...] = x + 0.5 * acc


def _ffn(x, g, wg, wu, wd, tm):
    n, d = x.shape
    f = wg.shape[1]
    return pl.pallas_call(
        functools.partial(_ffn_kernel, chunk=MXU_DIM),
        grid=(n // tm,),
        in_specs=[pl.BlockSpec((tm, d), lambda i: (i, 0)), _resident((1, d)),
                  _resident((d, f)), _resident((d, f)), _resident((f, d))],
        out_specs=pl.BlockSpec((tm, d), lambda i: (i, 0)),
        out_shape=jax.ShapeDtypeStruct((n, d), F32),
        compiler_params=_params("parallel"),
        name="ffn",
    )(x, g, wg, wu, wd)


_PROJ_COLUMNS = (
    ("qa", 256, "norm_rope_scale", 0),
    ("shared", 256, "shared", 1),
    ("qi", 512, "rope", None),
    ("wi", 128, "index_weight", None),
    ("qb", 256, "norm_rope_scale", 2),
    ("kb", 256, "norm_rope", 3),
    ("vb", 256, "plain", None),
    ("qc", 512, "norm_rope_scale", 4),
    ("kc", 512, "norm_rope", 5),
    ("vc", 512, "plain", None),
    ("gate", 3072, "sigmoid", None),
)
_PROJ_OUTPUTS = (
    ("qa", 256, BF16), ("ka", 256, BF16), ("va", 128, BF16), ("qi", 512, BF16), ("ki", 256, BF16), ("wi", 128, F32),
    ("qb", 256, BF16), ("kb", 256, BF16), ("vb", 256, BF16), ("qc", 512, BF16), ("kc", 512, BF16), ("vc", 512, BF16),
    ("gate", 3072, BF16),
)
_Q_SCALE = HEAD_DIM ** -0.5 * LOG2E


def _proj_kernel(x_ref, g_ref, qkg_ref, cos_ref, sin_ref, w_ref, *o_refs):
    out = {name: ref for (name, _, _), ref in zip(_PROJ_OUTPUTS, o_refs)}
    tm = x_ref.shape[0]
    h = _rmsnorm_rows(x_ref[...], g_ref[...]).astype(BF16)
    cos = cos_ref[...]
    sin = sin_ref[...]
    lane = lax.broadcasted_iota(jnp.int32, (tm, LANES), 1)
    take_upper = (lane % HEAD_DIM) < ROT_HALF
    low_half = lane < HEAD_DIM
    r = lax.broadcasted_iota(jnp.int32, (LANES, LANES), 0) // HEAD_DIM
    c = lax.broadcasted_iota(jnp.int32, (LANES, LANES), 1) // HEAD_DIM
    same_head = (r == c).astype(BF16)

    def epilogue(y, kind, g_idx):
        if kind in ("norm_rope_scale", "norm_rope"):
            ssq = _dot((y * y).astype(BF16), same_head)
            y = y * lax.rsqrt(ssq * (1.0 / HEAD_DIM) + EPS) * qkg_ref[g_idx:g_idx + 1, :]
        if kind in ("norm_rope_scale", "norm_rope", "rope"):
            partner = jnp.where(take_upper, pltpu.roll(y, LANES - ROT_HALF, 1), pltpu.roll(y, ROT_HALF, 1))
            y = y * cos + partner * sin
        if kind == "norm_rope_scale":
            y = y * _Q_SCALE
        if kind == "index_weight":
            y = y * ((IDX_HEADS * IDX_DIM) ** -0.5)
        if kind == "sigmoid":
            y = jax.nn.sigmoid(y)
        return y

    def on_both_halves(y):
        return jnp.where(low_half, y, pltpu.roll(y, HEAD_DIM, 1))

    col = 0
    for name, width, kind, g_idx in _PROJ_COLUMNS:
        step = min(width, MXU_DIM)
        for c0 in range(0, width, step):
            y = _dot(h, w_ref[:, col + c0:col + c0 + step])
            if kind == "shared":
                key_value, index_key = y[:, :LANES], y[:, LANES:]
                ka = on_both_halves(epilogue(key_value, "norm_rope", g_idx)).astype(BF16)
                ki = on_both_halves(epilogue(index_key, "rope", None)).astype(BF16)
                for s0 in range(0, out["ka"].shape[1], LANES):
                    out["ka"][:, s0:s0 + LANES] = ka
                    out["ki"][:, s0:s0 + LANES] = ki
                out["va"][...] = jnp.where(low_half, pltpu.roll(key_value, HEAD_DIM, 1), 1.0).astype(BF16)
                continue
            o_ref = out[name]
            for s0 in range(0, step, LANES):
                o_ref[:, c0 + s0:c0 + s0 + LANES] = epilogue(y[:, s0:s0 + LANES], kind, g_idx).astype(o_ref.dtype)
        col += width


def _proj_weights(w_in):
    splits = (256, 64, 64, 512, 64, 8, 256, 256, 256, 512, 512, 512, 1024, 1024, 1024)
    offs = [0]
    for s in splits:
        offs.append(offs[-1] + s)
    qa, ka, va, qi, ki, wi, qb, kb, vb, qc, kc, vc, ga, gb, gc = [w_in[:, offs[i]:offs[i + 1]] for i in range(len(splits))]
    zeros = lambda n: jnp.zeros((w_in.shape[0], n), w_in.dtype)
    groups = dict(qa=qa, shared=jnp.concatenate([ka, va, ki, zeros(HEAD_DIM)], axis=1), qi=qi,
                  wi=jnp.concatenate([wi, zeros(LANES - IDX_HEADS)], axis=1),
                  qb=qb, kb=kb, vb=vb, qc=qc, kc=kc, vc=vc, gate=jnp.concatenate([ga, gb, gc], axis=1))
    return jnp.concatenate([groups[name] for name, *_ in _PROJ_COLUMNS], axis=1).astype(BF16)


def _proj(x, g, qkg, cos, sin, w, tm):
    n, d = x.shape
    row = lambda width: pl.BlockSpec((tm, width), lambda i: (i, 0))
    return pl.pallas_call(
        _proj_kernel,
        grid=(n // tm,),
        in_specs=[row(d), _resident((1, d)), _resident(qkg.shape), row(LANES), row(LANES), _resident(w.shape)],
        out_specs=[row(width) for _, width, _ in _PROJ_OUTPUTS],
        out_shape=[jax.ShapeDtypeStruct((n, width), dtype) for _, width, dtype in _PROJ_OUTPUTS],
        compiler_params=_params("parallel"),
        name="in_proj",
    )(x, g, qkg, cos, sin, w)


def _lane_fold(x, op):
    out = x[:, :LANES]
    for c in range(LANES, x.shape[1], LANES):
        out = op(out, x[:, c:c + LANES])
    return out


def _lane_tile(x, width):
    return jnp.concatenate([x] * (width // LANES), axis=1)


def _reset_stats(max_ref, acc_ref):
    max_ref[...] = jnp.full(max_ref.shape, MASKED, F32)
    acc_ref[...] = jnp.zeros(acc_ref.shape, F32)


def _record_scores(s, j, r0, s_ref, max_ref):
    n = s.shape[0]
    s_ref[j, r0:r0 + n, :] = s
    max_ref[r0:r0 + n, :] = jnp.maximum(max_ref[r0:r0 + n, :], _lane_fold(s, jnp.maximum))


def _finish_max(max_ref):
    m = max_ref[...]
    max_ref[...] = jnp.broadcast_to(jnp.max(m, axis=-1, keepdims=True), m.shape)


def _accumulate_probs(j, r0, n, v, s_ref, max_ref, acc_ref):
    s = s_ref[j, r0:r0 + n, :]
    p = jnp.exp2(s - _lane_tile(max_ref[r0:r0 + n, :], s.shape[1]))
    acc_ref[r0:r0 + n, :] += _dot(p.astype(BF16), v)


def _with_ones(v):
    return jnp.concatenate([v, jnp.ones(v.shape, v.dtype)], axis=1)


def _normalised(r0, n, acc_ref):
    return acc_ref[r0:r0 + n, :LANES] / acc_ref[r0:r0 + n, LANES:]


def _head_lane_masks(rows, width):
    lane = lax.broadcasted_iota(jnp.int32, (rows, width), 1)
    return [(lane >= h * HEAD_DIM) & (lane < (h + 1) * HEAD_DIM) for h in range(width // HEAD_DIM)]


def _stack_masked(q, masks):
    zero = jnp.zeros_like(q)
    return jnp.concatenate([jnp.where(m, q, zero) for m in masks], axis=0)


def _local_causal(t):
    row = lax.broadcasted_iota(jnp.int32, (t, t), 0)
    col = lax.broadcasted_iota(jnp.int32, (t, t), 1)
    return col <= row


def _tile_rows(j, t):
    return pl.ds(pl.multiple_of(j * t, t), t)


def _for_each_tile(n, body, group=2):
    @pl.loop(0, n // group)
    def _(i):
        for k in range(group):
            body(group * i + k)

    part = group // 2
    while part >= 1:
        @pl.when(n % (2 * part) >= part)
        def _():
            for k in range(part):
                body(n - n % (2 * part) + k)
        part //= 2


def _interleave_pairs(o, t):
    lane = lax.broadcasted_iota(jnp.int32, (t, LANES), 1)
    low = lane < HEAD_DIM
    return jnp.concatenate([jnp.where(low, o[0], o[1]), jnp.where(low, o[2], o[3])], axis=1)


_ATT_SCRATCH = lambda n_tiles, rows, t, acc_width: [
    pltpu.VMEM((n_tiles, rows, t), F32),
    pltpu.VMEM((rows, LANES), F32),
    pltpu.VMEM((rows, acc_width), F32),
]


def _diff_kernel(lam_ref, g_ref, q_ref, k_ref, v_ref, o_ref, s_ref, max_ref, acc_ref, *, lam_init, t):
    it = pl.program_id(1)
    lane = lax.broadcasted_iota(jnp.int32, (t, C_VDIM), 1)
    maps = [lane < HEAD_DIM, lane >= HEAD_DIM]
    q_all = q_ref[0]
    qs = [_stack_masked(q_all[:, h * C_VDIM:(h + 1) * C_VDIM], maps) for h in range(C_HEADS)]
    _reset_stats(max_ref, acc_ref)
    causal = jnp.where(_local_causal(t), 0.0, MASKED)
    causal2 = jnp.concatenate([causal, causal], axis=0)

    def scores(j):
        rows = _tile_rows(j, t)
        bias = jnp.where(j == it, causal2, 0.0)
        for h in range(C_HEADS):
            s = _nt_dot(qs[h], k_ref[0, rows, h * C_VDIM:(h + 1) * C_VDIM]) + bias
            _record_scores(s, j, h * 2 * t, s_ref, max_ref)

    _for_each_tile(it + 1, scores, group=4)
    _finish_max(max_ref)

    def probs(j):
        rows = _tile_rows(j, t)
        for h in range(C_HEADS):
            _accumulate_probs(j, h * 2 * t, 2 * t, _with_ones(v_ref[0, rows, h * C_VDIM:(h + 1) * C_VDIM]),
                              s_ref, max_ref, acc_ref)

    _for_each_tile(it + 1, probs, group=4)

    lp = lam_ref[...]
    lam = (jnp.exp(jnp.sum(lp[0:1] * lp[1:2], axis=-1, keepdims=True))
           - jnp.exp(jnp.sum(lp[2:3] * lp[3:4], axis=-1, keepdims=True)) + lam_init)
    for h in range(C_HEADS):
        o = _normalised(h * 2 * t, t, acc_ref) - lam * _normalised(h * 2 * t + t, t, acc_ref)
        o = o * lax.rsqrt(jnp.mean(o * o, axis=-1, keepdims=True) + EPS) * g_ref[...] * (1.0 - lam_init)
        o_ref[0, :, h * C_VDIM:(h + 1) * C_VDIM] = o.astype(o_ref.dtype)


def _diff_attention(qc, kc, vc, lam_p, subln_g, lam_init):
    b, s, w = qc.shape
    t = ATT_TILE
    qspec = pl.BlockSpec((1, t, w), lambda bi, i: (bi, i, 0))
    kvspec = pl.BlockSpec((1, s, w), lambda bi, i: (bi, 0, 0))
    return pl.pallas_call(
        functools.partial(_diff_kernel, lam_init=lam_init, t=t),
        grid=(b, s // t),
        in_specs=[pl.BlockSpec(lam_p.shape, lambda bi, i: (0, 0)),
                  pl.BlockSpec((1, C_VDIM), lambda bi, i: (0, 0)), qspec, kvspec, kvspec],
        out_specs=qspec,
        out_shape=jax.ShapeDtypeStruct((b, s, w), BF16),
        scratch_shapes=_ATT_SCRATCH(s // t, 2 * C_HEADS * t, t, 2 * LANES),
        compiler_params=_params("parallel", "arbitrary"),
        name="diff_attention",
    )(lam_p, subln_g, qc, kc, vc)


def _moba_kernel(q_ref, k_ref, v_ref, o_ref, kmean_hi, kmean_lo, bias_ref, s_ref, max_ref, acc_ref, *, t):
    it = pl.program_id(1)
    n_blk = k_ref.shape[1] // t
    width = q_ref.shape[2]

    @pl.when(it == 0)
    def _():
        kmean_hi[...] = jnp.zeros_like(kmean_hi)
        kmean_lo[...] = jnp.zeros_like(kmean_lo)
        for n in range(n_blk):
            km = jnp.mean(k_ref[0, n * t:(n + 1) * t, :].astype(F32), axis=0, keepdims=True)
            hi = km.astype(BF16)
            kmean_hi[n:n + 1, :] = hi
            kmean_lo[n:n + 1, :] = (km - hi.astype(F32)).astype(BF16)

    heads = _head_lane_masks(t, width)
    q_stack = _stack_masked(q_ref[0], heads)

    no_bias = jnp.zeros(bias_ref.shape[1:], F32)
    bias_ref[it] = no_bias

    @pl.when(it <= MOBA_TOPK)
    def _():
        for n in range(min(MOBA_TOPK, n_blk)):
            @pl.when(n < it)
            def _():
                bias_ref[n] = no_bias

    @pl.when(it > MOBA_TOPK)
    def _():
        blk = lax.broadcasted_iota(jnp.int32, (SUBLANES, t), 0)
        past = blk < it
        pad = jnp.full((LANES - SUBLANES, t), MASKED, F32)
        bias_q = []
        for h in range(B_HEADS):
            qh = q_stack[h * t:(h + 1) * t]
            gate = (_nt_dot(kmean_hi[...], qh) + _nt_dot(kmean_lo[...], qh))[:SUBLANES]
            gate = jnp.where(past, gate, -jnp.inf)
            rank = jnp.zeros((SUBLANES, t), jnp.int32)
            for n in range(n_blk):
                gn = gate[n:n + 1, :]
                rank = rank + jnp.where((gn > gate) | ((gn == gate) & (blk > n)), 1, 0)
            bias = jnp.where((rank < MOBA_TOPK) & past, 0.0, MASKED)
            bias_q.append(jnp.concatenate([bias, pad], axis=0).T.astype(BF16))
        bias_q = jnp.concatenate(bias_q, axis=0)
        lane_of = lax.broadcasted_iota(jnp.int32, (LANES, LANES), 0)
        for n in range(n_blk - 1):
            @pl.when(n < it)
            def _():
                pick = (lane_of == n).astype(BF16)
                bias_ref[n] = _dot(bias_q, pick)

    _reset_stats(max_ref, acc_ref)
    causal = jnp.where(_local_causal(t), 0.0, MASKED)
    causal4 = jnp.concatenate([causal] * B_HEADS, axis=0)

    def scores(j):
        s = _nt_dot(q_stack, k_ref[0, _tile_rows(j, t), :]) + _lane_tile(bias_ref[j], t)
        _record_scores(s + jnp.where(j == it, causal4, 0.0), j, 0, s_ref, max_ref)

    _for_each_tile(it + 1, scores, group=4)
    _finish_max(max_ref)

    def probs(j):
        rows = _tile_rows(j, t)
        for pair in range(B_HEADS // 2):
            _accumulate_probs(j, pair * 2 * t, 2 * t, _with_ones(v_ref[0, rows, pair * LANES:(pair + 1) * LANES]),
                              s_ref, max_ref, acc_ref)

    _for_each_tile(it + 1, probs, group=4)

    o = [_normalised(h * t, t, acc_ref) for h in range(B_HEADS)]
    o_ref[0] = _interleave_pairs(o, t).astype(o_ref.dtype)


def _moba_attention(qb, kb, vb):
    b, s, w = qb.shape
    t = MOBA_BLOCK
    assert s // t <= SUBLANES
    qspec = pl.BlockSpec((1, t, w), lambda bi, i: (bi, i, 0))
    kvspec = pl.BlockSpec((1, s, w), lambda bi, i: (bi, 0, 0))
    return pl.pallas_call(
        functools.partial(_moba_kernel, t=t),
        grid=(b, s // t),
        in_specs=[qspec, kvspec, kvspec],
        out_specs=qspec,
        out_shape=jax.ShapeDtypeStruct((b, s, w), BF16),
        scratch_shapes=[pltpu.VMEM((LANES, w), BF16), pltpu.VMEM((LANES, w), BF16),
                        pltpu.VMEM((s // t, B_HEADS * t, LANES), F32)]
                       + _ATT_SCRATCH(s // t, B_HEADS * t, t, 2 * LANES),
        compiler_params=_params("parallel", "arbitrary"),
        name="moba_attention",
    )(qb, kb, vb)


def _sublane_fold(x, rows):
    parts = [x[r:r + rows] for r in range(0, x.shape[0], rows)]
    while len(parts) > 1:
        parts = [parts[i] + parts[i + 1] for i in range(0, len(parts), 2)]
    return parts[0]


I16 = jnp.int16
I16_ROWS = 2 * SUBLANES
I16_MIN = -(2 ** 15)


def _dsa_kernel(q_ref, k_ref, v_ref, qi_ref, ki_ref, wi_ref, o_ref,
                keys_ref, hi_ref, lo_ref, ties_ref, s_ref, max_ref, acc_ref, *, t, topk):
    it = pl.program_id(1)
    width = q_ref.shape[2]
    heads = _head_lane_masks(t, width)
    causal = _local_causal(t)

    qidx = qi_ref[0]
    qi_stack = jnp.concatenate(
        [_stack_masked(qidx[:, g * width:(g + 1) * width], heads) for g in range(IDX_HEADS // 4)], axis=0)
    wi = wi_ref[0]
    w_rep = [jnp.broadcast_to(wi[:, h:h + 1], (t, LANES)) for h in range(IDX_HEADS)]

    def score_tile(j):
        logit = _nt_dot(qi_stack, ki_ref[0, _tile_rows(j, t), :])
        cols = []
        for c in range(0, t, LANES):
            acc = jnp.zeros((t, LANES), F32)
            for h in range(IDX_HEADS):
                acc = acc + jnp.maximum(logit[h * t:(h + 1) * t, c:c + LANES], 0.0) * w_rep[h]
            cols.append(acc)
        isc = jnp.concatenate(cols, axis=1) + 0.0
        isc = jnp.where(j == it, jnp.where(causal, isc, -jnp.inf), isc)
        bits = lax.bitcast_convert_type(isc, jnp.int32)
        key = bits ^ ((bits >> 31) & 0x7FFFFFFF)
        keys_ref[j] = key
        key_t = key.T
        hi_ref[j] = (key_t >> 16).astype(I16)
        lo_ref[j] = ((key_t & 0xFFFF) + I16_MIN).astype(I16)

    _for_each_tile(it + 1, score_tile, group=4)
    n_pairs = it // 2 + 1

    @pl.when(it % 2 == 0)
    def _():
        hi_ref[it + 1] = jnp.full((t, t), I16_MIN, I16)
        lo_ref[it + 1] = jnp.full((t, t), I16_MIN, I16)

    def count(ref, pred):
        def one(j, cnt):
            return cnt + _sublane_fold(jnp.where(pred(ref[j]), jnp.int16(1), jnp.int16(0)), I16_ROWS)
        cnt = lax.fori_loop(0, n_pairs, lambda i, cnt: one(2 * i + 1, one(2 * i, cnt)), jnp.zeros((I16_ROWS, t), I16))
        return jnp.sum(cnt.astype(jnp.int32), axis=0, keepdims=True)

    def kth_largest(ref, k):
        thr = jnp.where(count(ref, lambda x: x >= jnp.int16(0)) >= k, 0, jnp.full((1, t), I16_MIN, jnp.int32))

        def bit_step(i, thr):
            cand = thr | (1 << (14 - i))
            return jnp.where(count(ref, lambda x: x >= cand.astype(I16)) >= k, cand, thr)

        return lax.fori_loop(0, 15, bit_step, thr)

    thr_hi = kth_largest(hi_ref, topk)
    thr_hi16 = thr_hi.astype(I16)
    k_lo = topk - count(hi_ref, lambda x: x > thr_hi16)

    def keep_matching(j):
        lo_ref[j] = jnp.where(hi_ref[j] == thr_hi16, lo_ref[j], jnp.int16(I16_MIN))

    @pl.loop(0, n_pairs)
    def _(i):
        keep_matching(2 * i)
        keep_matching(2 * i + 1)

    thr_lo = kth_largest(lo_ref, k_lo)
    thr_lo16 = thr_lo.astype(I16)
    wanted = (k_lo - count(lo_ref, lambda x: x > thr_lo16)).astype(F32)
    thr = (thr_hi << 16) | (thr_lo - I16_MIN)
    thr_q = _lane_tile(jnp.broadcast_to(thr, (LANES, t)).T, t)
    wanted_q = _lane_tile(jnp.broadcast_to(wanted, (LANES, t)).T, t)

    q_stack = _stack_masked(q_ref[0], heads)
    r = lax.broadcasted_iota(jnp.int32, (t, t), 0)
    c = lax.broadcasted_iota(jnp.int32, (t, t), 1)
    before = (r < c).astype(BF16)
    ones = jnp.ones((t, LANES), BF16)
    _reset_stats(max_ref, acc_ref)

    ties_ref[...] = jnp.zeros(ties_ref.shape, F32)

    not_causal = jnp.where(causal, 0.0, MASKED)

    def select_tile(j):
        key = keys_ref[j]
        tie = key == thr_q
        tie_b = jnp.where(tie, 1.0, 0.0).astype(BF16)
        tie_rank = _lane_tile(ties_ref[...], t) + _dot(tie_b, before)
        bias = jnp.where(key > thr_q, 0.0, jnp.where(tie, jnp.where(tie_rank < wanted_q, 0.0, MASKED), MASKED))
        bias = bias + jnp.where(j == it, not_causal, 0.0)
        s = _nt_dot(q_stack, k_ref[0, _tile_rows(j, t), :]) + jnp.concatenate([bias] * A_HEADS, axis=0)
        _record_scores(s, j, 0, s_ref, max_ref)
        ties_ref[...] += _dot(tie_b, ones)

    _for_each_tile(it + 1, select_tile)
    _finish_max(max_ref)

    _for_each_tile(it + 1, lambda j: _accumulate_probs(
        j, 0, A_HEADS * t, v_ref[0, _tile_rows(j, t), :], s_ref, max_ref, acc_ref), group=4)

    sums_only = lax.broadcasted_iota(jnp.int32, (t, LANES), 1) >= HEAD_DIM
    o = []
    for h in range(A_HEADS):
        a = acc_ref[h * t:(h + 1) * t, :]
        inv = 1.0 / jnp.where(sums_only, a, 1.0)
        o.append(pltpu.roll(a, HEAD_DIM, 1) * inv if h % 2 else a * pltpu.roll(inv, HEAD_DIM, 1))
    o_ref[0] = _interleave_pairs(o, t).astype(o_ref.dtype)


def _dsa_attention(qa, ka, va, qi, ki, wi):
    b, s, w = qa.shape
    t = ATT_TILE
    topk = min(INDEX_TOPK, s // 4)
    assert t >= topk
    row = lambda width: pl.BlockSpec((1, t, width), lambda bi, i: (bi, i, 0))
    full = lambda width: pl.BlockSpec((1, s, width), lambda bi, i: (bi, 0, 0))
    return pl.pallas_call(
        functools.partial(_dsa_kernel, t=t, topk=topk),
        grid=(b, s // t),
        in_specs=[row(w), full(w), full(va.shape[2]), row(qi.shape[2]), full(w), row(LANES)],
        out_specs=row(w),
        out_shape=jax.ShapeDtypeStruct((b, s, w), BF16),
        scratch_shapes=[pltpu.VMEM((s // t, t, t), jnp.int32),
                        pltpu.VMEM((s // t + s // t % 2, t, t), I16), pltpu.VMEM((s // t + s // t % 2, t, t), I16),
                        pltpu.VMEM((t, LANES), F32)]
                       + _ATT_SCRATCH(s // t, A_HEADS * t, t, LANES),
        compiler_params=_params("parallel", "arbitrary"),
        name="dsa_attention",
    )(qa, ka, va, qi, ki, wi)


def _merge_kernel(x_ref, oa_ref, ob_ref, oc_ref, gate_ref, wb_ref, wo_ref, o_ref):
    d = x_ref.shape[1]
    wa, wbw = oa_ref.shape[1], ob_ref.shape[1]
    merged = gate_ref[:, 0:d].astype(F32) * _dot(oa_ref[...], wb_ref[0:wa, :])
    merged = merged + gate_ref[:, d:2 * d].astype(F32) * _dot(ob_ref[...], wb_ref[wa:wa + wbw, :])
    merged = merged + gate_ref[:, 2 * d:3 * d].astype(F32) * _dot(oc_ref[...], wb_ref[wa + wbw:, :])
    o_ref[...] = x_ref[...] + _dot(merged.astype(BF16), wo_ref[...])


def _merge(x, oa, ob, oc, gate, wb, wo, tm):
    n, d = x.shape
    row = lambda w: pl.BlockSpec((tm, w), lambda i: (i, 0))
    return pl.pallas_call(
        _merge_kernel,
        grid=(n // tm,),
        in_specs=[row(d), row(oa.shape[1]), row(ob.shape[1]), row(oc.shape[1]), row(gate.shape[1]),
                  _resident(wb.shape), _resident(wo.shape)],
        out_specs=row(d),
        out_shape=jax.ShapeDtypeStruct((n, d), F32),
        compiler_params=_params("parallel"),
        name="merge_out",
    )(x, oa, ob, oc, gate, wb, wo)


def kernel(x, positions, norm_g, w_in, qk_norm_g, lambda_params, diff_subln_g, w_branch, w_out,
           ffn_w_gate, ffn_w_up, ffn_w_down):
    b, s, d = x.shape
    n = b * s
    depth = norm_g.shape[0]
    tm = min(512, n)
    assert n % tm == 0 and s % ATT_TILE == 0 and ATT_TILE == MOBA_BLOCK

    cos, sin = _rope_tables(positions, min(4 * tm, n))
    xf = x.reshape(n, d)
    for layer in range(depth):
        lam_init = 0.8 - 0.6 * math.exp(-0.3 * layer)
        ffn_w = lambda i: (ffn_w_gate[layer, i].astype(BF16), ffn_w_up[layer, i].astype(BF16),
                           ffn_w_down[layer, i].astype(BF16))
        xf = _ffn(xf, norm_g[layer, 0][None, :], *ffn_w(0), tm)

        qkg = jnp.tile(qk_norm_g[layer], (1, LANES // HEAD_DIM))
        outs = _proj(xf, norm_g[layer, 1][None, :], qkg, cos, sin, _proj_weights(w_in[layer]), tm)
        p = {name: o.reshape(b, s, o.shape[1]) for (name, *_), o in zip(_PROJ_OUTPUTS, outs)}
        o_a = _dsa_attention(p["qa"], p["ka"], p["va"], p["qi"], p["ki"], p["wi"])
        o_b = _moba_attention(p["qb"], p["kb"], p["vb"])
        o_c = _diff_attention(p["qc"], p["kc"], p["vc"], lambda_params[layer], diff_subln_g[layer][None, :], lam_init)
        xf = _merge(xf, o_a.reshape(n, -1), o_b.reshape(n, -1), o_c.reshape(n, -1), p["gate"].reshape(n, -1),
                    w_branch[layer].astype(BF16), w_out[layer].astype(BF16), tm)

        xf = _ffn(xf, norm_g[layer, 2][None, :], *ffn_w(1), tm)
    return xf.reshape(b, s, d)
```

```python
import functools
import math

import jax
import jax.numpy as jnp
from jax import lax
from jax.experimental import pallas as pl
from jax.experimental.pallas import tpu as pltpu

F32 = jnp.float32
BF16 = jnp.bfloat16

HEAD_DIM = 64
ROT_DIM = HEAD_DIM // 4
ROT_HALF = ROT_DIM // 2
ROPE_THETA = 500000.0
EPS = 1e-6
A_HEADS = 4
IDX_HEADS = 8
IDX_DIM = 64
INDEX_TOPK = 256
B_HEADS = 4
MOBA_BLOCK = 256
MOBA_TOPK = 3
C_HEADS = 4
C_VDIM = 2 * HEAD_DIM

LANES = 128
SUBLANES = 8
MXU_DIM = 256
ATT_TILE = 256
MASKED = -1e30
VMEM_LIMIT = 56 * 1024 * 1024
LOG2E = 1.4426950408889634


def _nt_dot(a, b):
    return lax.dot_general(a, b, (((1,), (1,)), ((), ())), preferred_element_type=F32)


def _dot(a, b):
    return jnp.dot(a, b, preferred_element_type=F32)


def _resident(shape):
    nd = len(shape)
    return pl.BlockSpec(shape, lambda *_: (0,) * nd, pipeline_mode=pl.Buffered(1))


def _params(*sem):
    return pltpu.CompilerParams(dimension_semantics=sem, vmem_limit_bytes=VMEM_LIMIT)


def _rope_table_kernel(pos_ref, freq_ref, cos_ref, sin_ref):
    ang = pos_ref[...].astype(F32) * freq_ref[...]
    lane = lax.broadcasted_iota(jnp.int32, ang.shape, 1) % HEAD_DIM
    c = jnp.cos(ang)
    s = jnp.sin(ang)
    cos_ref[...] = jnp.where(lane < ROT_DIM, c, 1.0)
    sin_ref[...] = jnp.where(lane < ROT_HALF, -s, jnp.where(lane < ROT_DIM, s, 0.0))


def _rope_tables(positions, tm):
    n = positions.size
    pos = positions.reshape(n, 1)
    inv_freq = jnp.power(ROPE_THETA, -jnp.arange(0, ROT_DIM, 2, dtype=F32) / ROT_DIM)
    freq = jnp.tile(jnp.concatenate([inv_freq, inv_freq, jnp.zeros((HEAD_DIM - ROT_DIM,), F32)]), 2)[None, :]
    return pl.pallas_call(
        _rope_table_kernel,
        grid=(n // tm,),
        in_specs=[pl.BlockSpec((tm, 1), lambda i: (i, 0)), pl.BlockSpec((1, LANES), lambda i: (0, 0))],
        out_specs=[pl.BlockSpec((tm, LANES), lambda i: (i, 0))] * 2,
        out_shape=[jax.ShapeDtypeStruct((n, LANES), F32)] * 2,
        compiler_params=_params("parallel"),
        name="rope_tables",
    )(pos, freq)


def _rmsnorm_rows(x, g):
    return x * lax.rsqrt(jnp.mean(x * x, axis=-1, keepdims=True) + EPS) * g


def _ffn_kernel(x_ref, g_ref, wg_ref, wu_ref, wd_ref, o_ref, *, chunk):
    x = x_ref[...]
    h = _rmsnorm_rows(x, g_ref[...]).astype(BF16)
    acc = jnp.zeros(x.shape, F32)
    for c in range(0, wg_ref.shape[1], chunk):
        a = _dot(h, wg_ref[:, c:c + chunk])
        b = _dot(h, wu_ref[:, c:c + chunk])
        t = (a * jax.nn.sigmoid(a) * b).astype(BF16)
        acc = acc + _dot(t, wd_ref[c:c + chunk, :])
    o_ref[...] = x + 0.5 * acc


def _ffn(x, g, wg, wu, wd, tm):
    n, d = x.shape
    f = wg.shape[1]
    return pl.pallas_call(
        functools.partial(_ffn_kernel, chunk=MXU_DIM),
        grid=(n // tm,),
        in_specs=[pl.BlockSpec((tm, d), lambda i: (i, 0)), _resident((1, d)),
                  _resident((d, f)), _resident((d, f)), _resident((f, d))],
        out_specs=pl.BlockSpec((tm, d), lambda i: (i, 0)),
        out_shape=jax.ShapeDtypeStruct((n, d), F32),
        compiler_params=_params("parallel"),
        name="ffn",
    )(x, g, wg, wu, wd)


_PROJ_COLUMNS = (
    ("qa", 256, "norm_rope_scale", 0),
    ("shared", 256, "shared", 1),
    ("qi", 512, "rope", None),
    ("wi", 128, "index_weight", None),
    ("qb", 256, "norm_rope_scale", 2),
    ("kb", 256, "norm_rope", 3),
    ("vb", 256, "plain", None),
    ("qc", 512, "norm_rope_scale", 4),
    ("kc", 512, "norm_rope", 5),
    ("vc", 512, "plain", None),
    ("gate", 3072, "sigmoid", None),
)
_PROJ_OUTPUTS = (
    ("qa", 256, BF16), ("ka", 256, BF16), ("va", 128, BF16), ("qi", 512, BF16), ("ki", 256, BF16), ("wi", 128, F32),
    ("qb", 256, BF16), ("kb", 256, BF16), ("vb", 256, BF16), ("qc", 512, BF16), ("kc", 512, BF16), ("vc", 512, BF16),
    ("gate", 3072, BF16),
)
_Q_SCALE = HEAD_DIM ** -0.5 * LOG2E


def _proj_kernel(x_ref, g_ref, qkg_ref, cos_ref, sin_ref, w_ref, *o_refs):
    out = {name: ref for (name, _, _), ref in zip(_PROJ_OUTPUTS, o_refs)}
    tm = x_ref.shape[0]
    h = _rmsnorm_rows(x_ref[...], g_ref[...]).astype(BF16)
    cos = cos_ref[...]
    sin = sin_ref[...]
    lane = lax.broadcasted_iota(jnp.int32, (tm, LANES), 1)
    take_upper = (lane % HEAD_DIM) < ROT_HALF
    low_half = lane < HEAD_DIM
    r = lax.broadcasted_iota(jnp.int32, (LANES, LANES), 0) // HEAD_DIM
    c = lax.broadcasted_iota(jnp.int32, (LANES, LANES), 1) // HEAD_DIM
    same_head = (r == c).astype(BF16)

    def epilogue(y, kind, g_idx):
        if kind in ("norm_rope_scale", "norm_rope"):
            ssq = _dot((y * y).astype(BF16), same_head)
            y = y * lax.rsqrt(ssq * (1.0 / HEAD_DIM) + EPS) * qkg_ref[g_idx:g_idx + 1, :]
        if kind in ("norm_rope_scale", "norm_rope", "rope"):
            partner = jnp.where(take_upper, pltpu.roll(y, LANES - ROT_HALF, 1), pltpu.roll(y, ROT_HALF, 1))
            y = y * cos + partner * sin
        if kind == "norm_rope_scale":
            y = y * _Q_SCALE
        if kind == "index_weight":
            y = y * ((IDX_HEADS * IDX_DIM) ** -0.5)
        if kind == "sigmoid":
            y = jax.nn.sigmoid(y)
        return y

    def on_both_halves(y):
        return jnp.where(low_half, y, pltpu.roll(y, HEAD_DIM, 1))

    col = 0
    for name, width, kind, g_idx in _PROJ_COLUMNS:
        step = min(width, MXU_DIM)
        for c0 in range(0, width, step):
            y = _dot(h, w_ref[:, col + c0:col + c0 + step])
            if kind == "shared":
                key_value, index_key = y[:, :LANES], y[:, LANES:]
                ka = on_both_halves(epilogue(key_value, "norm_rope", g_idx)).astype(BF16)
                ki = on_both_halves(epilogue(index_key, "rope", None)).astype(BF16)
                for s0 in range(0, out["ka"].shape[1], LANES):
                    out["ka"][:, s0:s0 + LANES] = ka
                    out["ki"][:, s0:s0 + LANES] = ki
                out["va"][...] = jnp.where(low_half, pltpu.roll(key_value, HEAD_DIM, 1), 1.0).astype(BF16)
                continue
            o_ref = out[name]
            for s0 in range(0, step, LANES):
                o_ref[:, c0 + s0:c0 + s0 + LANES] = epilogue(y[:, s0:s0 + LANES], kind, g_idx).astype(o_ref.dtype)
        col += width


def _proj_weights(w_in):
    splits = (256, 64, 64, 512, 64, 8, 256, 256, 256, 512, 512, 512, 1024, 1024, 1024)
    offs = [0]
    for s in splits:
        offs.append(offs[-1] + s)
    qa, ka, va, qi, ki, wi, qb, kb, vb, qc, kc, vc, ga, gb, gc = [w_in[:, offs[i]:offs[i + 1]] for i in range(len(splits))]
    zeros = lambda n: jnp.zeros((w_in.shape[0], n), w_in.dtype)
    groups = dict(qa=qa, shared=jnp.concatenate([ka, va, ki, zeros(HEAD_DIM)], axis=1), qi=qi,
                  wi=jnp.concatenate([wi, zeros(LANES - IDX_HEADS)], axis=1),
                  qb=qb, kb=kb, vb=vb, qc=qc, kc=kc, vc=vc, gate=jnp.concatenate([ga, gb, gc], axis=1))
    return jnp.concatenate([groups[name] for name, *_ in _PROJ_COLUMNS], axis=1).astype(BF16)


def _proj(x, g, qkg, cos, sin, w, tm):
    n, d = x.shape
    row = lambda width: pl.BlockSpec((tm, width), lambda i: (i, 0))
    return pl.pallas_call(
        _proj_kernel,
        grid=(n // tm,),
        in_specs=[row(d), _resident((1, d)), _resident(qkg.shape), row(LANES), row(LANES), _resident(w.shape)],
        out_specs=[row(width) for _, width, _ in _PROJ_OUTPUTS],
        out_shape=[jax.ShapeDtypeStruct((n, width), dtype) for _, width, dtype in _PROJ_OUTPUTS],
        compiler_params=_params("parallel"),
        name="in_proj",
    )(x, g, qkg, cos, sin, w)


def _lane_fold(x, op):
    out = x[:, :LANES]
    for c in range(LANES, x.shape[1], LANES):
        out = op(out, x[:, c:c + LANES])
    return out


def _lane_tile(x, width):
    return jnp.concatenate([x] * (width // LANES), axis=1)


def _reset_stats(max_ref, acc_ref):
    max_ref[...] = jnp.full(max_ref.shape, MASKED, F32)
    acc_ref[...] = jnp.zeros(acc_ref.shape, F32)


def _record_scores(s, j, r0, s_ref, max_ref):
    n = s.shape[0]
    s_ref[j, r0:r0 + n, :] = s
    max_ref[r0:r0 + n, :] = jnp.maximum(max_ref[r0:r0 + n, :], _lane_fold(s, jnp.maximum))


def _finish_max(max_ref):
    m = max_ref[...]
    max_ref[...] = jnp.broadcast_to(jnp.max(m, axis=-1, keepdims=True), m.shape)


def _accumulate_probs(j, r0, n, v, s_ref, max_ref, acc_ref):
    s = s_ref[j, r0:r0 + n, :]
    p = jnp.exp2(s - _lane_tile(max_ref[r0:r0 + n, :], s.shape[1]))
    acc_ref[r0:r0 + n, :] += _dot(p.astype(BF16), v)


def _with_ones(v):
    return jnp.concatenate([v, jnp.ones(v.shape, v.dtype)], axis=1)


def _normalised(r0, n, acc_ref):
    return acc_ref[r0:r0 + n, :LANES] / acc_ref[r0:r0 + n, LANES:]


def _head_lane_masks(rows, width):
    lane = lax.broadcasted_iota(jnp.int32, (rows, width), 1)
    return [(lane >= h * HEAD_DIM) & (lane < (h + 1) * HEAD_DIM) for h in range(width // HEAD_DIM)]


def _stack_masked(q, masks):
    zero = jnp.zeros_like(q)
    return jnp.concatenate([jnp.where(m, q, zero) for m in masks], axis=0)


def _local_causal(t):
    row = lax.broadcasted_iota(jnp.int32, (t, t), 0)
    col = lax.broadcasted_iota(jnp.int32, (t, t), 1)
    return col <= row


def _tile_rows(j, t):
    return pl.ds(pl.multiple_of(j * t, t), t)


def _for_each_tile(n, body, group=2):
    @pl.loop(0, n // group)
    def _(i):
        for k in range(group):
            body(group * i + k)

    part = group // 2
    while part >= 1:
        @pl.when(n % (2 * part) >= part)
        def _():
            for k in range(part):
                body(n - n % (2 * part) + k)
        part //= 2


def _interleave_pairs(o, t):
    lane = lax.broadcasted_iota(jnp.int32, (t, LANES), 1)
    low = lane < HEAD_DIM
    return jnp.concatenate([jnp.where(low, o[0], o[1]), jnp.where(low, o[2], o[3])], axis=1)


_ATT_SCRATCH = lambda n_tiles, rows, t, acc_width: [
    pltpu.VMEM((n_tiles, rows, t), F32),
    pltpu.VMEM((rows, LANES), F32),
    pltpu.VMEM((rows, acc_width), F32),
]


def _diff_kernel(lam_ref, g_ref, q_ref, k_ref, v_ref, o_ref, s_ref, max_ref, acc_ref, *, lam_init, t):
    it = pl.program_id(1)
    lane = lax.broadcasted_iota(jnp.int32, (t, C_VDIM), 1)
    maps = [lane < HEAD_DIM, lane >= HEAD_DIM]
    q_all = q_ref[0]
    qs = [_stack_masked(q_all[:, h * C_VDIM:(h + 1) * C_VDIM], maps) for h in range(C_HEADS)]
    _reset_stats(max_ref, acc_ref)
    causal = jnp.where(_local_causal(t), 0.0, MASKED)
    causal2 = jnp.concatenate([causal, causal], axis=0)

    def scores(j):
        rows = _tile_rows(j, t)
        bias = jnp.where(j == it, causal2, 0.0)
        for h in range(C_HEADS):
            s = _nt_dot(qs[h], k_ref[0, rows, h * C_VDIM:(h + 1) * C_VDIM]) + bias
            _record_scores(s, j, h * 2 * t, s_ref, max_ref)

    _for_each_tile(it + 1, scores, group=4)
    _finish_max(max_ref)

    def probs(j):
        rows = _tile_rows(j, t)
        for h in range(C_HEADS):
            _accumulate_probs(j, h * 2 * t, 2 * t, _with_ones(v_ref[0, rows, h * C_VDIM:(h + 1) * C_VDIM]),
                              s_ref, max_ref, acc_ref)

    _for_each_tile(it + 1, probs, group=4)

    lp = lam_ref[...]
    lam = (jnp.exp(jnp.sum(lp[0:1] * lp[1:2], axis=-1, keepdims=True))
           - jnp.exp(jnp.sum(lp[2:3] * lp[3:4], axis=-1, keepdims=True)) + lam_init)
    for h in range(C_HEADS):
        o = _normalised(h * 2 * t, t, acc_ref) - lam * _normalised(h * 2 * t + t, t, acc_ref)
        o = o * lax.rsqrt(jnp.mean(o * o, axis=-1, keepdims=True) + EPS) * g_ref[...] * (1.0 - lam_init)
        o_ref[0, :, h * C_VDIM:(h + 1) * C_VDIM] = o.astype(o_ref.dtype)


def _diff_attention(qc, kc, vc, lam_p, subln_g, lam_init):
    b, s, w = qc.shape
    t = ATT_TILE
    qspec = pl.BlockSpec((1, t, w), lambda bi, i: (bi, i, 0))
    kvspec = pl.BlockSpec((1, s, w), lambda bi, i: (bi, 0, 0))
    return pl.pallas_call(
        functools.partial(_diff_kernel, lam_init=lam_init, t=t),
        grid=(b, s // t),
        in_specs=[pl.BlockSpec(lam_p.shape, lambda bi, i: (0, 0)),
                  pl.BlockSpec((1, C_VDIM), lambda bi, i: (0, 0)), qspec, kvspec, kvspec],
        out_specs=qspec,
        out_shape=jax.ShapeDtypeStruct((b, s, w), BF16),
        scratch_shapes=_ATT_SCRATCH(s // t, 2 * C_HEADS * t, t, 2 * LANES),
        compiler_params=_params("parallel", "arbitrary"),
        name="diff_attention",
    )(lam_p, subln_g, qc, kc, vc)


def _moba_kernel(q_ref, k_ref, v_ref, o_ref, kmean_hi, kmean_lo, bias_ref, s_ref, max_ref, acc_ref, *, t):
    it = pl.program_id(1)
    n_blk = k_ref.shape[1] // t
    width = q_ref.shape[2]

    @pl.when(it == 0)
    def _():
        kmean_hi[...] = jnp.zeros_like(kmean_hi)
        kmean_lo[...] = jnp.zeros_like(kmean_lo)
        for n in range(n_blk):
            km = jnp.mean(k_ref[0, n * t:(n + 1) * t, :].astype(F32), axis=0, keepdims=True)
            hi = km.astype(BF16)
            kmean_hi[n:n + 1, :] = hi
            kmean_lo[n:n + 1, :] = (km - hi.astype(F32)).astype(BF16)

    heads = _head_lane_masks(t, width)
    q_stack = _stack_masked(q_ref[0], heads)

    no_bias = jnp.zeros(bias_ref.shape[1:], F32)
    bias_ref[it] = no_bias

    @pl.when(it <= MOBA_TOPK)
    def _():
        for n in range(min(MOBA_TOPK, n_blk)):
            @pl.when(n < it)
            def _():
                bias_ref[n] = no_bias

    @pl.when(it > MOBA_TOPK)
    def _():
        blk = lax.broadcasted_iota(jnp.int32, (SUBLANES, t), 0)
        past = blk < it
        pad = jnp.full((LANES - SUBLANES, t), MASKED, F32)
        bias_q = []
        for h in range(B_HEADS):
            qh = q_stack[h * t:(h + 1) * t]
            gate = (_nt_dot(kmean_hi[...], qh) + _nt_dot(kmean_lo[...], qh))[:SUBLANES]
            gate = jnp.where(past, gate, -jnp.inf)
            rank = jnp.zeros((SUBLANES, t), jnp.int32)
            for n in range(n_blk):
                gn = gate[n:n + 1, :]
                rank = rank + jnp.where((gn > gate) | ((gn == gate) & (blk > n)), 1, 0)
            bias = jnp.where((rank < MOBA_TOPK) & past, 0.0, MASKED)
            bias_q.append(jnp.concatenate([bias, pad], axis=0).T.astype(BF16))
        bias_q = jnp.concatenate(bias_q, axis=0)
        lane_of = lax.broadcasted_iota(jnp.int32, (LANES, LANES), 0)
        for n in range(n_blk - 1):
            @pl.when(n < it)
            def _():
                pick = (lane_of == n).astype(BF16)
                bias_ref[n] = _dot(bias_q, pick)

    _reset_stats(max_ref, acc_ref)
    causal = jnp.where(_local_causal(t), 0.0, MASKED)
    causal4 = jnp.concatenate([causal] * B_HEADS, axis=0)

    def scores(j):
        s = _nt_dot(q_stack, k_ref[0, _tile_rows(j, t), :]) + _lane_tile(bias_ref[j], t)
        _record_scores(s + jnp.where(j == it, causal4, 0.0), j, 0, s_ref, max_ref)

    _for_each_tile(it + 1, scores, group=4)
    _finish_max(max_ref)

    def probs(j):
        rows = _tile_rows(j, t)
        for pair in range(B_HEADS // 2):
            _accumulate_probs(j, pair * 2 * t, 2 * t, _with_ones(v_ref[0, rows, pair * LANES:(pair + 1) * LANES]),
                              s_ref, max_ref, acc_ref)

    _for_each_tile(it + 1, probs, group=4)

    o = [_normalised(h * t, t, acc_ref) for h in range(B_HEADS)]
    o_ref[0] = _interleave_pairs(o, t).astype(o_ref.dtype)


def _moba_attention(qb, kb, vb):
    b, s, w = qb.shape
    t = MOBA_BLOCK
    assert s // t <= SUBLANES
    qspec = pl.BlockSpec((1, t, w), lambda bi, i: (bi, i, 0))
    kvspec = pl.BlockSpec((1, s, w), lambda bi, i: (bi, 0, 0))
    return pl.pallas_call(
        functools.partial(_moba_kernel, t=t),
        grid=(b, s // t),
        in_specs=[qspec, kvspec, kvspec],
        out_specs=qspec,
        out_shape=jax.ShapeDtypeStruct((b, s, w), BF16),
        scratch_shapes=[pltpu.VMEM((LANES, w), BF16), pltpu.VMEM((LANES, w), BF16),
                        pltpu.VMEM((s // t, B_HEADS * t, LANES), F32)]
                       + _ATT_SCRATCH(s // t, B_HEADS * t, t, 2 * LANES),
        compiler_params=_params("parallel", "arbitrary"),
        name="moba_attention",
    )(qb, kb, vb)


def _sublane_fold(x, rows):
    parts = [x[r:r + rows] for r in range(0, x.shape[0], rows)]
    while len(parts) > 1:
        parts = [parts[i] + parts[i + 1] for i in range(0, len(parts), 2)]
    return parts[0]


I16 = jnp.int16
I16_ROWS = 2 * SUBLANES
I16_MIN = -(2 ** 15)


def _dsa_kernel(q_ref, k_ref, v_ref, qi_ref, ki_ref, wi_ref, o_ref,
                keys_ref, hi_ref, lo_ref, ties_ref, s_ref, max_ref, acc_ref, *, t, topk):
    it = pl.program_id(1)
    width = q_ref.shape[2]
    heads = _head_lane_masks(t, width)
    causal = _local_causal(t)

    qidx = qi_ref[0]
    qi_stack = jnp.concatenate(
        [_stack_masked(qidx[:, g * width:(g + 1) * width], heads) for g in range(IDX_HEADS // 4)], axis=0)
    wi = wi_ref[0]
    w_rep = [jnp.broadcast_to(wi[:, h:h + 1], (t, LANES)) for h in range(IDX_HEADS)]

    def score_tile(j):
        logit = _nt_dot(qi_stack, ki_ref[0, _tile_rows(j, t), :])
        cols = []
        for c in range(0, t, LANES):
            acc = jnp.zeros((t, LANES), F32)
            for h in range(IDX_HEADS):
                acc = acc + jnp.maximum(logit[h * t:(h + 1) * t, c:c + LANES], 0.0) * w_rep[h]
            cols.append(acc)
        isc = jnp.concatenate(cols, axis=1) + 0.0
        isc = jnp.where(j == it, jnp.where(causal, isc, -jnp.inf), isc)
        bits = lax.bitcast_convert_type(isc, jnp.int32)
        key = bits ^ ((bits >> 31) & 0x7FFFFFFF)
        keys_ref[j] = key
        key_t = key.T
        hi_ref[j] = (key_t >> 16).astype(I16)
        lo_ref[j] = ((key_t & 0xFFFF) + I16_MIN).astype(I16)

    _for_each_tile(it + 1, score_tile, group=4)
    n_pairs = it // 2 + 1

    @pl.when(it % 2 == 0)
    def _():
        hi_ref[it + 1] = jnp.full((t, t), I16_MIN, I16)
        lo_ref[it + 1] = jnp.full((t, t), I16_MIN, I16)

    def count(ref, pred):
        def one(j, cnt):
            return cnt + _sublane_fold(jnp.where(pred(ref[j]), jnp.int16(1), jnp.int16(0)), I16_ROWS)
        cnt = lax.fori_loop(0, n_pairs, lambda i, cnt: one(2 * i + 1, one(2 * i, cnt)), jnp.zeros((I16_ROWS, t), I16))
        return jnp.sum(cnt.astype(jnp.int32), axis=0, keepdims=True)

    def kth_largest(ref, k):
        thr = jnp.where(count(ref, lambda x: x >= jnp.int16(0)) >= k, 0, jnp.full((1, t), I16_MIN, jnp.int32))

        def bit_step(i, thr):
            cand = thr | (1 << (14 - i))
            return jnp.where(count(ref, lambda x: x >= cand.astype(I16)) >= k, cand, thr)

        return lax.fori_loop(0, 15, bit_step, thr)

    thr_hi = kth_largest(hi_ref, topk)
    thr_hi16 = thr_hi.astype(I16)
    k_lo = topk - count(hi_ref, lambda x: x > thr_hi16)

    def keep_matching(j):
        lo_ref[j] = jnp.where(hi_ref[j] == thr_hi16, lo_ref[j], jnp.int16(I16_MIN))

    @pl.loop(0, n_pairs)
    def _(i):
        keep_matching(2 * i)
        keep_matching(2 * i + 1)

    thr_lo = kth_largest(lo_ref, k_lo)
    thr_lo16 = thr_lo.astype(I16)
    wanted = (k_lo - count(lo_ref, lambda x: x > thr_lo16)).astype(F32)
    thr = (thr_hi << 16) | (thr_lo - I16_MIN)
    thr_q = _lane_tile(jnp.broadcast_to(thr, (LANES, t)).T, t)
    wanted_q = _lane_tile(jnp.broadcast_to(wanted, (LANES, t)).T, t)

    q_stack = _stack_masked(q_ref[0], heads)
    r = lax.broadcasted_iota(jnp.int32, (t, t), 0)
    c = lax.broadcasted_iota(jnp.int32, (t, t), 1)
    before = (r < c).astype(BF16)
    ones = jnp.ones((t, LANES), BF16)
    _reset_stats(max_ref, acc_ref)

    ties_ref[...] = jnp.zeros(ties_ref.shape, F32)

    not_causal = jnp.where(causal, 0.0, MASKED)

    def select_tile(j):
        key = keys_ref[j]
        tie = key == thr_q
        tie_b = jnp.where(tie, 1.0, 0.0).astype(BF16)
        tie_rank = _lane_tile(ties_ref[...], t) + _dot(tie_b, before)
        bias = jnp.where(key > thr_q, 0.0, jnp.where(tie, jnp.where(tie_rank < wanted_q, 0.0, MASKED), MASKED))
        bias = bias + jnp.where(j == it, not_causal, 0.0)
        s = _nt_dot(q_stack, k_ref[0, _tile_rows(j, t), :]) + jnp.concatenate([bias] * A_HEADS, axis=0)
        _record_scores(s, j, 0, s_ref, max_ref)
        ties_ref[...] += _dot(tie_b, ones)

    _for_each_tile(it + 1, select_tile)
    _finish_max(max_ref)

    _for_each_tile(it + 1, lambda j: _accumulate_probs(
        j, 0, A_HEADS * t, v_ref[0, _tile_rows(j, t), :], s_ref, max_ref, acc_ref), group=4)

    sums_only = lax.broadcasted_iota(jnp.int32, (t, LANES), 1) >= HEAD_DIM
    o = []
    for h in range(A_HEADS):
        a = acc_ref[h * t:(h + 1) * t, :]
        inv = 1.0 / jnp.where(sums_only, a, 1.0)
        o.append(pltpu.roll(a, HEAD_DIM, 1) * inv if h % 2 else a * pltpu.roll(inv, HEAD_DIM, 1))
    o_ref[0] = _interleave_pairs(o, t).astype(o_ref.dtype)


def _dsa_attention(qa, ka, va, qi, ki, wi):
    b, s, w = qa.shape
    t = ATT_TILE
    topk = min(INDEX_TOPK, s // 4)
    assert t >= topk
    row = lambda width: pl.BlockSpec((1, t, width), lambda bi, i: (bi, i, 0))
    full = lambda width: pl.BlockSpec((1, s, width), lambda bi, i: (bi, 0, 0))
    return pl.pallas_call(
        functools.partial(_dsa_kernel, t=t, topk=topk),
        grid=(b, s // t),
        in_specs=[row(w), full(w), full(va.shape[2]), row(qi.shape[2]), full(w), row(LANES)],
        out_specs=row(w),
        out_shape=jax.ShapeDtypeStruct((b, s, w), BF16),
        scratch_shapes=[pltpu.VMEM((s // t, t, t), jnp.int32),
                        pltpu.VMEM((s // t + s // t % 2, t, t), I16), pltpu.VMEM((s // t + s // t % 2, t, t), I16),
                        pltpu.VMEM((t, LANES), F32)]
                       + _ATT_SCRATCH(s // t, A_HEADS * t, t, LANES),
        compiler_params=_params("parallel", "arbitrary"),
        name="dsa_attention",
    )(qa, ka, va, qi, ki, wi)


def _merge_kernel(x_ref, oa_ref, ob_ref, oc_ref, gate_ref, wb_ref, wo_ref, o_ref):
    d = x_ref.shape[1]
    wa, wbw = oa_ref.shape[1], ob_ref.shape[1]
    merged = gate_ref[:, 0:d].astype(F32) * _dot(oa_ref[...], wb_ref[0:wa, :])
    merged = merged + gate_ref[:, d:2 * d].astype(F32) * _dot(ob_ref[...], wb_ref[wa:wa + wbw, :])
    merged = merged + gate_ref[:, 2 * d:3 * d].astype(F32) * _dot(oc_ref[...], wb_ref[wa + wbw:, :])
    o_ref[...] = x_ref[...] + _dot(merged.astype(BF16), wo_ref[...])


def _merge(x, oa, ob, oc, gate, wb, wo, tm):
    n, d = x.shape
    row = lambda w: pl.BlockSpec((tm, w), lambda i: (i, 0))
    return pl.pallas_call(
        _merge_kernel,
        grid=(n // tm,),
        in_specs=[row(d), row(oa.shape[1]), row(ob.shape[1]), row(oc.shape[1]), row(gate.shape[1]),
                  _resident(wb.shape), _resident(wo.shape)],
        out_specs=row(d),
        out_shape=jax.ShapeDtypeStruct((n, d), F32),
        compiler_params=_params("parallel"),
        name="merge_out",
    )(x, oa, ob, oc, gate, wb, wo)


def kernel(x, positions, norm_g, w_in, qk_norm_g, lambda_params, diff_subln_g, w_branch, w_out,
           ffn_w_gate, ffn_w_up, ffn_w_down):
    b, s, d = x.shape
    n = b * s
    depth = norm_g.shape[0]
    tm = min(512, n)
    tm_ffn = min(1024, n)
    assert n % tm_ffn == 0 and s % ATT_TILE == 0 and ATT_TILE == MOBA_BLOCK

    cos, sin = _rope_tables(positions, min(4 * tm, n))
    xf = x.reshape(n, d)
    for layer in range(depth):
        lam_init = 0.8 - 0.6 * math.exp(-0.3 * layer)
        ffn_w = lambda i: (ffn_w_gate[layer, i].astype(BF16), ffn_w_up[layer, i].astype(BF16),
                           ffn_w_down[layer, i].astype(BF16))
        xf = _ffn(xf, norm_g[layer, 0][None, :], *ffn_w(0), tm_ffn)

        qkg = jnp.tile(qk_norm_g[layer], (1, LANES // HEAD_DIM))
        outs = _proj(xf, norm_g[layer, 1][None, :], qkg, cos, sin, _proj_weights(w_in[layer]), tm)
        p = {name: o.reshape(b, s, o.shape[1]) for (name, *_), o in zip(_PROJ_OUTPUTS, outs)}
        o_a = _dsa_attention(p["qa"], p["ka"], p["va"], p["qi"], p["ki"], p["wi"])
        o_b = _moba_attention(p["qb"], p["kb"], p["vb"])
        o_c = _diff_attention(p["qc"], p["kc"], p["vc"], lambda_params[layer], diff_subln_g[layer][None, :], lam_init)
        xf = _merge(xf, o_a.reshape(n, -1), o_b.reshape(n, -1), o_c.reshape(n, -1), p["gate"].reshape(n, -1),
                    w_branch[layer].astype(BF16), w_out[layer].astype(BF16), tm_ffn)

        xf = _ffn(xf, norm_g[layer, 2][None, :], *ffn_w(1), tm_ffn)
    return xf.reshape(b, s, d)
```

```python
import functools
import math

import jax
import jax.numpy as jnp
from jax import lax
from jax.experimental import pallas as pl
from jax.experimental.pallas import tpu as pltpu

F32 = jnp.float32
BF16 = jnp.bfloat16

HEAD_DIM = 64
ROT_DIM = HEAD_DIM // 4
ROT_HALF = ROT_DIM // 2
ROPE_THETA = 500000.0
EPS = 1e-6
A_HEADS = 4
IDX_HEADS = 8
IDX_DIM = 64
INDEX_TOPK = 256
B_HEADS = 4
MOBA_BLOCK = 256
MOBA_TOPK = 3
C_HEADS = 4
C_VDIM = 2 * HEAD_DIM

LANES = 128
SUBLANES = 8
MXU_DIM = 256
ATT_TILE = 256
MASKED = -1e30
VMEM_LIMIT = 56 * 1024 * 1024
LOG2E = 1.4426950408889634


def _nt_dot(a, b):
    return lax.dot_general(a, b, (((1,), (1,)), ((), ())), preferred_element_type=F32)


def _dot(a, b):
    return jnp.dot(a, b, preferred_element_type=F32)


def _resident(shape):
    nd = len(shape)
    return pl.BlockSpec(shape, lambda *_: (0,) * nd, pipeline_mode=pl.Buffered(1))


def _params(*sem):
    return pltpu.CompilerParams(dimension_semantics=sem, vmem_limit_bytes=VMEM_LIMIT)


def _rope_table_kernel(pos_ref, freq_ref, cos_ref, sin_ref):
    ang = pos_ref[...].astype(F32) * freq_ref[...]
    lane = lax.broadcasted_iota(jnp.int32, ang.shape, 1) % HEAD_DIM
    c = jnp.cos(ang)
    s = jnp.sin(ang)
    cos_ref[...] = jnp.where(lane < ROT_DIM, c, 1.0)
    sin_ref[...] = jnp.where(lane < ROT_HALF, -s, jnp.where(lane < ROT_DIM, s, 0.0))


def _rope_tables(positions, tm):
    n = positions.size
    pos = positions.reshape(n, 1)
    inv_freq = jnp.power(ROPE_THETA, -jnp.arange(0, ROT_DIM, 2, dtype=F32) / ROT_DIM)
    freq = jnp.tile(jnp.concatenate([inv_freq, inv_freq, jnp.zeros((HEAD_DIM - ROT_DIM,), F32)]), 2)[None, :]
    return pl.pallas_call(
        _rope_table_kernel,
        grid=(n // tm,),
        in_specs=[pl.BlockSpec((tm, 1), lambda i: (i, 0)), pl.BlockSpec((1, LANES), lambda i: (0, 0))],
        out_specs=[pl.BlockSpec((tm, LANES), lambda i: (i, 0))] * 2,
        out_shape=[jax.ShapeDtypeStruct((n, LANES), F32)] * 2,
        compiler_params=_params("parallel"),
        name="rope_tables",
    )(pos, freq)


def _rmsnorm_rows(x, g):
    return x * lax.rsqrt(jnp.mean(x * x, axis=-1, keepdims=True) + EPS) * g


def _ffn_kernel(x_ref, g_ref, wg_ref, wu_ref, wd_ref, o_ref, *, chunk):
    x = x_ref[...]
    h = _rmsnorm_rows(x, g_ref[...]).astype(BF16)
    acc = jnp.zeros(x.shape, F32)
    for c in range(0, wg_ref.shape[1], chunk):
        a = _dot(h, wg_ref[:, c:c + chunk])
        b = _dot(h, wu_ref[:, c:c + chunk])
        t = (a * jax.nn.sigmoid(a) * b).astype(BF16)
        acc = acc + _dot(t, wd_ref[c:c + chunk, :])
    o_ref[...] = x + 0.5 * acc


def _ffn(x, g, wg, wu, wd, tm):
    n, d = x.shape
    f = wg.shape[1]
    return pl.pallas_call(
        functools.partial(_ffn_kernel, chunk=MXU_DIM),
        grid=(n // tm,),
        in_specs=[pl.BlockSpec((tm, d), lambda i: (i, 0)), _resident((1, d)),
                  _resident((d, f)), _resident((d, f)), _resident((f, d))],
        out_specs=pl.BlockSpec((tm, d), lambda i: (i, 0)),
        out_shape=jax.ShapeDtypeStruct((n, d), F32),
        compiler_params=_params("parallel"),
        name="ffn",
    )(x, g, wg, wu, wd)


_PROJ_COLUMNS = (
    ("qa", 256, "norm_rope_scale", 0),
    ("shared", 256, "shared", 1),
    ("qi", 512, "rope", None),
    ("wi", 128, "index_weight", None),
    ("qb", 256, "norm_rope_scale", 2),
    ("kb", 256, "norm_rope", 3),
    ("vb", 256, "plain", None),
    ("qc", 512, "norm_rope_scale", 4),
    ("kc", 512, "norm_rope", 5),
    ("vc", 512, "plain", None),
    ("gate", 3072, "sigmoid", None),
)
_PROJ_OUTPUTS = (
    ("qa", 256, BF16), ("ka", 256, BF16), ("va", 128, BF16), ("qi", 512, BF16), ("ki", 256, BF16), ("wi", 128, F32),
    ("qb", 256, BF16), ("kb", 256, BF16), ("vb", 256, BF16), ("qc", 512, BF16), ("kc", 512, BF16), ("vc", 512, BF16),
    ("gate", 3072, BF16),
)
_Q_SCALE = HEAD_DIM ** -0.5 * LOG2E


def _proj_kernel(x_ref, g_ref, qkg_ref, cos_ref, sin_ref, w_ref, *o_refs):
    out = {name: ref for (name, _, _), ref in zip(_PROJ_OUTPUTS, o_refs)}
    tm = x_ref.shape[0]
    h = _rmsnorm_rows(x_ref[...], g_ref[...]).astype(BF16)
    cos = cos_ref[...]
    sin = sin_ref[...]
    lane = lax.broadcasted_iota(jnp.int32, (tm, LANES), 1)
    take_upper = (lane % HEAD_DIM) < ROT_HALF
    low_half = lane < HEAD_DIM
    r = lax.broadcasted_iota(jnp.int32, (LANES, LANES), 0) // HEAD_DIM
    c = lax.broadcasted_iota(jnp.int32, (LANES, LANES), 1) // HEAD_DIM
    same_head = (r == c).astype(BF16)

    def epilogue(y, kind, g_idx):
        if kind in ("norm_rope_scale", "norm_rope"):
            ssq = _dot((y * y).astype(BF16), same_head)
            y = y * lax.rsqrt(ssq * (1.0 / HEAD_DIM) + EPS) * qkg_ref[g_idx:g_idx + 1, :]
        if kind in ("norm_rope_scale", "norm_rope", "rope"):
            partner = jnp.where(take_upper, pltpu.roll(y, LANES - ROT_HALF, 1), pltpu.roll(y, ROT_HALF, 1))
            y = y * cos + partner * sin
        if kind == "norm_rope_scale":
            y = y * _Q_SCALE
        if kind == "index_weight":
            y = y * ((IDX_HEADS * IDX_DIM) ** -0.5)
        if kind == "sigmoid":
            y = jax.nn.sigmoid(y)
        return y

    def on_both_halves(y):
        return jnp.where(low_half, y, pltpu.roll(y, HEAD_DIM, 1))

    col = 0
    for name, width, kind, g_idx in _PROJ_COLUMNS:
        step = min(width, MXU_DIM)
        for c0 in range(0, width, step):
            y = _dot(h, w_ref[:, col + c0:col + c0 + step])
            if kind == "shared":
                key_value, index_key = y[:, :LANES], y[:, LANES:]
                ka = on_both_halves(epilogue(key_value, "norm_rope", g_idx)).astype(BF16)
                ki = on_both_halves(epilogue(index_key, "rope", None)).astype(BF16)
                for s0 in range(0, out["ka"].shape[1], LANES):
                    out["ka"][:, s0:s0 + LANES] = ka
                    out["ki"][:, s0:s0 + LANES] = ki
                out["va"][...] = jnp.where(low_half, pltpu.roll(key_value, HEAD_DIM, 1), 1.0).astype(BF16)
                continue
            o_ref = out[name]
            for s0 in range(0, step, LANES):
                o_ref[:, c0 + s0:c0 + s0 + LANES] = epilogue(y[:, s0:s0 + LANES], kind, g_idx).astype(o_ref.dtype)
        col += width


def _proj_weights(w_in):
    w = w_in.astype(BF16)
    splits = (256, 64, 64, 512, 64, 8, 256, 256, 256, 512, 512, 512, 1024, 1024, 1024)
    offs = [0]
    for s in splits:
        offs.append(offs[-1] + s)
    qa, ka, va, qi, ki, wi, qb, kb, vb, qc, kc, vc, ga, gb, gc = [w[..., offs[i]:offs[i + 1]] for i in range(len(splits))]
    zeros = lambda n: jnp.zeros(w.shape[:-1] + (n,), w.dtype)
    groups = dict(qa=qa, shared=jnp.concatenate([ka, va, ki, zeros(HEAD_DIM)], axis=-1), qi=qi,
                  wi=jnp.concatenate([wi, zeros(LANES - IDX_HEADS)], axis=-1),
                  qb=qb, kb=kb, vb=vb, qc=qc, kc=kc, vc=vc, gate=jnp.concatenate([ga, gb, gc], axis=-1))
    return jnp.concatenate([groups[name] for name, *_ in _PROJ_COLUMNS], axis=-1)


def _proj(x, g, qkg, cos, sin, w, tm):
    n, d = x.shape
    row = lambda width: pl.BlockSpec((tm, width), lambda i: (i, 0))
    return pl.pallas_call(
        _proj_kernel,
        grid=(n // tm,),
        in_specs=[row(d), _resident((1, d)), _resident(qkg.shape), row(LANES), row(LANES), _resident(w.shape)],
        out_specs=[row(width) for _, width, _ in _PROJ_OUTPUTS],
        out_shape=[jax.ShapeDtypeStruct((n, width), dtype) for _, width, dtype in _PROJ_OUTPUTS],
        compiler_params=_params("parallel"),
        name="in_proj",
    )(x, g, qkg, cos, sin, w)


def _lane_fold(x, op):
    out = x[:, :LANES]
    for c in range(LANES, x.shape[1], LANES):
        out = op(out, x[:, c:c + LANES])
    return out


def _lane_tile(x, width):
    return jnp.concatenate([x] * (width // LANES), axis=1)


def _reset_stats(max_ref, acc_ref):
    max_ref[...] = jnp.full(max_ref.shape, MASKED, F32)
    acc_ref[...] = jnp.zeros(acc_ref.shape, F32)


def _record_scores(s, j, r0, s_ref, max_ref):
    n = s.shape[0]
    s_ref[j, r0:r0 + n, :] = s
    max_ref[r0:r0 + n, :] = jnp.maximum(max_ref[r0:r0 + n, :], _lane_fold(s, jnp.maximum))


def _finish_max(max_ref):
    m = max_ref[...]
    max_ref[...] = jnp.broadcast_to(jnp.max(m, axis=-1, keepdims=True), m.shape)


def _accumulate_probs(j, r0, n, v, s_ref, max_ref, acc_ref):
    s = s_ref[j, r0:r0 + n, :]
    p = jnp.exp2(s - _lane_tile(max_ref[r0:r0 + n, :], s.shape[1]))
    acc_ref[r0:r0 + n, :] += _dot(p.astype(BF16), v)


def _with_ones(v):
    return jnp.concatenate([v, jnp.ones(v.shape, v.dtype)], axis=1)


def _normalised(r0, n, acc_ref):
    return acc_ref[r0:r0 + n, :LANES] / acc_ref[r0:r0 + n, LANES:]


def _head_lane_masks(rows, width):
    lane = lax.broadcasted_iota(jnp.int32, (rows, width), 1)
    return [(lane >= h * HEAD_DIM) & (lane < (h + 1) * HEAD_DIM) for h in range(width // HEAD_DIM)]


def _stack_masked(q, masks):
    zero = jnp.zeros_like(q)
    return jnp.concatenate([jnp.where(m, q, zero) for m in masks], axis=0)


def _local_causal(t):
    row = lax.broadcasted_iota(jnp.int32, (t, t), 0)
    col = lax.broadcasted_iota(jnp.int32, (t, t), 1)
    return col <= row


def _tile_rows(j, t):
    return pl.ds(pl.multiple_of(j * t, t), t)


def _for_each_tile(n, body, group=2):
    @pl.loop(0, n // group)
    def _(i):
        for k in range(group):
            body(group * i + k)

    part = group // 2
    while part >= 1:
        @pl.when(n % (2 * part) >= part)
        def _():
            for k in range(part):
                body(n - n % (2 * part) + k)
        part //= 2


def _interleave_pairs(o, t):
    lane = lax.broadcasted_iota(jnp.int32, (t, LANES), 1)
    low = lane < HEAD_DIM
    return jnp.concatenate([jnp.where(low, o[0], o[1]), jnp.where(low, o[2], o[3])], axis=1)


_ATT_SCRATCH = lambda n_tiles, rows, t, acc_width: [
    pltpu.VMEM((n_tiles, rows, t), F32),
    pltpu.VMEM((rows, LANES), F32),
    pltpu.VMEM((rows, acc_width), F32),
]


def _diff_kernel(lam_ref, g_ref, q_ref, k_ref, v_ref, o_ref, s_ref, max_ref, acc_ref, *, lam_init, t):
    it = pl.program_id(1)
    lane = lax.broadcasted_iota(jnp.int32, (t, C_VDIM), 1)
    maps = [lane < HEAD_DIM, lane >= HEAD_DIM]
    q_all = q_ref[0]
    qs = [_stack_masked(q_all[:, h * C_VDIM:(h + 1) * C_VDIM], maps) for h in range(C_HEADS)]
    _reset_stats(max_ref, acc_ref)
    causal = jnp.where(_local_causal(t), 0.0, MASKED)
    causal2 = jnp.concatenate([causal, causal], axis=0)

    def scores(j):
        rows = _tile_rows(j, t)
        bias = jnp.where(j == it, causal2, 0.0)
        for h in range(C_HEADS):
            s = _nt_dot(qs[h], k_ref[0, rows, h * C_VDIM:(h + 1) * C_VDIM]) + bias
            _record_scores(s, j, h * 2 * t, s_ref, max_ref)

    _for_each_tile(it + 1, scores, group=4)
    _finish_max(max_ref)

    def probs(j):
        rows = _tile_rows(j, t)
        for h in range(C_HEADS):
            _accumulate_probs(j, h * 2 * t, 2 * t, _with_ones(v_ref[0, rows, h * C_VDIM:(h + 1) * C_VDIM]),
                              s_ref, max_ref, acc_ref)

    _for_each_tile(it + 1, probs, group=4)

    lp = lam_ref[...]
    lam = (jnp.exp(jnp.sum(lp[0:1] * lp[1:2], axis=-1, keepdims=True))
           - jnp.exp(jnp.sum(lp[2:3] * lp[3:4], axis=-1, keepdims=True)) + lam_init)
    for h in range(C_HEADS):
        o = _normalised(h * 2 * t, t, acc_ref) - lam * _normalised(h * 2 * t + t, t, acc_ref)
        o = o * lax.rsqrt(jnp.mean(o * o, axis=-1, keepdims=True) + EPS) * g_ref[...] * (1.0 - lam_init)
        o_ref[0, :, h * C_VDIM:(h + 1) * C_VDIM] = o.astype(o_ref.dtype)


def _diff_attention(qc, kc, vc, lam_p, subln_g, lam_init):
    b, s, w = qc.shape
    t = ATT_TILE
    qspec = pl.BlockSpec((1, t, w), lambda bi, i: (bi, i, 0))
    kvspec = pl.BlockSpec((1, s, w), lambda bi, i: (bi, 0, 0))
    return pl.pallas_call(
        functools.partial(_diff_kernel, lam_init=lam_init, t=t),
        grid=(b, s // t),
        in_specs=[pl.BlockSpec(lam_p.shape, lambda bi, i: (0, 0)),
                  pl.BlockSpec((1, C_VDIM), lambda bi, i: (0, 0)), qspec, kvspec, kvspec],
        out_specs=qspec,
        out_shape=jax.ShapeDtypeStruct((b, s, w), BF16),
        scratch_shapes=_ATT_SCRATCH(s // t, 2 * C_HEADS * t, t, 2 * LANES),
        compiler_params=_params("parallel", "arbitrary"),
        name="diff_attention",
    )(lam_p, subln_g, qc, kc, vc)


def _moba_kernel(q_ref, k_ref, v_ref, o_ref, kmean_hi, kmean_lo, bias_ref, s_ref, max_ref, acc_ref, *, t):
    it = pl.program_id(1)
    n_blk = k_ref.shape[1] // t
    width = q_ref.shape[2]

    @pl.when(it == 0)
    def _():
        kmean_hi[...] = jnp.zeros_like(kmean_hi)
        kmean_lo[...] = jnp.zeros_like(kmean_lo)
        for n in range(n_blk):
            km = jnp.mean(k_ref[0, n * t:(n + 1) * t, :].astype(F32), axis=0, keepdims=True)
            hi = km.astype(BF16)
            kmean_hi[n:n + 1, :] = hi
            kmean_lo[n:n + 1, :] = (km - hi.astype(F32)).astype(BF16)

    heads = _head_lane_masks(t, width)
    q_stack = _stack_masked(q_ref[0], heads)

    no_bias = jnp.zeros(bias_ref.shape[1:], F32)
    bias_ref[it] = no_bias

    @pl.when(it <= MOBA_TOPK)
    def _():
        for n in range(min(MOBA_TOPK, n_blk)):
            @pl.when(n < it)
            def _():
                bias_ref[n] = no_bias

    @pl.when(it > MOBA_TOPK)
    def _():
        blk = lax.broadcasted_iota(jnp.int32, (SUBLANES, t), 0)
        past = blk < it
        pad = jnp.full((LANES - SUBLANES, t), MASKED, F32)
        bias_q = []
        for h in range(B_HEADS):
            qh = q_stack[h * t:(h + 1) * t]
            gate = (_nt_dot(kmean_hi[...], qh) + _nt_dot(kmean_lo[...], qh))[:SUBLANES]
            gate = jnp.where(past, gate, -jnp.inf)
            rank = jnp.zeros((SUBLANES, t), jnp.int32)
            for n in range(n_blk):
                gn = gate[n:n + 1, :]
                rank = rank + jnp.where((gn > gate) | ((gn == gate) & (blk > n)), 1, 0)
            bias = jnp.where((rank < MOBA_TOPK) & past, 0.0, MASKED)
            bias_q.append(jnp.concatenate([bias, pad], axis=0).T.astype(BF16))
        bias_q = jnp.concatenate(bias_q, axis=0)
        lane_of = lax.broadcasted_iota(jnp.int32, (LANES, LANES), 0)
        for n in range(n_blk - 1):
            @pl.when(n < it)
            def _():
                pick = (lane_of == n).astype(BF16)
                bias_ref[n] = _dot(bias_q, pick)

    _reset_stats(max_ref, acc_ref)
    causal = jnp.where(_local_causal(t), 0.0, MASKED)
    causal4 = jnp.concatenate([causal] * B_HEADS, axis=0)

    def scores(j):
        s = _nt_dot(q_stack, k_ref[0, _tile_rows(j, t), :]) + _lane_tile(bias_ref[j], t)
        _record_scores(s + jnp.where(j == it, causal4, 0.0), j, 0, s_ref, max_ref)

    _for_each_tile(it + 1, scores, group=4)
    _finish_max(max_ref)

    def probs(j):
        rows = _tile_rows(j, t)
        for pair in range(B_HEADS // 2):
            _accumulate_probs(j, pair * 2 * t, 2 * t, _with_ones(v_ref[0, rows, pair * LANES:(pair + 1) * LANES]),
                              s_ref, max_ref, acc_ref)

    _for_each_tile(it + 1, probs, group=4)

    o = [_normalised(h * t, t, acc_ref) for h in range(B_HEADS)]
    o_ref[0] = _interleave_pairs(o, t).astype(o_ref.dtype)


def _moba_attention(qb, kb, vb):
    b, s, w = qb.shape
    t = MOBA_BLOCK
    assert s // t <= SUBLANES
    qspec = pl.BlockSpec((1, t, w), lambda bi, i: (bi, i, 0))
    kvspec = pl.BlockSpec((1, s, w), lambda bi, i: (bi, 0, 0))
    return pl.pallas_call(
        functools.partial(_moba_kernel, t=t),
        grid=(b, s // t),
        in_specs=[qspec, kvspec, kvspec],
        out_specs=qspec,
        out_shape=jax.ShapeDtypeStruct((b, s, w), BF16),
        scratch_shapes=[pltpu.VMEM((LANES, w), BF16), pltpu.VMEM((LANES, w), BF16),
                        pltpu.VMEM((s // t, B_HEADS * t, LANES), F32)]
                       + _ATT_SCRATCH(s // t, B_HEADS * t, t, 2 * LANES),
        compiler_params=_params("parallel", "arbitrary"),
        name="moba_attention",
    )(qb, kb, vb)


def _sublane_fold(x, rows):
    parts = [x[r:r + rows] for r in range(0, x.shape[0], rows)]
    while len(parts) > 1:
        parts = [parts[i] + parts[i + 1] for i in range(0, len(parts), 2)]
    return parts[0]


I16 = jnp.int16
I16_ROWS = 2 * SUBLANES
I16_MIN = -(2 ** 15)


def _dsa_kernel(q_ref, k_ref, v_ref, qi_ref, ki_ref, wi_ref, o_ref,
                keys_ref, hi_ref, lo_ref, ties_ref, s_ref, max_ref, acc_ref, *, t, topk):
    it = pl.program_id(1)
    width = q_ref.shape[2]
    heads = _head_lane_masks(t, width)
    causal = _local_causal(t)

    qidx = qi_ref[0]
    qi_stack = jnp.concatenate(
        [_stack_masked(qidx[:, g * width:(g + 1) * width], heads) for g in range(IDX_HEADS // 4)], axis=0)
    wi = wi_ref[0]
    w_rep = [jnp.broadcast_to(wi[:, h:h + 1], (t, LANES)) for h in range(IDX_HEADS)]

    def score_tile(j):
        logit = _nt_dot(qi_stack, ki_ref[0, _tile_rows(j, t), :])
        cols = []
        for c in range(0, t, LANES):
            acc = jnp.zeros((t, LANES), F32)
            for h in range(IDX_HEADS):
                acc = acc + jnp.maximum(logit[h * t:(h + 1) * t, c:c + LANES], 0.0) * w_rep[h]
            cols.append(acc)
        isc = jnp.concatenate(cols, axis=1) + 0.0
        isc = jnp.where(j == it, jnp.where(causal, isc, -jnp.inf), isc)
        bits = lax.bitcast_convert_type(isc, jnp.int32)
        key = bits ^ ((bits >> 31) & 0x7FFFFFFF)
        keys_ref[j] = key
        key_t = key.T
        hi_ref[j] = (key_t >> 16).astype(I16)
        lo_ref[j] = ((key_t & 0xFFFF) + I16_MIN).astype(I16)

    _for_each_tile(it + 1, score_tile, group=4)
    n_pairs = it // 2 + 1

    @pl.when(it % 2 == 0)
    def _():
        hi_ref[it + 1] = jnp.full((t, t), I16_MIN, I16)
        lo_ref[it + 1] = jnp.full((t, t), I16_MIN, I16)

    def count(ref, pred):
        def one(j, cnt):
            return cnt + _sublane_fold(jnp.where(pred(ref[j]), jnp.int16(1), jnp.int16(0)), I16_ROWS)
        cnt = lax.fori_loop(0, n_pairs, lambda i, cnt: one(2 * i + 1, one(2 * i, cnt)), jnp.zeros((I16_ROWS, t), I16))
        return jnp.sum(cnt.astype(jnp.int32), axis=0, keepdims=True)

    def kth_largest(ref, k):
        thr = jnp.where(count(ref, lambda x: x >= jnp.int16(0)) >= k, 0, jnp.full((1, t), I16_MIN, jnp.int32))

        def bit_step(i, thr):
            cand = thr | (1 << (14 - i))
            return jnp.where(count(ref, lambda x: x >= cand.astype(I16)) >= k, cand, thr)

        return lax.fori_loop(0, 15, bit_step, thr)

    thr_hi = kth_largest(hi_ref, topk)
    thr_hi16 = thr_hi.astype(I16)
    k_lo = topk - count(hi_ref, lambda x: x > thr_hi16)

    def keep_matching(j):
        lo_ref[j] = jnp.where(hi_ref[j] == thr_hi16, lo_ref[j], jnp.int16(I16_MIN))

    @pl.loop(0, n_pairs)
    def _(i):
        keep_matching(2 * i)
        keep_matching(2 * i + 1)

    thr_lo = kth_largest(lo_ref, k_lo)
    thr_lo16 = thr_lo.astype(I16)
    wanted = (k_lo - count(lo_ref, lambda x: x > thr_lo16)).astype(F32)
    thr = (thr_hi << 16) | (thr_lo - I16_MIN)
    thr_q = _lane_tile(jnp.broadcast_to(thr, (LANES, t)).T, t)
    wanted_q = _lane_tile(jnp.broadcast_to(wanted, (LANES, t)).T, t)

    q_stack = _stack_masked(q_ref[0], heads)
    r = lax.broadcasted_iota(jnp.int32, (t, t), 0)
    c = lax.broadcasted_iota(jnp.int32, (t, t), 1)
    before = (r < c).astype(BF16)
    ones = jnp.ones((t, LANES), BF16)
    _reset_stats(max_ref, acc_ref)

    ties_ref[...] = jnp.zeros(ties_ref.shape, F32)

    not_causal = jnp.where(causal, 0.0, MASKED)

    def select_tile(j):
        key = keys_ref[j]
        tie = key == thr_q
        tie_b = jnp.where(tie, 1.0, 0.0).astype(BF16)
        tie_rank = _lane_tile(ties_ref[...], t) + _dot(tie_b, before)
        bias = jnp.where(key > thr_q, 0.0, jnp.where(tie, jnp.where(tie_rank < wanted_q, 0.0, MASKED), MASKED))
        bias = bias + jnp.where(j == it, not_causal, 0.0)
        s = _nt_dot(q_stack, k_ref[0, _tile_rows(j, t), :]) + jnp.concatenate([bias] * A_HEADS, axis=0)
        _record_scores(s, j, 0, s_ref, max_ref)
        ties_ref[...] += _dot(tie_b, ones)

    _for_each_tile(it + 1, select_tile)
    _finish_max(max_ref)

    _for_each_tile(it + 1, lambda j: _accumulate_probs(
        j, 0, A_HEADS * t, v_ref[0, _tile_rows(j, t), :], s_ref, max_ref, acc_ref), group=4)

    sums_only = lax.broadcasted_iota(jnp.int32, (t, LANES), 1) >= HEAD_DIM
    o = []
    for h in range(A_HEADS):
        a = acc_ref[h * t:(h + 1) * t, :]
        inv = 1.0 / jnp.where(sums_only, a, 1.0)
        o.append(pltpu.roll(a, HEAD_DIM, 1) * inv if h % 2 else a * pltpu.roll(inv, HEAD_DIM, 1))
    o_ref[0] = _interleave_pairs(o, t).astype(o_ref.dtype)


def _dsa_attention(qa, ka, va, qi, ki, wi):
    b, s, w = qa.shape
    t = ATT_TILE
    topk = min(INDEX_TOPK, s // 4)
    assert t >= topk
    row = lambda width: pl.BlockSpec((1, t, width), lambda bi, i: (bi, i, 0))
    full = lambda width: pl.BlockSpec((1, s, width), lambda bi, i: (bi, 0, 0))
    return pl.pallas_call(
        functools.partial(_dsa_kernel, t=t, topk=topk),
        grid=(b, s // t),
        in_specs=[row(w), full(w), full(va.shape[2]), row(qi.shape[2]), full(w), row(LANES)],
        out_specs=row(w),
        out_shape=jax.ShapeDtypeStruct((b, s, w), BF16),
        scratch_shapes=[pltpu.VMEM((s // t, t, t), jnp.int32),
                        pltpu.VMEM((s // t + s // t % 2, t, t), I16), pltpu.VMEM((s // t + s // t % 2, t, t), I16),
                        pltpu.VMEM((t, LANES), F32)]
                       + _ATT_SCRATCH(s // t, A_HEADS * t, t, LANES),
        compiler_params=_params("parallel", "arbitrary"),
        name="dsa_attention",
    )(qa, ka, va, qi, ki, wi)


def _merge_kernel(x_ref, oa_ref, ob_ref, oc_ref, gate_ref, wb_ref, wo_ref, o_ref):
    d = x_ref.shape[1]
    wa, wbw = oa_ref.shape[1], ob_ref.shape[1]
    merged = gate_ref[:, 0:d].astype(F32) * _dot(oa_ref[...], wb_ref[0:wa, :])
    merged = merged + gate_ref[:, d:2 * d].astype(F32) * _dot(ob_ref[...], wb_ref[wa:wa + wbw, :])
    merged = merged + gate_ref[:, 2 * d:3 * d].astype(F32) * _dot(oc_ref[...], wb_ref[wa + wbw:, :])
    o_ref[...] = x_ref[...] + _dot(merged.astype(BF16), wo_ref[...])


def _merge(x, oa, ob, oc, gate, wb, wo, tm):
    n, d = x.shape
    row = lambda w: pl.BlockSpec((tm, w), lambda i: (i, 0))
    return pl.pallas_call(
        _merge_kernel,
        grid=(n // tm,),
        in_specs=[row(d), row(oa.shape[1]), row(ob.shape[1]), row(oc.shape[1]), row(gate.shape[1]),
                  _resident(wb.shape), _resident(wo.shape)],
        out_specs=row(d),
        out_shape=jax.ShapeDtypeStruct((n, d), F32),
        compiler_params=_params("parallel"),
        name="merge_out",
    )(x, oa, ob, oc, gate, wb, wo)


def kernel(x, positions, norm_g, w_in, qk_norm_g, lambda_params, diff_subln_g, w_branch, w_out,
           ffn_w_gate, ffn_w_up, ffn_w_down):
    b, s, d = x.shape
    n = b * s
    depth = norm_g.shape[0]
    tm = min(512, n)
    tm_ffn = min(1024, n)
    assert n % tm_ffn == 0 and s % ATT_TILE == 0 and ATT_TILE == MOBA_BLOCK

    cos, sin = _rope_tables(positions, min(4 * tm, n))
    w_proj = _proj_weights(w_in)
    w_gate, w_up, w_down = (w.astype(BF16) for w in (ffn_w_gate, ffn_w_up, ffn_w_down))
    w_mix, w_o = w_branch.astype(BF16), w_out.astype(BF16)
    xf = x.reshape(n, d)
    for layer in range(depth):
        lam_init = 0.8 - 0.6 * math.exp(-0.3 * layer)
        ffn_w = lambda i: (w_gate[layer, i], w_up[layer, i], w_down[layer, i])
        xf = _ffn(xf, norm_g[layer, 0][None, :], *ffn_w(0), tm_ffn)

        qkg = jnp.tile(qk_norm_g[layer], (1, LANES // HEAD_DIM))
        outs = _proj(xf, norm_g[layer, 1][None, :], qkg, cos, sin, w_proj[layer], tm)
        p = {name: o.reshape(b, s, o.shape[1]) for (name, *_), o in zip(_PROJ_OUTPUTS, outs)}
        o_a = _dsa_attention(p["qa"], p["ka"], p["va"], p["qi"], p["ki"], p["wi"])
        o_b = _moba_attention(p["qb"], p["kb"], p["vb"])
        o_c = _diff_attention(p["qc"], p["kc"], p["vc"], lambda_params[layer], diff_subln_g[layer][None, :], lam_init)
        xf = _merge(xf, o_a.reshape(n, -1), o_b.reshape(n, -1), o_c.reshape(n, -1), p["gate"].reshape(n, -1),
                    w_mix[layer], w_o[layer], tm_ffn)

        xf = _ffn(xf, norm_g[layer, 2][None, :], *ffn_w(1), tm_ffn)
    return xf.reshape(b, s, d)
```

```python
import functools
import math

import jax
import jax.numpy as jnp
from jax import lax
from jax.experimental import pallas as pl
from jax.experimental.pallas import tpu as pltpu

F32 = jnp.float32
BF16 = jnp.bfloat16

HEAD_DIM = 64
ROT_DIM = HEAD_DIM // 4
ROT_HALF = ROT_DIM // 2
ROPE_THETA = 500000.0
EPS = 1e-6
A_HEADS = 4
IDX_HEADS = 8
IDX_DIM = 64
INDEX_TOPK = 256
B_HEADS = 4
MOBA_BLOCK = 256
MOBA_TOPK = 3
C_HEADS = 4
C_VDIM = 2 * HEAD_DIM

LANES = 128
SUBLANES = 8
MXU_DIM = 256
ATT_TILE = 256
MASKED = -1e30
VMEM_LIMIT = 56 * 1024 * 1024
LOG2E = 1.4426950408889634


def _nt_dot(a, b):
    return lax.dot_general(a, b, (((1,), (1,)), ((), ())), preferred_element_type=F32)


def _dot(a, b):
    return jnp.dot(a, b, preferred_element_type=F32)


def _resident(shape):
    nd = len(shape)
    return pl.BlockSpec(shape, lambda *_: (0,) * nd, pipeline_mode=pl.Buffered(1))


def _params(*sem):
    return pltpu.CompilerParams(dimension_semantics=sem, vmem_limit_bytes=VMEM_LIMIT)


def _rope_table_kernel(pos_ref, freq_ref, cos_ref, sin_ref):
    ang = pos_ref[...].astype(F32) * freq_ref[...]
    lane = lax.broadcasted_iota(jnp.int32, ang.shape, 1) % HEAD_DIM
    c = jnp.cos(ang)
    s = jnp.sin(ang)
    cos_ref[...] = jnp.where(lane < ROT_DIM, c, 1.0)
    sin_ref[...] = jnp.where(lane < ROT_HALF, -s, jnp.where(lane < ROT_DIM, s, 0.0))


def _rope_tables(positions, tm):
    n = positions.size
    pos = positions.reshape(n, 1)
    inv_freq = jnp.power(ROPE_THETA, -jnp.arange(0, ROT_DIM, 2, dtype=F32) / ROT_DIM)
    freq = jnp.tile(jnp.concatenate([inv_freq, inv_freq, jnp.zeros((HEAD_DIM - ROT_DIM,), F32)]), 2)[None, :]
    return pl.pallas_call(
        _rope_table_kernel,
        grid=(n // tm,),
        in_specs=[pl.BlockSpec((tm, 1), lambda i: (i, 0)), pl.BlockSpec((1, LANES), lambda i: (0, 0))],
        out_specs=[pl.BlockSpec((tm, LANES), lambda i: (i, 0))] * 2,
        out_shape=[jax.ShapeDtypeStruct((n, LANES), F32)] * 2,
        compiler_params=_params("parallel"),
        name="rope_tables",
    )(pos, freq)


def _rmsnorm_rows(x, g):
    return x * lax.rsqrt(jnp.mean(x * x, axis=-1, keepdims=True) + EPS) * g


def _ffn_kernel(x_ref, g_ref, wg_ref, wu_ref, wd_ref, o_ref, *, chunk):
    x = x_ref[...]
    h = _rmsnorm_rows(x, g_ref[...]).astype(BF16)
    acc = jnp.zeros(x.shape, F32)
    for c in range(0, wg_ref.shape[1], chunk):
        a = _dot(h, wg_ref[:, c:c + chunk])
        b = _dot(h, wu_ref[:, c:c + chunk])
        t = (a * jax.nn.sigmoid(a) * b).astype(BF16)
        acc = acc + _dot(t, wd_ref[c:c + chunk, :])
    o_ref[...] = x + 0.5 * acc


def _ffn(x, g, wg, wu, wd, tm):
    n, d = x.shape
    f = wg.shape[1]
    return pl.pallas_call(
        functools.partial(_ffn_kernel, chunk=MXU_DIM),
        grid=(n // tm,),
        in_specs=[pl.BlockSpec((tm, d), lambda i: (i, 0)), _resident((1, d)),
                  _resident((d, f)), _resident((d, f)), _resident((f, d))],
        out_specs=pl.BlockSpec((tm, d), lambda i: (i, 0)),
        out_shape=jax.ShapeDtypeStruct((n, d), F32),
        compiler_params=_params("parallel"),
        name="ffn",
    )(x, g, wg, wu, wd)


_W_IN_SPLITS = (("qa", 256), ("ka", 64), ("va", 64), ("qi", 512), ("ki", 64), ("wi", 8), ("qb", 256), ("kb", 256),
                ("vb", 256), ("qc", 512), ("kc", 512), ("vc", 512), ("ga", 1024), ("gb", 1024), ("gc", 1024))
_PROJ_COLUMNS = (
    ("qa", 256, "norm_rope_scale", 0),
    ("key_value", 128, "key_value", 1),
    ("qi", 512, "rope", None),
    ("index_key_weight", 128, "index_key_weight", None),
    ("qb", 256, "norm_rope_scale", 2),
    ("kb", 256, "norm_rope", 3),
    ("vb", 256, "plain", None),
    ("qc", 512, "norm_rope_scale", 4),
    ("kc", 512, "norm_rope", 5),
    ("vc", 512, "plain", None),
    ("gate", 3072, "sigmoid", None),
)
_PROJ_OUTPUTS = (
    ("qa", 256, BF16), ("ka", 256, BF16), ("va", 128, BF16), ("qi", 512, BF16), ("ki", 256, BF16), ("wi", 128, F32),
    ("qb", 256, BF16), ("kb", 256, BF16), ("vb", 256, BF16), ("qc", 512, BF16), ("kc", 512, BF16), ("vc", 512, BF16),
    ("gate", 3072, BF16),
)
_Q_SCALE = HEAD_DIM ** -0.5 * LOG2E


def _proj_kernel(x_ref, g_ref, qkg_ref, cos_ref, sin_ref, w_ref, *o_refs):
    out = {name: ref for (name, _, _), ref in zip(_PROJ_OUTPUTS, o_refs)}
    tm = x_ref.shape[0]
    h = _rmsnorm_rows(x_ref[...], g_ref[...]).astype(BF16)
    cos = cos_ref[...]
    sin = sin_ref[...]
    lane = lax.broadcasted_iota(jnp.int32, (tm, LANES), 1)
    take_upper = (lane % HEAD_DIM) < ROT_HALF
    low_half = lane < HEAD_DIM
    r = lax.broadcasted_iota(jnp.int32, (LANES, LANES), 0) // HEAD_DIM
    c = lax.broadcasted_iota(jnp.int32, (LANES, LANES), 1) // HEAD_DIM
    same_head = (r == c).astype(BF16)

    def epilogue(y, kind, g_idx):
        if kind in ("norm_rope_scale", "norm_rope"):
            ssq = _dot((y * y).astype(BF16), same_head)
            y = y * lax.rsqrt(ssq * (1.0 / HEAD_DIM) + EPS) * qkg_ref[g_idx:g_idx + 1, :]
        if kind in ("norm_rope_scale", "norm_rope", "rope"):
            partner = jnp.where(take_upper, pltpu.roll(y, LANES - ROT_HALF, 1), pltpu.roll(y, ROT_HALF, 1))
            y = y * cos + partner * sin
        if kind == "norm_rope_scale":
            y = y * _Q_SCALE
        if kind == "sigmoid":
            y = jax.nn.sigmoid(y)
        return y

    def on_both_halves(y):
        return jnp.where(low_half, y, pltpu.roll(y, HEAD_DIM, 1))

    col = 0
    for name, width, kind, g_idx in _PROJ_COLUMNS:
        step = min(width, MXU_DIM)
        for c0 in range(0, width, step):
            y = _dot(h, w_ref[:, col + c0:col + c0 + step])
            if kind == "key_value":
                ka = on_both_halves(epilogue(y, "norm_rope", g_idx)).astype(BF16)
                for s0 in range(0, out["ka"].shape[1], LANES):
                    out["ka"][:, s0:s0 + LANES] = ka
                out["va"][...] = jnp.where(low_half, pltpu.roll(y, HEAD_DIM, 1), 1.0).astype(BF16)
                continue
            if kind == "index_key_weight":
                ki = on_both_halves(epilogue(y, "rope", None)).astype(BF16)
                for s0 in range(0, out["ki"].shape[1], LANES):
                    out["ki"][:, s0:s0 + LANES] = ki
                out["wi"][...] = y * ((IDX_HEADS * IDX_DIM) ** -0.5)
                continue
            o_ref = out[name]
            for s0 in range(0, step, LANES):
                o_ref[:, c0 + s0:c0 + s0 + LANES] = epilogue(y[:, s0:s0 + LANES], kind, g_idx).astype(o_ref.dtype)
        col += width


def _proj_weights(w_in):
    w = w_in.astype(BF16)
    names = [name for name, _ in _W_IN_SPLITS]
    end_of_wi = sum(width for _, width in _W_IN_SPLITS[:names.index("wi") + 1])
    pad = jnp.zeros(w.shape[:-1] + (-end_of_wi % LANES,), w.dtype)
    padded = jnp.concatenate([w[..., :end_of_wi], pad, w[..., end_of_wi:]], axis=-1)
    assert padded.shape[-1] == sum(width for _, width, _, _ in _PROJ_COLUMNS)
    return padded


def _proj(x, g, qkg, cos, sin, w, tm):
    n, d = x.shape
    row = lambda width: pl.BlockSpec((tm, width), lambda i: (i, 0))
    return pl.pallas_call(
        _proj_kernel,
        grid=(n // tm,),
        in_specs=[row(d), _resident((1, d)), _resident(qkg.shape), row(LANES), row(LANES), _resident(w.shape)],
        out_specs=[row(width) for _, width, _ in _PROJ_OUTPUTS],
        out_shape=[jax.ShapeDtypeStruct((n, width), dtype) for _, width, dtype in _PROJ_OUTPUTS],
        compiler_params=_params("parallel"),
        name="in_proj",
    )(x, g, qkg, cos, sin, w)


def _lane_fold(x, op):
    out = x[:, :LANES]
    for c in range(LANES, x.shape[1], LANES):
        out = op(out, x[:, c:c + LANES])
    return out


def _lane_tile(x, width):
    return jnp.concatenate([x] * (width // LANES), axis=1)


def _reset_stats(max_ref, acc_ref):
    max_ref[...] = jnp.full(max_ref.shape, MASKED, F32)
    acc_ref[...] = jnp.zeros(acc_ref.shape, F32)


def _record_scores(s, j, r0, s_ref, max_ref):
    n = s.shape[0]
    s_ref[j, r0:r0 + n, :] = s
    max_ref[r0:r0 + n, :] = jnp.maximum(max_ref[r0:r0 + n, :], _lane_fold(s, jnp.maximum))


def _finish_max(max_ref):
    m = max_ref[...]
    max_ref[...] = jnp.broadcast_to(jnp.max(m, axis=-1, keepdims=True), m.shape)


def _accumulate_probs(j, r0, n, v, s_ref, max_ref, acc_ref):
    s = s_ref[j, r0:r0 + n, :]
    p = jnp.exp2(s - _lane_tile(max_ref[r0:r0 + n, :], s.shape[1]))
    acc_ref[r0:r0 + n, :] += _dot(p.astype(BF16), v)


def _with_ones(v):
    return jnp.concatenate([v, jnp.ones(v.shape, v.dtype)], axis=1)


def _normalised(r0, n, acc_ref):
    return acc_ref[r0:r0 + n, :LANES] / acc_ref[r0:r0 + n, LANES:]


def _head_lane_masks(rows, width):
    lane = lax.broadcasted_iota(jnp.int32, (rows, width), 1)
    return [(lane >= h * HEAD_DIM) & (lane < (h + 1) * HEAD_DIM) for h in range(width // HEAD_DIM)]


def _stack_masked(q, masks):
    zero = jnp.zeros_like(q)
    return jnp.concatenate([jnp.where(m, q, zero) for m in masks], axis=0)


def _local_causal(t):
    row = lax.broadcasted_iota(jnp.int32, (t, t), 0)
    col = lax.broadcasted_iota(jnp.int32, (t, t), 1)
    return col <= row


def _tile_rows(j, t):
    return pl.ds(pl.multiple_of(j * t, t), t)


def _for_each_tile(n, body, group=2):
    @pl.loop(0, n // group)
    def _(i):
        for k in range(group):
            body(group * i + k)

    part = group // 2
    while part >= 1:
        @pl.when(n % (2 * part) >= part)
        def _():
            for k in range(part):
                body(n - n % (2 * part) + k)
        part //= 2


def _interleave_pairs(o, t):
    lane = lax.broadcasted_iota(jnp.int32, (t, LANES), 1)
    low = lane < HEAD_DIM
    return jnp.concatenate([jnp.where(low, o[0], o[1]), jnp.where(low, o[2], o[3])], axis=1)


_ATT_SCRATCH = lambda n_tiles, rows, t, acc_width: [
    pltpu.VMEM((n_tiles, rows, t), F32),
    pltpu.VMEM((rows, LANES), F32),
    pltpu.VMEM((rows, acc_width), F32),
]


def _diff_kernel(lam_ref, g_ref, q_ref, k_ref, v_ref, o_ref, s_ref, max_ref, acc_ref, *, lam_init, t):
    it = pl.program_id(1)
    lane = lax.broadcasted_iota(jnp.int32, (t, C_VDIM), 1)
    maps = [lane < HEAD_DIM, lane >= HEAD_DIM]
    q_all = q_ref[0]
    qs = [_stack_masked(q_all[:, h * C_VDIM:(h + 1) * C_VDIM], maps) for h in range(C_HEADS)]
    _reset_stats(max_ref, acc_ref)
    causal = jnp.where(_local_causal(t), 0.0, MASKED)
    causal2 = jnp.concatenate([causal, causal], axis=0)

    def scores(j):
        rows = _tile_rows(j, t)
        bias = jnp.where(j == it, causal2, 0.0)
        for h in range(C_HEADS):
            s = _nt_dot(qs[h], k_ref[0, rows, h * C_VDIM:(h + 1) * C_VDIM]) + bias
            _record_scores(s, j, h * 2 * t, s_ref, max_ref)

    _for_each_tile(it + 1, scores, group=4)
    _finish_max(max_ref)

    def probs(j):
        rows = _tile_rows(j, t)
        for h in range(C_HEADS):
            _accumulate_probs(j, h * 2 * t, 2 * t, _with_ones(v_ref[0, rows, h * C_VDIM:(h + 1) * C_VDIM]),
                              s_ref, max_ref, acc_ref)

    _for_each_tile(it + 1, probs, group=4)

    lp = lam_ref[...]
    lam = (jnp.exp(jnp.sum(lp[0:1] * lp[1:2], axis=-1, keepdims=True))
           - jnp.exp(jnp.sum(lp[2:3] * lp[3:4], axis=-1, keepdims=True)) + lam_init)
    for h in range(C_HEADS):
        o = _normalised(h * 2 * t, t, acc_ref) - lam * _normalised(h * 2 * t + t, t, acc_ref)
        o = o * lax.rsqrt(jnp.mean(o * o, axis=-1, keepdims=True) + EPS) * g_ref[...] * (1.0 - lam_init)
        o_ref[0, :, h * C_VDIM:(h + 1) * C_VDIM] = o.astype(o_ref.dtype)


def _diff_attention(qc, kc, vc, lam_p, subln_g, lam_init):
    b, s, w = qc.shape
    t = ATT_TILE
    qspec = pl.BlockSpec((1, t, w), lambda bi, i: (bi, i, 0))
    kvspec = pl.BlockSpec((1, s, w), lambda bi, i: (bi, 0, 0))
    return pl.pallas_call(
        functools.partial(_diff_kernel, lam_init=lam_init, t=t),
        grid=(b, s // t),
        in_specs=[pl.BlockSpec(lam_p.shape, lambda bi, i: (0, 0)),
                  pl.BlockSpec((1, C_VDIM), lambda bi, i: (0, 0)), qspec, kvspec, kvspec],
        out_specs=qspec,
        out_shape=jax.ShapeDtypeStruct((b, s, w), BF16),
        scratch_shapes=_ATT_SCRATCH(s // t, 2 * C_HEADS * t, t, 2 * LANES),
        compiler_params=_params("parallel", "arbitrary"),
        name="diff_attention",
    )(lam_p, subln_g, qc, kc, vc)


def _moba_kernel(q_ref, k_ref, v_ref, o_ref, kmean_hi, kmean_lo, bias_ref, s_ref, max_ref, acc_ref, *, t):
    it = pl.program_id(1)
    n_blk = k_ref.shape[1] // t
    width = q_ref.shape[2]

    @pl.when(it == 0)
    def _():
        kmean_hi[...] = jnp.zeros_like(kmean_hi)
        kmean_lo[...] = jnp.zeros_like(kmean_lo)
        for n in range(n_blk):
            km = jnp.mean(k_ref[0, n * t:(n + 1) * t, :].astype(F32), axis=0, keepdims=True)
            hi = km.astype(BF16)
            kmean_hi[n:n + 1, :] = hi
            kmean_lo[n:n + 1, :] = (km - hi.astype(F32)).astype(BF16)

    heads = _head_lane_masks(t, width)
    q_stack = _stack_masked(q_ref[0], heads)

    no_bias = jnp.zeros(bias_ref.shape[1:], F32)
    bias_ref[it] = no_bias

    @pl.when(it <= MOBA_TOPK)
    def _():
        for n in range(min(MOBA_TOPK, n_blk)):
            @pl.when(n < it)
            def _():
                bias_ref[n] = no_bias

    @pl.when(it > MOBA_TOPK)
    def _():
        blk = lax.broadcasted_iota(jnp.int32, (SUBLANES, t), 0)
        past = blk < it
        pad = jnp.full((LANES - SUBLANES, t), MASKED, F32)
        bias_q = []
        for h in range(B_HEADS):
            qh = q_stack[h * t:(h + 1) * t]
            gate = (_nt_dot(kmean_hi[...], qh) + _nt_dot(kmean_lo[...], qh))[:SUBLANES]
            gate = jnp.where(past, gate, -jnp.inf)
            rank = jnp.zeros((SUBLANES, t), jnp.int32)
            for n in range(n_blk):
                gn = gate[n:n + 1, :]
                rank = rank + jnp.where((gn > gate) | ((gn == gate) & (blk > n)), 1, 0)
            bias = jnp.where((rank < MOBA_TOPK) & past, 0.0, MASKED)
            bias_q.append(jnp.concatenate([bias, pad], axis=0).T.astype(BF16))
        bias_q = jnp.concatenate(bias_q, axis=0)
        lane_of = lax.broadcasted_iota(jnp.int32, (LANES, LANES), 0)
        for n in range(n_blk - 1):
            @pl.when(n < it)
            def _():
                pick = (lane_of == n).astype(BF16)
                bias_ref[n] = _dot(bias_q, pick)

    _reset_stats(max_ref, acc_ref)
    causal = jnp.where(_local_causal(t), 0.0, MASKED)
    causal4 = jnp.concatenate([causal] * B_HEADS, axis=0)

    def scores(j):
        s = _nt_dot(q_stack, k_ref[0, _tile_rows(j, t), :]) + _lane_tile(bias_ref[j], t)
        _record_scores(s + jnp.where(j == it, causal4, 0.0), j, 0, s_ref, max_ref)

    _for_each_tile(it + 1, scores, group=4)
    _finish_max(max_ref)

    def probs(j):
        rows = _tile_rows(j, t)
        for pair in range(B_HEADS // 2):
            _accumulate_probs(j, pair * 2 * t, 2 * t, _with_ones(v_ref[0, rows, pair * LANES:(pair + 1) * LANES]),
                              s_ref, max_ref, acc_ref)

    _for_each_tile(it + 1, probs, group=4)

    o = [_normalised(h * t, t, acc_ref) for h in range(B_HEADS)]
    o_ref[0] = _interleave_pairs(o, t).astype(o_ref.dtype)


def _moba_attention(qb, kb, vb):
    b, s, w = qb.shape
    t = MOBA_BLOCK
    assert s // t <= SUBLANES
    qspec = pl.BlockSpec((1, t, w), lambda bi, i: (bi, i, 0))
    kvspec = pl.BlockSpec((1, s, w), lambda bi, i: (bi, 0, 0))
    return pl.pallas_call(
        functools.partial(_moba_kernel, t=t),
        grid=(b, s // t),
        in_specs=[qspec, kvspec, kvspec],
        out_specs=qspec,
        out_shape=jax.ShapeDtypeStruct((b, s, w), BF16),
        scratch_shapes=[pltpu.VMEM((LANES, w), BF16), pltpu.VMEM((LANES, w), BF16),
                        pltpu.VMEM((s // t, B_HEADS * t, LANES), F32)]
                       + _ATT_SCRATCH(s // t, B_HEADS * t, t, 2 * LANES),
        compiler_params=_params("parallel", "arbitrary"),
        name="moba_attention",
    )(qb, kb, vb)


def _sublane_fold(x, rows):
    parts = [x[r:r + rows] for r in range(0, x.shape[0], rows)]
    while len(parts) > 1:
        parts = [parts[i] + parts[i + 1] for i in range(0, len(parts), 2)]
    return parts[0]


I16 = jnp.int16
I16_ROWS = 2 * SUBLANES
I16_MIN = -(2 ** 15)


def _dsa_kernel(q_ref, k_ref, v_ref, qi_ref, ki_ref, wi_ref, o_ref,
                keys_ref, hi_ref, lo_ref, ties_ref, s_ref, max_ref, acc_ref, *, t, topk):
    it = pl.program_id(1)
    width = q_ref.shape[2]
    heads = _head_lane_masks(t, width)
    causal = _local_causal(t)

    qidx = qi_ref[0]
    qi_stack = jnp.concatenate(
        [_stack_masked(qidx[:, g * width:(g + 1) * width], heads) for g in range(IDX_HEADS // 4)], axis=0)
    wi = wi_ref[0]
    w_rep = [jnp.broadcast_to(wi[:, HEAD_DIM + h:HEAD_DIM + h + 1], (t, LANES)) for h in range(IDX_HEADS)]

    def score_tile(j):
        logit = _nt_dot(qi_stack, ki_ref[0, _tile_rows(j, t), :])
        cols = []
        for c in range(0, t, LANES):
            acc = jnp.zeros((t, LANES), F32)
            for h in range(IDX_HEADS):
                acc = acc + jnp.maximum(logit[h * t:(h + 1) * t, c:c + LANES], 0.0) * w_rep[h]
            cols.append(acc)
        isc = jnp.concatenate(cols, axis=1) + 0.0
        isc = jnp.where(j == it, jnp.where(causal, isc, -jnp.inf), isc)
        bits = lax.bitcast_convert_type(isc, jnp.int32)
        key = bits ^ ((bits >> 31) & 0x7FFFFFFF)
        keys_ref[j] = key
        key_t = key.T
        hi_ref[j] = (key_t >> 16).astype(I16)
        lo_ref[j] = ((key_t & 0xFFFF) + I16_MIN).astype(I16)

    _for_each_tile(it + 1, score_tile, group=4)
    n_pairs = it // 2 + 1

    @pl.when(it % 2 == 0)
    def _():
        hi_ref[it + 1] = jnp.full((t, t), I16_MIN, I16)
        lo_ref[it + 1] = jnp.full((t, t), I16_MIN, I16)

    def count(ref, pred):
        def one(j, cnt):
            return cnt + _sublane_fold(jnp.where(pred(ref[j]), jnp.int16(1), jnp.int16(0)), I16_ROWS)
        cnt = lax.fori_loop(0, n_pairs, lambda i, cnt: one(2 * i + 1, one(2 * i, cnt)), jnp.zeros((I16_ROWS, t), I16))
        return jnp.sum(cnt.astype(jnp.int32), axis=0, keepdims=True)

    def kth_largest(ref, k):
        thr = jnp.where(count(ref, lambda x: x >= jnp.int16(0)) >= k, 0, jnp.full((1, t), I16_MIN, jnp.int32))

        def bit_step(i, thr):
            cand = thr | (1 << (14 - i))
            return jnp.where(count(ref, lambda x: x >= cand.astype(I16)) >= k, cand, thr)

        return lax.fori_loop(0, 15, bit_step, thr)

    thr_hi = kth_largest(hi_ref, topk)
    thr_hi16 = thr_hi.astype(I16)
    k_lo = topk - count(hi_ref, lambda x: x > thr_hi16)

    def keep_matching(j):
        lo_ref[j] = jnp.where(hi_ref[j] == thr_hi16, lo_ref[j], jnp.int16(I16_MIN))

    @pl.loop(0, n_pairs)
    def _(i):
        keep_matching(2 * i)
        keep_matching(2 * i + 1)

    thr_lo = kth_largest(lo_ref, k_lo)
    thr_lo16 = thr_lo.astype(I16)
    wanted = (k_lo - count(lo_ref, lambda x: x > thr_lo16)).astype(F32)
    thr = (thr_hi << 16) | (thr_lo - I16_MIN)
    thr_q = _lane_tile(jnp.broadcast_to(thr, (LANES, t)).T, t)
    wanted_q = _lane_tile(jnp.broadcast_to(wanted, (LANES, t)).T, t)

    q_stack = _stack_masked(q_ref[0], heads)
    r = lax.broadcasted_iota(jnp.int32, (t, t), 0)
    c = lax.broadcasted_iota(jnp.int32, (t, t), 1)
    before = (r < c).astype(BF16)
    ones = jnp.ones((t, LANES), BF16)
    _reset_stats(max_ref, acc_ref)

    ties_ref[...] = jnp.zeros(ties_ref.shape, F32)

    not_causal = jnp.where(causal, 0.0, MASKED)

    def select_tile(j):
        key = keys_ref[j]
        tie = key == thr_q
        tie_b = jnp.where(tie, 1.0, 0.0).astype(BF16)
        tie_rank = _lane_tile(ties_ref[...], t) + _dot(tie_b, before)
        bias = jnp.where(key > thr_q, 0.0, jnp.where(tie, jnp.where(tie_rank < wanted_q, 0.0, MASKED), MASKED))
        bias = bias + jnp.where(j == it, not_causal, 0.0)
        s = _nt_dot(q_stack, k_ref[0, _tile_rows(j, t), :]) + jnp.concatenate([bias] * A_HEADS, axis=0)
        _record_scores(s, j, 0, s_ref, max_ref)
        ties_ref[...] += _dot(tie_b, ones)

    _for_each_tile(it + 1, select_tile)
    _finish_max(max_ref)

    _for_each_tile(it + 1, lambda j: _accumulate_probs(
        j, 0, A_HEADS * t, v_ref[0, _tile_rows(j, t), :], s_ref, max_ref, acc_ref), group=4)

    sums_only = lax.broadcasted_iota(jnp.int32, (t, LANES), 1) >= HEAD_DIM
    o = []
    for h in range(A_HEADS):
        a = acc_ref[h * t:(h + 1) * t, :]
        inv = 1.0 / jnp.where(sums_only, a, 1.0)
        o.append(pltpu.roll(a, HEAD_DIM, 1) * inv if h % 2 else a * pltpu.roll(inv, HEAD_DIM, 1))
    o_ref[0] = _interleave_pairs(o, t).astype(o_ref.dtype)


def _dsa_attention(qa, ka, va, qi, ki, wi):
    b, s, w = qa.shape
    t = ATT_TILE
    topk = min(INDEX_TOPK, s // 4)
    assert t >= topk
    row = lambda width: pl.BlockSpec((1, t, width), lambda bi, i: (bi, i, 0))
    full = lambda width: pl.BlockSpec((1, s, width), lambda bi, i: (bi, 0, 0))
    return pl.pallas_call(
        functools.partial(_dsa_kernel, t=t, topk=topk),
        grid=(b, s // t),
        in_specs=[row(w), full(w), full(va.shape[2]), row(qi.shape[2]), full(w), row(LANES)],
        out_specs=row(w),
        out_shape=jax.ShapeDtypeStruct((b, s, w), BF16),
        scratch_shapes=[pltpu.VMEM((s // t, t, t), jnp.int32),
                        pltpu.VMEM((s // t + s // t % 2, t, t), I16), pltpu.VMEM((s // t + s // t % 2, t, t), I16),
                        pltpu.VMEM((t, LANES), F32)]
                       + _ATT_SCRATCH(s // t, A_HEADS * t, t, LANES),
        compiler_params=_params("parallel", "arbitrary"),
        name="dsa_attention",
    )(qa, ka, va, qi, ki, wi)


def _merge_kernel(x_ref, oa_ref, ob_ref, oc_ref, gate_ref, wb_ref, wo_ref, o_ref):
    d = x_ref.shape[1]
    wa, wbw = oa_ref.shape[1], ob_ref.shape[1]
    merged = gate_ref[:, 0:d].astype(F32) * _dot(oa_ref[...], wb_ref[0:wa, :])
    merged = merged + gate_ref[:, d:2 * d].astype(F32) * _dot(ob_ref[...], wb_ref[wa:wa + wbw, :])
    merged = merged + gate_ref[:, 2 * d:3 * d].astype(F32) * _dot(oc_ref[...], wb_ref[wa + wbw:, :])
    o_ref[...] = x_ref[...] + _dot(merged.astype(BF16), wo_ref[...])


def _merge(x, oa, ob, oc, gate, wb, wo, tm):
    n, d = x.shape
    row = lambda w: pl.BlockSpec((tm, w), lambda i: (i, 0))
    return pl.pallas_call(
        _merge_kernel,
        grid=(n // tm,),
        in_specs=[row(d), row(oa.shape[1]), row(ob.shape[1]), row(oc.shape[1]), row(gate.shape[1]),
                  _resident(wb.shape), _resident(wo.shape)],
        out_specs=row(d),
        out_shape=jax.ShapeDtypeStruct((n, d), F32),
        compiler_params=_params("parallel"),
        name="merge_out",
    )(x, oa, ob, oc, gate, wb, wo)


def kernel(x, positions, norm_g, w_in, qk_norm_g, lambda_params, diff_subln_g, w_branch, w_out,
           ffn_w_gate, ffn_w_up, ffn_w_down):
    b, s, d = x.shape
    n = b * s
    depth = norm_g.shape[0]
    tm = min(512, n)
    tm_ffn = min(1024, n)
    assert n % tm_ffn == 0 and s % ATT_TILE == 0 and ATT_TILE == MOBA_BLOCK

    cos, sin = _rope_tables(positions, min(4 * tm, n))
    w_proj = _proj_weights(w_in)
    w_gate, w_up, w_down = (w.astype(BF16) for w in (ffn_w_gate, ffn_w_up, ffn_w_down))
    w_mix, w_o = w_branch.astype(BF16), w_out.astype(BF16)
    xf = x.reshape(n, d)
    for layer in range(depth):
        lam_init = 0.8 - 0.6 * math.exp(-0.3 * layer)
        ffn_w = lambda i: (w_gate[layer, i], w_up[layer, i], w_down[layer, i])
        xf = _ffn(xf, norm_g[layer, 0][None, :], *ffn_w(0), tm_ffn)

        qkg = jnp.tile(qk_norm_g[layer], (1, LANES // HEAD_DIM))
        outs = _proj(xf, norm_g[layer, 1][None, :], qkg, cos, sin, w_proj[layer], tm)
        p = {name: o.reshape(b, s, o.shape[1]) for (name, *_), o in zip(_PROJ_OUTPUTS, outs)}
        o_a = _dsa_attention(p["qa"], p["ka"], p["va"], p["qi"], p["ki"], p["wi"])
        o_b = _moba_attention(p["qb"], p["kb"], p["vb"])
        o_c = _diff_attention(p["qc"], p["kc"], p["vc"], lambda_params[layer], diff_subln_g[layer][None, :], lam_init)
        xf = _merge(xf, o_a.reshape(n, -1), o_b.reshape(n, -1), o_c.reshape(n, -1), p["gate"].reshape(n, -1),
                    w_mix[layer], w_o[layer], tm_ffn)

        xf = _ffn(xf, norm_g[layer, 2][None, :], *ffn_w(1), tm_ffn)
    return xf.reshape(b, s, d)
```

```python
import functools
import math

import jax
import jax.numpy as jnp
from jax import lax
from jax.experimental import pallas as pl
from jax.experimental.pallas import tpu as pltpu

F32 = jnp.float32
BF16 = jnp.bfloat16

HEAD_DIM = 64
ROT_DIM = HEAD_DIM // 4
ROT_HALF = ROT_DIM // 2
ROPE_THETA = 500000.0
EPS = 1e-6
A_HEADS = 4
IDX_HEADS = 8
IDX_DIM = 64
INDEX_TOPK = 256
B_HEADS = 4
MOBA_BLOCK = 256
MOBA_TOPK = 3
C_HEADS = 4
C_VDIM = 2 * HEAD_DIM

LANES = 128
SUBLANES = 8
MXU_DIM = 256
ATT_TILE = 256
MASKED = -1e30
VMEM_LIMIT = 56 * 1024 * 1024
LOG2E = 1.4426950408889634


def _nt_dot(a, b):
    return lax.dot_general(a, b, (((1,), (1,)), ((), ())), preferred_element_type=F32)


def _dot(a, b):
    return jnp.dot(a, b, preferred_element_type=F32)


def _resident(shape, index=()):
    block = (None,) * len(index) + tuple(shape[len(index):])
    at = tuple(index) + (0,) * (len(shape) - len(index))
    return pl.BlockSpec(block, lambda *_: at, pipeline_mode=pl.Buffered(1))


def _params(*sem):
    return pltpu.CompilerParams(dimension_semantics=sem, vmem_limit_bytes=VMEM_LIMIT)


def _rope_table_kernel(pos_ref, freq_ref, cos_ref, sin_ref):
    ang = pos_ref[...].astype(F32) * freq_ref[...]
    lane = lax.broadcasted_iota(jnp.int32, ang.shape, 1) % HEAD_DIM
    c = jnp.cos(ang)
    s = jnp.sin(ang)
    cos_ref[...] = jnp.where(lane < ROT_DIM, c, 1.0)
    sin_ref[...] = jnp.where(lane < ROT_HALF, -s, jnp.where(lane < ROT_DIM, s, 0.0))


def _rope_tables(positions, tm):
    n = positions.size
    pos = positions.reshape(n, 1)
    inv_freq = jnp.power(ROPE_THETA, -jnp.arange(0, ROT_DIM, 2, dtype=F32) / ROT_DIM)
    freq = jnp.tile(jnp.concatenate([inv_freq, inv_freq, jnp.zeros((HEAD_DIM - ROT_DIM,), F32)]), 2)[None, :]
    return pl.pallas_call(
        _rope_table_kernel,
        grid=(n // tm,),
        in_specs=[pl.BlockSpec((tm, 1), lambda i: (i, 0)), pl.BlockSpec((1, LANES), lambda i: (0, 0))],
        out_specs=[pl.BlockSpec((tm, LANES), lambda i: (i, 0))] * 2,
        out_shape=[jax.ShapeDtypeStruct((n, LANES), F32)] * 2,
        compiler_params=_params("parallel"),
        name="rope_tables",
    )(pos, freq)


def _rmsnorm_rows(x, g):
    return x * lax.rsqrt(jnp.mean(x * x, axis=-1, keepdims=True) + EPS) * g


def _ffn_kernel(x_ref, g_ref, wg_ref, wu_ref, wd_ref, o_ref, *, chunk):
    x = x_ref[...]
    h = _rmsnorm_rows(x, g_ref[...]).astype(BF16)
    acc = jnp.zeros(x.shape, F32)
    for c in range(0, wg_ref.shape[1], chunk):
        a = _dot(h, wg_ref[:, c:c + chunk])
        b = _dot(h, wu_ref[:, c:c + chunk])
        t = (a * jax.nn.sigmoid(a) * b).astype(BF16)
        acc = acc + _dot(t, wd_ref[c:c + chunk, :])
    o_ref[...] = x + 0.5 * acc


def _ffn(x, g, wg, wu, wd, which, tm):
    n, d = x.shape
    return pl.pallas_call(
        functools.partial(_ffn_kernel, chunk=MXU_DIM),
        grid=(n // tm,),
        in_specs=[pl.BlockSpec((tm, d), lambda i: (i, 0)), _resident((1, d)),
                  _resident(wg.shape, which), _resident(wu.shape, which), _resident(wd.shape, which)],
        out_specs=pl.BlockSpec((tm, d), lambda i: (i, 0)),
        out_shape=jax.ShapeDtypeStruct((n, d), F32),
        compiler_params=_params("parallel"),
        name="ffn",
    )(x, g, wg, wu, wd)


_W_IN_SPLITS = (("qa", 256), ("ka", 64), ("va", 64), ("qi", 512), ("ki", 64), ("wi", 8), ("qb", 256), ("kb", 256),
                ("vb", 256), ("qc", 512), ("kc", 512), ("vc", 512), ("ga", 1024), ("gb", 1024), ("gc", 1024))
_PROJ_COLUMNS = (
    ("qa", 256, "norm_rope_scale", 0),
    ("key_value", 128, "key_value", 1),
    ("qi", 512, "rope", None),
    ("index_key_weight", 128, "index_key_weight", None),
    ("qb", 256, "norm_rope_scale", 2),
    ("kb", 256, "norm_rope", 3),
    ("vb", 256, "plain", None),
    ("qc", 512, "norm_rope_scale", 4),
    ("kc", 512, "norm_rope", 5),
    ("vc", 512, "plain", None),
    ("gate", 3072, "sigmoid", None),
)
_PROJ_OUTPUTS = (
    ("qa", 256, BF16), ("ka", 256, BF16), ("va", 128, BF16), ("qi", 512, BF16), ("ki", 256, BF16), ("wi", 128, F32),
    ("qb", 256, BF16), ("kb", 256, BF16), ("vb", 256, BF16), ("qc", 512, BF16), ("kc", 512, BF16), ("vc", 512, BF16),
    ("gate", 3072, BF16),
)
_Q_SCALE = HEAD_DIM ** -0.5 * LOG2E


def _proj_kernel(x_ref, g_ref, qkg_ref, cos_ref, sin_ref, w_ref, *o_refs):
    out = {name: ref for (name, _, _), ref in zip(_PROJ_OUTPUTS, o_refs)}
    tm = x_ref.shape[0]
    h = _rmsnorm_rows(x_ref[...], g_ref[...]).astype(BF16)
    cos = cos_ref[...]
    sin = sin_ref[...]
    lane = lax.broadcasted_iota(jnp.int32, (tm, LANES), 1)
    take_upper = (lane % HEAD_DIM) < ROT_HALF
    low_half = lane < HEAD_DIM
    r = lax.broadcasted_iota(jnp.int32, (LANES, LANES), 0) // HEAD_DIM
    c = lax.broadcasted_iota(jnp.int32, (LANES, LANES), 1) // HEAD_DIM
    same_head = (r == c).astype(BF16)

    def epilogue(y, kind, g_idx):
        if kind in ("norm_rope_scale", "norm_rope"):
            ssq = _dot((y * y).astype(BF16), same_head)
            y = y * lax.rsqrt(ssq * (1.0 / HEAD_DIM) + EPS) * qkg_ref[g_idx:g_idx + 1, :]
        if kind in ("norm_rope_scale", "norm_rope", "rope"):
            partner = jnp.where(take_upper, pltpu.roll(y, LANES - ROT_HALF, 1), pltpu.roll(y, ROT_HALF, 1))
            y = y * cos + partner * sin
        if kind == "norm_rope_scale":
            y = y * _Q_SCALE
        if kind == "sigmoid":
            y = jax.nn.sigmoid(y)
        return y

    def on_both_halves(y):
        return jnp.where(low_half, y, pltpu.roll(y, HEAD_DIM, 1))

    col = 0
    for name, width, kind, g_idx in _PROJ_COLUMNS:
        step = min(width, MXU_DIM)
        for c0 in range(0, width, step):
            y = _dot(h, w_ref[:, col + c0:col + c0 + step])
            if kind == "key_value":
                ka = on_both_halves(epilogue(y, "norm_rope", g_idx)).astype(BF16)
                for s0 in range(0, out["ka"].shape[1], LANES):
                    out["ka"][:, s0:s0 + LANES] = ka
                out["va"][...] = jnp.where(low_half, pltpu.roll(y, HEAD_DIM, 1), 1.0).astype(BF16)
                continue
            if kind == "index_key_weight":
                ki = on_both_halves(epilogue(y, "rope", None)).astype(BF16)
                for s0 in range(0, out["ki"].shape[1], LANES):
                    out["ki"][:, s0:s0 + LANES] = ki
                out["wi"][...] = y * ((IDX_HEADS * IDX_DIM) ** -0.5)
                continue
            o_ref = out[name]
            for s0 in range(0, step, LANES):
                o_ref[:, c0 + s0:c0 + s0 + LANES] = epilogue(y[:, s0:s0 + LANES], kind, g_idx).astype(o_ref.dtype)
        col += width


def _proj_weights(w_in):
    w = w_in.astype(BF16)
    names = [name for name, _ in _W_IN_SPLITS]
    end_of_wi = sum(width for _, width in _W_IN_SPLITS[:names.index("wi") + 1])
    pad = jnp.zeros(w.shape[:-1] + (-end_of_wi % LANES,), w.dtype)
    padded = jnp.concatenate([w[..., :end_of_wi], pad, w[..., end_of_wi:]], axis=-1)
    assert padded.shape[-1] == sum(width for _, width, _, _ in _PROJ_COLUMNS)
    return padded


def _proj(x, g, qkg, cos, sin, w, layer, tm):
    n, d = x.shape
    row = lambda width: pl.BlockSpec((tm, width), lambda i: (i, 0))
    return pl.pallas_call(
        _proj_kernel,
        grid=(n // tm,),
        in_specs=[row(d), _resident((1, d)), _resident(qkg.shape), row(LANES), row(LANES),
                  _resident(w.shape, (layer,))],
        out_specs=[row(width) for _, width, _ in _PROJ_OUTPUTS],
        out_shape=[jax.ShapeDtypeStruct((n, width), dtype) for _, width, dtype in _PROJ_OUTPUTS],
        compiler_params=_params("parallel"),
        name="in_proj",
    )(x, g, qkg, cos, sin, w)


def _lane_fold(x, op):
    out = x[:, :LANES]
    for c in range(LANES, x.shape[1], LANES):
        out = op(out, x[:, c:c + LANES])
    return out


def _lane_tile(x, width):
    return jnp.concatenate([x] * (width // LANES), axis=1)


def _reset_stats(max_ref, acc_ref):
    max_ref[...] = jnp.full(max_ref.shape, MASKED, F32)
    acc_ref[...] = jnp.zeros(acc_ref.shape, F32)


def _record_scores(s, j, r0, s_ref, max_ref):
    n = s.shape[0]
    s_ref[j, r0:r0 + n, :] = s
    max_ref[r0:r0 + n, :] = jnp.maximum(max_ref[r0:r0 + n, :], _lane_fold(s, jnp.maximum))


def _finish_max(max_ref):
    m = max_ref[...]
    max_ref[...] = jnp.broadcast_to(jnp.max(m, axis=-1, keepdims=True), m.shape)


def _accumulate_probs(j, r0, n, v, s_ref, max_ref, acc_ref):
    s = s_ref[j, r0:r0 + n, :]
    p = jnp.exp2(s - _lane_tile(max_ref[r0:r0 + n, :], s.shape[1]))
    acc_ref[r0:r0 + n, :] += _dot(p.astype(BF16), v)


def _with_ones(v):
    return jnp.concatenate([v, jnp.ones(v.shape, v.dtype)], axis=1)


def _normalised(r0, n, acc_ref):
    return acc_ref[r0:r0 + n, :LANES] / acc_ref[r0:r0 + n, LANES:]


def _head_lane_masks(rows, width):
    lane = lax.broadcasted_iota(jnp.int32, (rows, width), 1)
    return [(lane >= h * HEAD_DIM) & (lane < (h + 1) * HEAD_DIM) for h in range(width // HEAD_DIM)]


def _stack_masked(q, masks):
    zero = jnp.zeros_like(q)
    return jnp.concatenate([jnp.where(m, q, zero) for m in masks], axis=0)


def _local_causal(t):
    row = lax.broadcasted_iota(jnp.int32, (t, t), 0)
    col = lax.broadcasted_iota(jnp.int32, (t, t), 1)
    return col <= row


def _tile_rows(j, t):
    return pl.ds(pl.multiple_of(j * t, t), t)


def _for_each_tile(n, body, group=2):
    @pl.loop(0, n // group)
    def _(i):
        for k in range(group):
            body(group * i + k)

    part = group // 2
    while part >= 1:
        @pl.when(n % (2 * part) >= part)
        def _():
            for k in range(part):
                body(n - n % (2 * part) + k)
        part //= 2


def _interleave_pairs(o, t):
    lane = lax.broadcasted_iota(jnp.int32, (t, LANES), 1)
    low = lane < HEAD_DIM
    return jnp.concatenate([jnp.where(low, o[0], o[1]), jnp.where(low, o[2], o[3])], axis=1)


_ATT_SCRATCH = lambda n_tiles, rows, t, acc_width: [
    pltpu.VMEM((n_tiles, rows, t), F32),
    pltpu.VMEM((rows, LANES), F32),
    pltpu.VMEM((rows, acc_width), F32),
]


def _diff_kernel(lam_ref, g_ref, q_ref, k_ref, v_ref, o_ref, s_ref, max_ref, acc_ref, *, lam_init, t):
    it = pl.program_id(1)
    lane = lax.broadcasted_iota(jnp.int32, (t, C_VDIM), 1)
    maps = [lane < HEAD_DIM, lane >= HEAD_DIM]
    q_all = q_ref[0]
    qs = [_stack_masked(q_all[:, h * C_VDIM:(h + 1) * C_VDIM], maps) for h in range(C_HEADS)]
    _reset_stats(max_ref, acc_ref)
    causal = jnp.where(_local_causal(t), 0.0, MASKED)
    causal2 = jnp.concatenate([causal, causal], axis=0)

    def scores(j):
        rows = _tile_rows(j, t)
        bias = jnp.where(j == it, causal2, 0.0)
        for h in range(C_HEADS):
            s = _nt_dot(qs[h], k_ref[0, rows, h * C_VDIM:(h + 1) * C_VDIM]) + bias
            _record_scores(s, j, h * 2 * t, s_ref, max_ref)

    _for_each_tile(it + 1, scores, group=4)
    _finish_max(max_ref)

    def probs(j):
        rows = _tile_rows(j, t)
        for h in range(C_HEADS):
            _accumulate_probs(j, h * 2 * t, 2 * t, _with_ones(v_ref[0, rows, h * C_VDIM:(h + 1) * C_VDIM]),
                              s_ref, max_ref, acc_ref)

    _for_each_tile(it + 1, probs, group=4)

    lp = lam_ref[...]
    lam = (jnp.exp(jnp.sum(lp[0:1] * lp[1:2], axis=-1, keepdims=True))
           - jnp.exp(jnp.sum(lp[2:3] * lp[3:4], axis=-1, keepdims=True)) + lam_init)
    for h in range(C_HEADS):
        o = _normalised(h * 2 * t, t, acc_ref) - lam * _normalised(h * 2 * t + t, t, acc_ref)
        o = o * lax.rsqrt(jnp.mean(o * o, axis=-1, keepdims=True) + EPS) * g_ref[...] * (1.0 - lam_init)
        o_ref[0, :, h * C_VDIM:(h + 1) * C_VDIM] = o.astype(o_ref.dtype)


def _diff_attention(qc, kc, vc, lam_p, subln_g, lam_init):
    b, s, w = qc.shape
    t = ATT_TILE
    qspec = pl.BlockSpec((1, t, w), lambda bi, i: (bi, i, 0))
    kvspec = pl.BlockSpec((1, s, w), lambda bi, i: (bi, 0, 0))
    return pl.pallas_call(
        functools.partial(_diff_kernel, lam_init=lam_init, t=t),
        grid=(b, s // t),
        in_specs=[pl.BlockSpec(lam_p.shape, lambda bi, i: (0, 0)),
                  pl.BlockSpec((1, C_VDIM), lambda bi, i: (0, 0)), qspec, kvspec, kvspec],
        out_specs=qspec,
        out_shape=jax.ShapeDtypeStruct((b, s, w), BF16),
        scratch_shapes=_ATT_SCRATCH(s // t, 2 * C_HEADS * t, t, 2 * LANES),
        compiler_params=_params("parallel", "arbitrary"),
        name="diff_attention",
    )(lam_p, subln_g, qc, kc, vc)


def _moba_kernel(q_ref, k_ref, v_ref, o_ref, kmean_hi, kmean_lo, bias_ref, s_ref, max_ref, acc_ref, *, t):
    it = pl.program_id(1)
    n_blk = k_ref.shape[1] // t
    width = q_ref.shape[2]

    @pl.when(it == 0)
    def _():
        kmean_hi[...] = jnp.zeros_like(kmean_hi)
        kmean_lo[...] = jnp.zeros_like(kmean_lo)
        for n in range(n_blk):
            km = jnp.mean(k_ref[0, n * t:(n + 1) * t, :].astype(F32), axis=0, keepdims=True)
            hi = km.astype(BF16)
            kmean_hi[n:n + 1, :] = hi
            kmean_lo[n:n + 1, :] = (km - hi.astype(F32)).astype(BF16)

    heads = _head_lane_masks(t, width)
    q_stack = _stack_masked(q_ref[0], heads)

    no_bias = jnp.zeros(bias_ref.shape[1:], F32)
    bias_ref[it] = no_bias

    @pl.when(it <= MOBA_TOPK)
    def _():
        for n in range(min(MOBA_TOPK, n_blk)):
            @pl.when(n < it)
            def _():
                bias_ref[n] = no_bias

    @pl.when(it > MOBA_TOPK)
    def _():
        blk = lax.broadcasted_iota(jnp.int32, (SUBLANES, t), 0)
        past = blk < it
        pad = jnp.full((LANES - SUBLANES, t), MASKED, F32)
        bias_q = []
        for h in range(B_HEADS):
            qh = q_stack[h * t:(h + 1) * t]
            gate = (_nt_dot(kmean_hi[...], qh) + _nt_dot(kmean_lo[...], qh))[:SUBLANES]
            gate = jnp.where(past, gate, -jnp.inf)
            rank = jnp.zeros((SUBLANES, t), jnp.int32)
            for n in range(n_blk):
                gn = gate[n:n + 1, :]
                rank = rank + jnp.where((gn > gate) | ((gn == gate) & (blk > n)), 1, 0)
            bias = jnp.where((rank < MOBA_TOPK) & past, 0.0, MASKED)
            bias_q.append(jnp.concatenate([bias, pad], axis=0).T.astype(BF16))
        bias_q = jnp.concatenate(bias_q, axis=0)
        lane_of = lax.broadcasted_iota(jnp.int32, (LANES, LANES), 0)
        for n in range(n_blk - 1):
            @pl.when(n < it)
            def _():
                pick = (lane_of == n).astype(BF16)
                bias_ref[n] = _dot(bias_q, pick)

    _reset_stats(max_ref, acc_ref)
    causal = jnp.where(_local_causal(t), 0.0, MASKED)
    causal4 = jnp.concatenate([causal] * B_HEADS, axis=0)

    def scores(j):
        s = _nt_dot(q_stack, k_ref[0, _tile_rows(j, t), :]) + _lane_tile(bias_ref[j], t)
        _record_scores(s + jnp.where(j == it, causal4, 0.0), j, 0, s_ref, max_ref)

    _for_each_tile(it + 1, scores, group=4)
    _finish_max(max_ref)

    def probs(j):
        rows = _tile_rows(j, t)
        for pair in range(B_HEADS // 2):
            _accumulate_probs(j, pair * 2 * t, 2 * t, _with_ones(v_ref[0, rows, pair * LANES:(pair + 1) * LANES]),
                              s_ref, max_ref, acc_ref)

    _for_each_tile(it + 1, probs, group=4)

    o = [_normalised(h * t, t, acc_ref) for h in range(B_HEADS)]
    o_ref[0] = _interleave_pairs(o, t).astype(o_ref.dtype)


def _moba_attention(qb, kb, vb):
    b, s, w = qb.shape
    t = MOBA_BLOCK
    assert s // t <= SUBLANES
    qspec = pl.BlockSpec((1, t, w), lambda bi, i: (bi, i, 0))
    kvspec = pl.BlockSpec((1, s, w), lambda bi, i: (bi, 0, 0))
    return pl.pallas_call(
        functools.partial(_moba_kernel, t=t),
        grid=(b, s // t),
        in_specs=[qspec, kvspec, kvspec],
        out_specs=qspec,
        out_shape=jax.ShapeDtypeStruct((b, s, w), BF16),
        scratch_shapes=[pltpu.VMEM((LANES, w), BF16), pltpu.VMEM((LANES, w), BF16),
                        pltpu.VMEM((s // t, B_HEADS * t, LANES), F32)]
                       + _ATT_SCRATCH(s // t, B_HEADS * t, t, 2 * LANES),
        compiler_params=_params("parallel", "arbitrary"),
        name="moba_attention",
    )(qb, kb, vb)


def _sublane_fold(x, rows):
    parts = [x[r:r + rows] for r in range(0, x.shape[0], rows)]
    while len(parts) > 1:
        parts = [parts[i] + parts[i + 1] for i in range(0, len(parts), 2)]
    return parts[0]


I16 = jnp.int16
I16_ROWS = 2 * SUBLANES
I16_MIN = -(2 ** 15)


def _dsa_kernel(q_ref, k_ref, v_ref, qi_ref, ki_ref, wi_ref, o_ref,
                keys_ref, hi_ref, lo_ref, ties_ref, s_ref, max_ref, acc_ref, *, t, topk):
    it = pl.program_id(1)
    width = q_ref.shape[2]
    heads = _head_lane_masks(t, width)
    causal = _local_causal(t)

    qidx = qi_ref[0]
    qi_stack = jnp.concatenate(
        [_stack_masked(qidx[:, g * width:(g + 1) * width], heads) for g in range(IDX_HEADS // 4)], axis=0)
    wi = wi_ref[0]
    w_rep = [jnp.broadcast_to(wi[:, HEAD_DIM + h:HEAD_DIM + h + 1], (t, LANES)) for h in range(IDX_HEADS)]

    def score_tile(j):
        logit = _nt_dot(qi_stack, ki_ref[0, _tile_rows(j, t), :])
        cols = []
        for c in range(0, t, LANES):
            acc = jnp.zeros((t, LANES), F32)
            for h in range(IDX_HEADS):
                acc = acc + jnp.maximum(logit[h * t:(h + 1) * t, c:c + LANES], 0.0) * w_rep[h]
            cols.append(acc)
        isc = jnp.concatenate(cols, axis=1) + 0.0
        isc = jnp.where(j == it, jnp.where(causal, isc, -jnp.inf), isc)
        bits = lax.bitcast_convert_type(isc, jnp.int32)
        key = bits ^ ((bits >> 31) & 0x7FFFFFFF)
        keys_ref[j] = key
        key_t = key.T
        hi_ref[j] = (key_t >> 16).astype(I16)
        lo_ref[j] = ((key_t & 0xFFFF) + I16_MIN).astype(I16)

    _for_each_tile(it + 1, score_tile, group=4)
    n_pairs = it // 2 + 1

    @pl.when(it % 2 == 0)
    def _():
        hi_ref[it + 1] = jnp.full((t, t), I16_MIN, I16)
        lo_ref[it + 1] = jnp.full((t, t), I16_MIN, I16)

    def count(ref, pred):
        def one(j, cnt):
            return cnt + _sublane_fold(jnp.where(pred(ref[j]), jnp.int16(1), jnp.int16(0)), I16_ROWS)
        cnt = lax.fori_loop(0, n_pairs, lambda i, cnt: one(2 * i + 1, one(2 * i, cnt)), jnp.zeros((I16_ROWS, t), I16))
        return jnp.sum(cnt.astype(jnp.int32), axis=0, keepdims=True)

    def kth_largest(ref, k):
        thr = jnp.where(count(ref, lambda x: x >= jnp.int16(0)) >= k, 0, jnp.full((1, t), I16_MIN, jnp.int32))

        def bit_step(i, thr):
            cand = thr | (1 << (14 - i))
            return jnp.where(count(ref, lambda x: x >= cand.astype(I16)) >= k, cand, thr)

        return lax.fori_loop(0, 15, bit_step, thr)

    thr_hi = kth_largest(hi_ref, topk)
    thr_hi16 = thr_hi.astype(I16)
    k_lo = topk - count(hi_ref, lambda x: x > thr_hi16)

    def keep_matching(j):
        lo_ref[j] = jnp.where(hi_ref[j] == thr_hi16, lo_ref[j], jnp.int16(I16_MIN))

    @pl.loop(0, n_pairs)
    def _(i):
        keep_matching(2 * i)
        keep_matching(2 * i + 1)

    thr_lo = kth_largest(lo_ref, k_lo)
    thr_lo16 = thr_lo.astype(I16)
    wanted = (k_lo - count(lo_ref, lambda x: x > thr_lo16)).astype(F32)
    thr = (thr_hi << 16) | (thr_lo - I16_MIN)
    thr_q = _lane_tile(jnp.broadcast_to(thr, (LANES, t)).T, t)
    wanted_q = _lane_tile(jnp.broadcast_to(wanted, (LANES, t)).T, t)

    q_stack = _stack_masked(q_ref[0], heads)
    r = lax.broadcasted_iota(jnp.int32, (t, t), 0)
    c = lax.broadcasted_iota(jnp.int32, (t, t), 1)
    before = (r < c).astype(BF16)
    ones = jnp.ones((t, LANES), BF16)
    _reset_stats(max_ref, acc_ref)

    ties_ref[...] = jnp.zeros(ties_ref.shape, F32)

    not_causal = jnp.where(causal, 0.0, MASKED)

    def select_tile(j):
        key = keys_ref[j]
        tie = key == thr_q
        tie_b = jnp.where(tie, 1.0, 0.0).astype(BF16)
        tie_rank = _lane_tile(ties_ref[...], t) + _dot(tie_b, before)
        bias = jnp.where(key > thr_q, 0.0, jnp.where(tie, jnp.where(tie_rank < wanted_q, 0.0, MASKED), MASKED))
        bias = bias + jnp.where(j == it, not_causal, 0.0)
        s = _nt_dot(q_stack, k_ref[0, _tile_rows(j, t), :]) + jnp.concatenate([bias] * A_HEADS, axis=0)
        _record_scores(s, j, 0, s_ref, max_ref)
        ties_ref[...] += _dot(tie_b, ones)

    _for_each_tile(it + 1, select_tile)
    _finish_max(max_ref)

    _for_each_tile(it + 1, lambda j: _accumulate_probs(
        j, 0, A_HEADS * t, v_ref[0, _tile_rows(j, t), :], s_ref, max_ref, acc_ref), group=4)

    sums_only = lax.broadcasted_iota(jnp.int32, (t, LANES), 1) >= HEAD_DIM
    o = []
    for h in range(A_HEADS):
        a = acc_ref[h * t:(h + 1) * t, :]
        inv = 1.0 / jnp.where(sums_only, a, 1.0)
        o.append(pltpu.roll(a, HEAD_DIM, 1) * inv if h % 2 else a * pltpu.roll(inv, HEAD_DIM, 1))
    o_ref[0] = _interleave_pairs(o, t).astype(o_ref.dtype)


def _dsa_attention(qa, ka, va, qi, ki, wi):
    b, s, w = qa.shape
    t = ATT_TILE
    topk = min(INDEX_TOPK, s // 4)
    assert t >= topk
    row = lambda width: pl.BlockSpec((1, t, width), lambda bi, i: (bi, i, 0))
    full = lambda width: pl.BlockSpec((1, s, width), lambda bi, i: (bi, 0, 0))
    return pl.pallas_call(
        functools.partial(_dsa_kernel, t=t, topk=topk),
        grid=(b, s // t),
        in_specs=[row(w), full(w), full(va.shape[2]), row(qi.shape[2]), full(w), row(LANES)],
        out_specs=row(w),
        out_shape=jax.ShapeDtypeStruct((b, s, w), BF16),
        scratch_shapes=[pltpu.VMEM((s // t, t, t), jnp.int32),
                        pltpu.VMEM((s // t + s // t % 2, t, t), I16), pltpu.VMEM((s // t + s // t % 2, t, t), I16),
                        pltpu.VMEM((t, LANES), F32)]
                       + _ATT_SCRATCH(s // t, A_HEADS * t, t, LANES),
        compiler_params=_params("parallel", "arbitrary"),
        name="dsa_attention",
    )(qa, ka, va, qi, ki, wi)


def _merge_kernel(x_ref, oa_ref, ob_ref, oc_ref, gate_ref, wb_ref, wo_ref, o_ref):
    d = x_ref.shape[1]
    wa, wbw = oa_ref.shape[1], ob_ref.shape[1]
    merged = gate_ref[:, 0:d].astype(F32) * _dot(oa_ref[...], wb_ref[0:wa, :])
    merged = merged + gate_ref[:, d:2 * d].astype(F32) * _dot(ob_ref[...], wb_ref[wa:wa + wbw, :])
    merged = merged + gate_ref[:, 2 * d:3 * d].astype(F32) * _dot(oc_ref[...], wb_ref[wa + wbw:, :])
    o_ref[...] = x_ref[...] + _dot(merged.astype(BF16), wo_ref[...])


def _merge(x, oa, ob, oc, gate, wb, wo, layer, tm):
    n, d = x.shape
    row = lambda w: pl.BlockSpec((tm, w), lambda i: (i, 0))
    return pl.pallas_call(
        _merge_kernel,
        grid=(n // tm,),
        in_specs=[row(d), row(oa.shape[1]), row(ob.shape[1]), row(oc.shape[1]), row(gate.shape[1]),
                  _resident(wb.shape, (layer,)), _resident(wo.shape, (layer,))],
        out_specs=row(d),
        out_shape=jax.ShapeDtypeStruct((n, d), F32),
        compiler_params=_params("parallel"),
        name="merge_out",
    )(x, oa, ob, oc, gate, wb, wo)


def kernel(x, positions, norm_g, w_in, qk_norm_g, lambda_params, diff_subln_g, w_branch, w_out,
           ffn_w_gate, ffn_w_up, ffn_w_down):
    b, s, d = x.shape
    n = b * s
    depth = norm_g.shape[0]
    tm = min(512, n)
    tm_ffn = min(1024, n)
    assert n % tm_ffn == 0 and s % ATT_TILE == 0 and ATT_TILE == MOBA_BLOCK

    cos, sin = _rope_tables(positions, min(4 * tm, n))
    w_proj = _proj_weights(w_in)
    ffn_w = tuple(w.astype(BF16) for w in (ffn_w_gate, ffn_w_up, ffn_w_down))
    w_mix, w_o = w_branch.astype(BF16), w_out.astype(BF16)
    xf = x.reshape(n, d)
    for layer in range(depth):
        lam_init = 0.8 - 0.6 * math.exp(-0.3 * layer)
        xf = _ffn(xf, norm_g[layer, 0][None, :], *ffn_w, (layer, 0), tm_ffn)

        qkg = jnp.tile(qk_norm_g[layer], (1, LANES // HEAD_DIM))
        outs = _proj(xf, norm_g[layer, 1][None, :], qkg, cos, sin, w_proj, layer, tm)
        p = {name: o.reshape(b, s, o.shape[1]) for (name, *_), o in zip(_PROJ_OUTPUTS, outs)}
        o_a = _dsa_attention(p["qa"], p["ka"], p["va"], p["qi"], p["ki"], p["wi"])
        o_b = _moba_attention(p["qb"], p["kb"], p["vb"])
        o_c = _diff_attention(p["qc"], p["kc"], p["vc"], lambda_params[layer], diff_subln_g[layer][None, :], lam_init)
        xf = _merge(xf, o_a.reshape(n, -1), o_b.reshape(n, -1), o_c.reshape(n, -1), p["gate"].reshape(n, -1),
                    w_mix, w_o, layer, tm_ffn)

        xf = _ffn(xf, norm_g[layer, 2][None, :], *ffn_w, (layer, 1), tm_ffn)
    return xf.reshape(b, s, d)
```

```python
import functools
import math

import jax
import jax.numpy as jnp
from jax import lax
from jax.experimental import pallas as pl
from jax.experimental.pallas import tpu as pltpu

F32 = jnp.float32
BF16 = jnp.bfloat16

HEAD_DIM = 64
ROT_DIM = HEAD_DIM // 4
ROT_HALF = ROT_DIM // 2
ROPE_THETA = 500000.0
EPS = 1e-6
A_HEADS = 4
IDX_HEADS = 8
IDX_DIM = 64
INDEX_TOPK = 256
B_HEADS = 4
MOBA_BLOCK = 256
MOBA_TOPK = 3
C_HEADS = 4
C_VDIM = 2 * HEAD_DIM

LANES = 128
SUBLANES = 8
MXU_DIM = 256
ATT_TILE = 256
MASKED = -1e30
VMEM_LIMIT = 56 * 1024 * 1024
LOG2E = 1.4426950408889634


def _nt_dot(a, b):
    return lax.dot_general(a, b, (((1,), (1,)), ((), ())), preferred_element_type=F32)


def _dot(a, b):
    return jnp.dot(a, b, preferred_element_type=F32)


def _resident(shape, index=()):
    block = (None,) * len(index) + tuple(shape[len(index):])
    at = tuple(index) + (0,) * (len(shape) - len(index))
    return pl.BlockSpec(block, lambda *_: at, pipeline_mode=pl.Buffered(1))


def _params(*sem):
    return pltpu.CompilerParams(dimension_semantics=sem, vmem_limit_bytes=VMEM_LIMIT)


def _rope_table_kernel(pos_ref, freq_ref, cos_ref, sin_ref):
    ang = pos_ref[...].astype(F32) * freq_ref[...]
    lane = lax.broadcasted_iota(jnp.int32, ang.shape, 1) % HEAD_DIM
    c = jnp.cos(ang)
    s = jnp.sin(ang)
    cos_ref[...] = jnp.where(lane < ROT_DIM, c, 1.0)
    sin_ref[...] = jnp.where(lane < ROT_HALF, -s, jnp.where(lane < ROT_DIM, s, 0.0))


def _rope_tables(positions, tm):
    n = positions.size
    pos = positions.reshape(n, 1)
    inv_freq = jnp.power(ROPE_THETA, -jnp.arange(0, ROT_DIM, 2, dtype=F32) / ROT_DIM)
    freq = jnp.tile(jnp.concatenate([inv_freq, inv_freq, jnp.zeros((HEAD_DIM - ROT_DIM,), F32)]), 2)[None, :]
    return pl.pallas_call(
        _rope_table_kernel,
        grid=(n // tm,),
        in_specs=[pl.BlockSpec((tm, 1), lambda i: (i, 0)), pl.BlockSpec((1, LANES), lambda i: (0, 0))],
        out_specs=[pl.BlockSpec((tm, LANES), lambda i: (i, 0))] * 2,
        out_shape=[jax.ShapeDtypeStruct((n, LANES), F32)] * 2,
        compiler_params=_params("parallel"),
        name="rope_tables",
    )(pos, freq)


def _rmsnorm_rows(x, g):
    return x * lax.rsqrt(jnp.mean(x * x, axis=-1, keepdims=True) + EPS) * g


def _ffn_kernel(x_ref, g_ref, wg_ref, wu_ref, wd_ref, o_ref, *, chunk):
    x = x_ref[...]
    h = _rmsnorm_rows(x, g_ref[...]).astype(BF16)
    acc = jnp.zeros(x.shape, F32)
    for c in range(0, wg_ref.shape[1], chunk):
        a = _dot(h, wg_ref[:, c:c + chunk])
        b = _dot(h, wu_ref[:, c:c + chunk])
        t = (a * jax.nn.sigmoid(a) * b).astype(BF16)
        acc = acc + _dot(t, wd_ref[c:c + chunk, :])
    o_ref[...] = x + 0.5 * acc


def _ffn(x, g, wg, wu, wd, which, tm):
    n, d = x.shape
    return pl.pallas_call(
        functools.partial(_ffn_kernel, chunk=MXU_DIM),
        grid=(n // tm,),
        in_specs=[pl.BlockSpec((tm, d), lambda i: (i, 0)), _resident((1, d)),
                  _resident(wg.shape, which), _resident(wu.shape, which), _resident(wd.shape, which)],
        out_specs=pl.BlockSpec((tm, d), lambda i: (i, 0)),
        out_shape=jax.ShapeDtypeStruct((n, d), F32),
        compiler_params=_params("parallel"),
        name="ffn",
    )(x, g, wg, wu, wd)


_W_IN_SPLITS = (("qa", 256), ("ka", 64), ("va", 64), ("qi", 512), ("ki", 64), ("wi", 8), ("qb", 256), ("kb", 256),
                ("vb", 256), ("qc", 512), ("kc", 512), ("vc", 512), ("ga", 1024), ("gb", 1024), ("gc", 1024))
_PROJ_COLUMNS = (
    ("qa", 256, "norm_rope_scale", 0),
    ("key_value", 128, "key_value", 1),
    ("qi", 512, "rope", None),
    ("index_key_weight", 128, "index_key_weight", None),
    ("qb", 256, "norm_rope_scale", 2),
    ("kb", 256, "norm_rope", 3),
    ("vb", 256, "plain", None),
    ("qc", 512, "norm_rope_scale", 4),
    ("kc", 512, "norm_rope", 5),
    ("vc", 512, "plain", None),
)
_GATE_COLUMNS = 3 * 1024
_PROJ_OUTPUTS = (
    ("qa", 256, BF16), ("ka", 256, BF16), ("va", 128, BF16), ("qi", 512, BF16), ("ki", 256, BF16), ("wi", 128, F32),
    ("qb", 256, BF16), ("kb", 256, BF16), ("vb", 256, BF16), ("qc", 512, BF16), ("kc", 512, BF16), ("vc", 512, BF16),
)
_Q_SCALE = HEAD_DIM ** -0.5 * LOG2E


def _proj_kernel(x_ref, g_ref, qkg_ref, cos_ref, sin_ref, w_ref, *o_refs):
    out = {name: ref for (name, _, _), ref in zip(_PROJ_OUTPUTS, o_refs)}
    tm = x_ref.shape[0]
    h = _rmsnorm_rows(x_ref[...], g_ref[...]).astype(BF16)
    cos = cos_ref[...]
    sin = sin_ref[...]
    lane = lax.broadcasted_iota(jnp.int32, (tm, LANES), 1)
    take_upper = (lane % HEAD_DIM) < ROT_HALF
    low_half = lane < HEAD_DIM
    r = lax.broadcasted_iota(jnp.int32, (LANES, LANES), 0) // HEAD_DIM
    c = lax.broadcasted_iota(jnp.int32, (LANES, LANES), 1) // HEAD_DIM
    same_head = (r == c).astype(BF16)

    def epilogue(y, kind, g_idx):
        if kind in ("norm_rope_scale", "norm_rope"):
            ssq = _dot((y * y).astype(BF16), same_head)
            y = y * lax.rsqrt(ssq * (1.0 / HEAD_DIM) + EPS) * qkg_ref[g_idx:g_idx + 1, :]
        if kind in ("norm_rope_scale", "norm_rope", "rope"):
            partner = jnp.where(take_upper, pltpu.roll(y, LANES - ROT_HALF, 1), pltpu.roll(y, ROT_HALF, 1))
            y = y * cos + partner * sin
        if kind == "norm_rope_scale":
            y = y * _Q_SCALE
        return y

    def on_both_halves(y):
        return jnp.where(low_half, y, pltpu.roll(y, HEAD_DIM, 1))

    col = 0
    for name, width, kind, g_idx in _PROJ_COLUMNS:
        step = min(width, MXU_DIM)
        for c0 in range(0, width, step):
            y = _dot(h, w_ref[:, col + c0:col + c0 + step])
            if kind == "key_value":
                ka = on_both_halves(epilogue(y, "norm_rope", g_idx)).astype(BF16)
                for s0 in range(0, out["ka"].shape[1], LANES):
                    out["ka"][:, s0:s0 + LANES] = ka
                out["va"][...] = jnp.where(low_half, pltpu.roll(y, HEAD_DIM, 1), 1.0).astype(BF16)
                continue
            if kind == "index_key_weight":
                ki = on_both_halves(epilogue(y, "rope", None)).astype(BF16)
                for s0 in range(0, out["ki"].shape[1], LANES):
                    out["ki"][:, s0:s0 + LANES] = ki
                out["wi"][...] = y * ((IDX_HEADS * IDX_DIM) ** -0.5)
                continue
            o_ref = out[name]
            for s0 in range(0, step, LANES):
                o_ref[:, c0 + s0:c0 + s0 + LANES] = epilogue(y[:, s0:s0 + LANES], kind, g_idx).astype(o_ref.dtype)
        col += width


def _proj_weights(w_in):
    w = w_in.astype(BF16)
    names = [name for name, _ in _W_IN_SPLITS]
    end_of_wi = sum(width for _, width in _W_IN_SPLITS[:names.index("wi") + 1])
    gates = w.shape[-1] - _GATE_COLUMNS
    pad = jnp.zeros(w.shape[:-1] + (-end_of_wi % LANES,), w.dtype)
    padded = jnp.concatenate([w[..., :end_of_wi], pad, w[..., end_of_wi:gates]], axis=-1)
    assert padded.shape[-1] == sum(width for _, width, _, _ in _PROJ_COLUMNS)
    return padded, w[..., gates:]


def _proj(x, g, qkg, cos, sin, w, layer, tm):
    n, d = x.shape
    row = lambda width: pl.BlockSpec((tm, width), lambda i: (i, 0))
    return pl.pallas_call(
        _proj_kernel,
        grid=(n // tm,),
        in_specs=[row(d), _resident((1, d)), _resident(qkg.shape), row(LANES), row(LANES),
                  _resident(w.shape, (layer,))],
        out_specs=[row(width) for _, width, _ in _PROJ_OUTPUTS],
        out_shape=[jax.ShapeDtypeStruct((n, width), dtype) for _, width, dtype in _PROJ_OUTPUTS],
        compiler_params=_params("parallel"),
        name="in_proj",
    )(x, g, qkg, cos, sin, w)


def _lane_fold(x, op):
    out = x[:, :LANES]
    for c in range(LANES, x.shape[1], LANES):
        out = op(out, x[:, c:c + LANES])
    return out


def _lane_tile(x, width):
    return jnp.concatenate([x] * (width // LANES), axis=1)


def _reset_stats(max_ref, acc_ref):
    max_ref[...] = jnp.full(max_ref.shape, MASKED, F32)
    acc_ref[...] = jnp.zeros(acc_ref.shape, F32)


def _record_scores(s, j, r0, s_ref, max_ref):
    n = s.shape[0]
    s_ref[j, r0:r0 + n, :] = s
    max_ref[r0:r0 + n, :] = jnp.maximum(max_ref[r0:r0 + n, :], _lane_fold(s, jnp.maximum))


def _finish_max(max_ref):
    m = max_ref[...]
    max_ref[...] = jnp.broadcast_to(jnp.max(m, axis=-1, keepdims=True), m.shape)


def _accumulate_probs(j, r0, n, v, s_ref, max_ref, acc_ref):
    s = s_ref[j, r0:r0 + n, :]
    p = jnp.exp2(s - _lane_tile(max_ref[r0:r0 + n, :], s.shape[1]))
    acc_ref[r0:r0 + n, :] += _dot(p.astype(BF16), v)


def _with_ones(v):
    return jnp.concatenate([v, jnp.ones(v.shape, v.dtype)], axis=1)


def _normalised(r0, n, acc_ref):
    return acc_ref[r0:r0 + n, :LANES] / acc_ref[r0:r0 + n, LANES:]


def _head_lane_masks(rows, width):
    lane = lax.broadcasted_iota(jnp.int32, (rows, width), 1)
    return [(lane >= h * HEAD_DIM) & (lane < (h + 1) * HEAD_DIM) for h in range(width // HEAD_DIM)]


def _stack_masked(q, masks):
    zero = jnp.zeros_like(q)
    return jnp.concatenate([jnp.where(m, q, zero) for m in masks], axis=0)


def _local_causal(t):
    row = lax.broadcasted_iota(jnp.int32, (t, t), 0)
    col = lax.broadcasted_iota(jnp.int32, (t, t), 1)
    return col <= row


def _tile_rows(j, t):
    return pl.ds(pl.multiple_of(j * t, t), t)


def _for_each_tile(n, body, group=2):
    @pl.loop(0, n // group)
    def _(i):
        for k in range(group):
            body(group * i + k)

    part = group // 2
    while part >= 1:
        @pl.when(n % (2 * part) >= part)
        def _():
            for k in range(part):
                body(n - n % (2 * part) + k)
        part //= 2


def _interleave_pairs(o, t):
    lane = lax.broadcasted_iota(jnp.int32, (t, LANES), 1)
    low = lane < HEAD_DIM
    return jnp.concatenate([jnp.where(low, o[0], o[1]), jnp.where(low, o[2], o[3])], axis=1)


_ATT_SCRATCH = lambda n_tiles, rows, t, acc_width: [
    pltpu.VMEM((n_tiles, rows, t), F32),
    pltpu.VMEM((rows, LANES), F32),
    pltpu.VMEM((rows, acc_width), F32),
]


def _diff_kernel(lam_ref, g_ref, q_ref, k_ref, v_ref, o_ref, s_ref, max_ref, acc_ref, *, lam_init, t):
    it = pl.program_id(1)
    lane = lax.broadcasted_iota(jnp.int32, (t, C_VDIM), 1)
    maps = [lane < HEAD_DIM, lane >= HEAD_DIM]
    q_all = q_ref[0]
    qs = [_stack_masked(q_all[:, h * C_VDIM:(h + 1) * C_VDIM], maps) for h in range(C_HEADS)]
    _reset_stats(max_ref, acc_ref)
    causal = jnp.where(_local_causal(t), 0.0, MASKED)
    causal2 = jnp.concatenate([causal, causal], axis=0)

    def scores(j):
        rows = _tile_rows(j, t)
        bias = jnp.where(j == it, causal2, 0.0)
        for h in range(C_HEADS):
            s = _nt_dot(qs[h], k_ref[0, rows, h * C_VDIM:(h + 1) * C_VDIM]) + bias
            _record_scores(s, j, h * 2 * t, s_ref, max_ref)

    _for_each_tile(it + 1, scores, group=4)
    _finish_max(max_ref)

    def probs(j):
        rows = _tile_rows(j, t)
        for h in range(C_HEADS):
            _accumulate_probs(j, h * 2 * t, 2 * t, _with_ones(v_ref[0, rows, h * C_VDIM:(h + 1) * C_VDIM]),
                              s_ref, max_ref, acc_ref)

    _for_each_tile(it + 1, probs, group=4)

    lp = lam_ref[...]
    lam = (jnp.exp(jnp.sum(lp[0:1] * lp[1:2], axis=-1, keepdims=True))
           - jnp.exp(jnp.sum(lp[2:3] * lp[3:4], axis=-1, keepdims=True)) + lam_init)
    for h in range(C_HEADS):
        o = _normalised(h * 2 * t, t, acc_ref) - lam * _normalised(h * 2 * t + t, t, acc_ref)
        o = o * lax.rsqrt(jnp.mean(o * o, axis=-1, keepdims=True) + EPS) * g_ref[...] * (1.0 - lam_init)
        o_ref[0, :, h * C_VDIM:(h + 1) * C_VDIM] = o.astype(o_ref.dtype)


def _diff_attention(qc, kc, vc, lam_p, subln_g, lam_init):
    b, s, w = qc.shape
    t = ATT_TILE
    qspec = pl.BlockSpec((1, t, w), lambda bi, i: (bi, i, 0))
    kvspec = pl.BlockSpec((1, s, w), lambda bi, i: (bi, 0, 0))
    return pl.pallas_call(
        functools.partial(_diff_kernel, lam_init=lam_init, t=t),
        grid=(b, s // t),
        in_specs=[pl.BlockSpec(lam_p.shape, lambda bi, i: (0, 0)),
                  pl.BlockSpec((1, C_VDIM), lambda bi, i: (0, 0)), qspec, kvspec, kvspec],
        out_specs=qspec,
        out_shape=jax.ShapeDtypeStruct((b, s, w), BF16),
        scratch_shapes=_ATT_SCRATCH(s // t, 2 * C_HEADS * t, t, 2 * LANES),
        compiler_params=_params("parallel", "arbitrary"),
        name="diff_attention",
    )(lam_p, subln_g, qc, kc, vc)


def _moba_kernel(q_ref, k_ref, v_ref, o_ref, kmean_hi, kmean_lo, bias_ref, s_ref, max_ref, acc_ref, *, t):
    it = pl.program_id(1)
    n_blk = k_ref.shape[1] // t
    width = q_ref.shape[2]

    @pl.when(it == 0)
    def _():
        kmean_hi[...] = jnp.zeros_like(kmean_hi)
        kmean_lo[...] = jnp.zeros_like(kmean_lo)
        for n in range(n_blk):
            km = jnp.mean(k_ref[0, n * t:(n + 1) * t, :].astype(F32), axis=0, keepdims=True)
            hi = km.astype(BF16)
            kmean_hi[n:n + 1, :] = hi
            kmean_lo[n:n + 1, :] = (km - hi.astype(F32)).astype(BF16)

    heads = _head_lane_masks(t, width)
    q_stack = _stack_masked(q_ref[0], heads)

    no_bias = jnp.zeros(bias_ref.shape[1:], F32)
    bias_ref[it] = no_bias

    @pl.when(it <= MOBA_TOPK)
    def _():
        for n in range(min(MOBA_TOPK, n_blk)):
            @pl.when(n < it)
            def _():
                bias_ref[n] = no_bias

    @pl.when(it > MOBA_TOPK)
    def _():
        blk = lax.broadcasted_iota(jnp.int32, (SUBLANES, t), 0)
        past = blk < it
        pad = jnp.full((LANES - SUBLANES, t), MASKED, F32)
        bias_q = []
        for h in range(B_HEADS):
            qh = q_stack[h * t:(h + 1) * t]
            gate = (_nt_dot(kmean_hi[...], qh) + _nt_dot(kmean_lo[...], qh))[:SUBLANES]
            gate = jnp.where(past, gate, -jnp.inf)
            rank = jnp.zeros((SUBLANES, t), jnp.int32)
            for n in range(n_blk):
                gn = gate[n:n + 1, :]
                rank = rank + jnp.where((gn > gate) | ((gn == gate) & (blk > n)), 1, 0)
            bias = jnp.where((rank < MOBA_TOPK) & past, 0.0, MASKED)
            bias_q.append(jnp.concatenate([bias, pad], axis=0).T.astype(BF16))
        bias_q = jnp.concatenate(bias_q, axis=0)
        lane_of = lax.broadcasted_iota(jnp.int32, (LANES, LANES), 0)
        for n in range(n_blk - 1):
            @pl.when(n < it)
            def _():
                pick = (lane_of == n).astype(BF16)
                bias_ref[n] = _dot(bias_q, pick)

    _reset_stats(max_ref, acc_ref)
    causal = jnp.where(_local_causal(t), 0.0, MASKED)
    causal4 = jnp.concatenate([causal] * B_HEADS, axis=0)

    def scores(j):
        s = _nt_dot(q_stack, k_ref[0, _tile_rows(j, t), :]) + _lane_tile(bias_ref[j], t)
        _record_scores(s + jnp.where(j == it, causal4, 0.0), j, 0, s_ref, max_ref)

    _for_each_tile(it + 1, scores, group=4)
    _finish_max(max_ref)

    def probs(j):
        rows = _tile_rows(j, t)
        for pair in range(B_HEADS // 2):
            _accumulate_probs(j, pair * 2 * t, 2 * t, _with_ones(v_ref[0, rows, pair * LANES:(pair + 1) * LANES]),
                              s_ref, max_ref, acc_ref)

    _for_each_tile(it + 1, probs, group=4)

    o = [_normalised(h * t, t, acc_ref) for h in range(B_HEADS)]
    o_ref[0] = _interleave_pairs(o, t).astype(o_ref.dtype)


def _moba_attention(qb, kb, vb):
    b, s, w = qb.shape
    t = MOBA_BLOCK
    assert s // t <= SUBLANES
    qspec = pl.BlockSpec((1, t, w), lambda bi, i: (bi, i, 0))
    kvspec = pl.BlockSpec((1, s, w), lambda bi, i: (bi, 0, 0))
    return pl.pallas_call(
        functools.partial(_moba_kernel, t=t),
        grid=(b, s // t),
        in_specs=[qspec, kvspec, kvspec],
        out_specs=qspec,
        out_shape=jax.ShapeDtypeStruct((b, s, w), BF16),
        scratch_shapes=[pltpu.VMEM((LANES, w), BF16), pltpu.VMEM((LANES, w), BF16),
                        pltpu.VMEM((s // t, B_HEADS * t, LANES), F32)]
                       + _ATT_SCRATCH(s // t, B_HEADS * t, t, 2 * LANES),
        compiler_params=_params("parallel", "arbitrary"),
        name="moba_attention",
    )(qb, kb, vb)


def _sublane_fold(x, rows):
    parts = [x[r:r + rows] for r in range(0, x.shape[0], rows)]
    while len(parts) > 1:
        parts = [parts[i] + parts[i + 1] for i in range(0, len(parts), 2)]
    return parts[0]


I16 = jnp.int16
I16_ROWS = 2 * SUBLANES
I16_MIN = -(2 ** 15)


def _dsa_kernel(q_ref, k_ref, v_ref, qi_ref, ki_ref, wi_ref, o_ref,
                keys_ref, hi_ref, lo_ref, ties_ref, s_ref, max_ref, acc_ref, *, t, topk):
    it = pl.program_id(1)
    width = q_ref.shape[2]
    heads = _head_lane_masks(t, width)
    causal = _local_causal(t)

    qidx = qi_ref[0]
    qi_stack = jnp.concatenate(
        [_stack_masked(qidx[:, g * width:(g + 1) * width], heads) for g in range(IDX_HEADS // 4)], axis=0)
    wi = wi_ref[0]
    w_rep = [jnp.broadcast_to(wi[:, HEAD_DIM + h:HEAD_DIM + h + 1], (t, LANES)) for h in range(IDX_HEADS)]

    def score_tile(j):
        logit = _nt_dot(qi_stack, ki_ref[0, _tile_rows(j, t), :])
        cols = []
        for c in range(0, t, LANES):
            acc = jnp.zeros((t, LANES), F32)
            for h in range(IDX_HEADS):
                acc = acc + jnp.maximum(logit[h * t:(h + 1) * t, c:c + LANES], 0.0) * w_rep[h]
            cols.append(acc)
        isc = jnp.concatenate(cols, axis=1) + 0.0
        isc = jnp.where(j == it, jnp.where(causal, isc, -jnp.inf), isc)
        bits = lax.bitcast_convert_type(isc, jnp.int32)
        key = bits ^ ((bits >> 31) & 0x7FFFFFFF)
        keys_ref[j] = key
        key_t = key.T
        hi_ref[j] = (key_t >> 16).astype(I16)
        lo_ref[j] = ((key_t & 0xFFFF) + I16_MIN).astype(I16)

    _for_each_tile(it + 1, score_tile, group=4)
    n_pairs = it // 2 + 1

    @pl.when(it % 2 == 0)
    def _():
        hi_ref[it + 1] = jnp.full((t, t), I16_MIN, I16)
        lo_ref[it + 1] = jnp.full((t, t), I16_MIN, I16)

    def count(ref, pred):
        def one(j, cnt):
            return cnt + _sublane_fold(jnp.where(pred(ref[j]), jnp.int16(1), jnp.int16(0)), I16_ROWS)
        cnt = lax.fori_loop(0, n_pairs, lambda i, cnt: one(2 * i + 1, one(2 * i, cnt)), jnp.zeros((I16_ROWS, t), I16))
        return jnp.sum(cnt.astype(jnp.int32), axis=0, keepdims=True)

    def kth_largest(ref, k):
        thr = jnp.where(count(ref, lambda x: x >= jnp.int16(0)) >= k, 0, jnp.full((1, t), I16_MIN, jnp.int32))

        def bit_step(i, thr):
            cand = thr | (1 << (14 - i))
            return jnp.where(count(ref, lambda x: x >= cand.astype(I16)) >= k, cand, thr)

        return lax.fori_loop(0, 15, bit_step, thr)

    thr_hi = kth_largest(hi_ref, topk)
    thr_hi16 = thr_hi.astype(I16)
    k_lo = topk - count(hi_ref, lambda x: x > thr_hi16)

    def keep_matching(j):
        lo_ref[j] = jnp.where(hi_ref[j] == thr_hi16, lo_ref[j], jnp.int16(I16_MIN))

    @pl.loop(0, n_pairs)
    def _(i):
        keep_matching(2 * i)
        keep_matching(2 * i + 1)

    thr_lo = kth_largest(lo_ref, k_lo)
    thr_lo16 = thr_lo.astype(I16)
    wanted = (k_lo - count(lo_ref, lambda x: x > thr_lo16)).astype(F32)
    thr = (thr_hi << 16) | (thr_lo - I16_MIN)
    thr_q = _lane_tile(jnp.broadcast_to(thr, (LANES, t)).T, t)
    wanted_q = _lane_tile(jnp.broadcast_to(wanted, (LANES, t)).T, t)

    q_stack = _stack_masked(q_ref[0], heads)
    r = lax.broadcasted_iota(jnp.int32, (t, t), 0)
    c = lax.broadcasted_iota(jnp.int32, (t, t), 1)
    before = (r < c).astype(BF16)
    ones = jnp.ones((t, LANES), BF16)
    _reset_stats(max_ref, acc_ref)

    ties_ref[...] = jnp.zeros(ties_ref.shape, F32)

    not_causal = jnp.where(causal, 0.0, MASKED)

    def select_tile(j):
        key = keys_ref[j]
        tie = key == thr_q
        tie_b = jnp.where(tie, 1.0, 0.0).astype(BF16)
        tie_rank = _lane_tile(ties_ref[...], t) + _dot(tie_b, before)
        bias = jnp.where(key > thr_q, 0.0, jnp.where(tie, jnp.where(tie_rank < wanted_q, 0.0, MASKED), MASKED))
        bias = bias + jnp.where(j == it, not_causal, 0.0)
        s = _nt_dot(q_stack, k_ref[0, _tile_rows(j, t), :]) + jnp.concatenate([bias] * A_HEADS, axis=0)
        _record_scores(s, j, 0, s_ref, max_ref)
        ties_ref[...] += _dot(tie_b, ones)

    _for_each_tile(it + 1, select_tile)
    _finish_max(max_ref)

    _for_each_tile(it + 1, lambda j: _accumulate_probs(
        j, 0, A_HEADS * t, v_ref[0, _tile_rows(j, t), :], s_ref, max_ref, acc_ref), group=4)

    sums_only = lax.broadcasted_iota(jnp.int32, (t, LANES), 1) >= HEAD_DIM
    o = []
    for h in range(A_HEADS):
        a = acc_ref[h * t:(h + 1) * t, :]
        inv = 1.0 / jnp.where(sums_only, a, 1.0)
        o.append(pltpu.roll(a, HEAD_DIM, 1) * inv if h % 2 else a * pltpu.roll(inv, HEAD_DIM, 1))
    o_ref[0] = _interleave_pairs(o, t).astype(o_ref.dtype)


def _dsa_attention(qa, ka, va, qi, ki, wi):
    b, s, w = qa.shape
    t = ATT_TILE
    topk = min(INDEX_TOPK, s // 4)
    assert t >= topk
    row = lambda width: pl.BlockSpec((1, t, width), lambda bi, i: (bi, i, 0))
    full = lambda width: pl.BlockSpec((1, s, width), lambda bi, i: (bi, 0, 0))
    return pl.pallas_call(
        functools.partial(_dsa_kernel, t=t, topk=topk),
        grid=(b, s // t),
        in_specs=[row(w), full(w), full(va.shape[2]), row(qi.shape[2]), full(w), row(LANES)],
        out_specs=row(w),
        out_shape=jax.ShapeDtypeStruct((b, s, w), BF16),
        scratch_shapes=[pltpu.VMEM((s // t, t, t), jnp.int32),
                        pltpu.VMEM((s // t + s // t % 2, t, t), I16), pltpu.VMEM((s // t + s // t % 2, t, t), I16),
                        pltpu.VMEM((t, LANES), F32)]
                       + _ATT_SCRATCH(s // t, A_HEADS * t, t, LANES),
        compiler_params=_params("parallel", "arbitrary"),
        name="dsa_attention",
    )(qa, ka, va, qi, ki, wi)


def _merge_kernel(x_ref, g_ref, oa_ref, ob_ref, oc_ref, wg_ref, wb_ref, wo_ref, o_ref, *, chunk):
    x = x_ref[...]
    d = x.shape[1]
    h = _rmsnorm_rows(x, g_ref[...]).astype(BF16)
    branches, row0 = [], 0
    for o_ref_i in (oa_ref, ob_ref, oc_ref):
        branches.append((o_ref_i[...], row0))
        row0 += o_ref_i.shape[1]
    merged = []
    for c in range(0, d, chunk):
        m = jnp.zeros((x.shape[0], chunk), F32)
        for i, (o_i, r0) in enumerate(branches):
            gate = jax.nn.sigmoid(_dot(h, wg_ref[:, i * d + c:i * d + c + chunk]))
            m = m + gate * _dot(o_i, wb_ref[r0:r0 + o_i.shape[1], c:c + chunk])
        merged.append(m.astype(BF16))
    o_ref[...] = x + _dot(jnp.concatenate(merged, axis=1), wo_ref[...])


def _merge(x, g, oa, ob, oc, wg, wb, wo, layer, tm):
    n, d = x.shape
    row = lambda w: pl.BlockSpec((tm, w), lambda i: (i, 0))
    return pl.pallas_call(
        functools.partial(_merge_kernel, chunk=MXU_DIM),
        grid=(n // tm,),
        in_specs=[row(d), _resident((1, d)), row(oa.shape[1]), row(ob.shape[1]), row(oc.shape[1]),
                  _resident(wg.shape, (layer,)), _resident(wb.shape, (layer,)), _resident(wo.shape, (layer,))],
        out_specs=row(d),
        out_shape=jax.ShapeDtypeStruct((n, d), F32),
        compiler_params=_params("parallel"),
        name="merge_out",
    )(x, g, oa, ob, oc, wg, wb, wo)


def kernel(x, positions, norm_g, w_in, qk_norm_g, lambda_params, diff_subln_g, w_branch, w_out,
           ffn_w_gate, ffn_w_up, ffn_w_down):
    b, s, d = x.shape
    n = b * s
    depth = norm_g.shape[0]
    tm = min(512, n)
    tm_ffn = min(1024, n)
    assert n % tm_ffn == 0 and s % ATT_TILE == 0 and ATT_TILE == MOBA_BLOCK

    cos, sin = _rope_tables(positions, min(4 * tm, n))
    w_proj, w_gates = _proj_weights(w_in)
    ffn_w = tuple(w.astype(BF16) for w in (ffn_w_gate, ffn_w_up, ffn_w_down))
    w_mix, w_o = w_branch.astype(BF16), w_out.astype(BF16)
    xf = x.reshape(n, d)
    for layer in range(depth):
        lam_init = 0.8 - 0.6 * math.exp(-0.3 * layer)
        xf = _ffn(xf, norm_g[layer, 0][None, :], *ffn_w, (layer, 0), tm_ffn)

        qkg = jnp.tile(qk_norm_g[layer], (1, LANES // HEAD_DIM))
        outs = _proj(xf, norm_g[layer, 1][None, :], qkg, cos, sin, w_proj, layer, tm)
        p = {name: o.reshape(b, s, o.shape[1]) for (name, *_), o in zip(_PROJ_OUTPUTS, outs)}
        o_a = _dsa_attention(p["qa"], p["ka"], p["va"], p["qi"], p["ki"], p["wi"])
        o_b = _moba_attention(p["qb"], p["kb"], p["vb"])
        o_c = _diff_attention(p["qc"], p["kc"], p["vc"], lambda_params[layer], diff_subln_g[layer][None, :], lam_init)
        xf = _merge(xf, norm_g[layer, 1][None, :], o_a.reshape(n, -1), o_b.reshape(n, -1), o_c.reshape(n, -1),
                    w_gates, w_mix, w_o, layer, tm_ffn)

        xf = _ffn(xf, norm_g[layer, 2][None, :], *ffn_w, (layer, 1), tm_ffn)
    return xf.reshape(b, s, d)
```

```python
import functools
import math

import jax
import jax.numpy as jnp
from jax import lax
from jax.experimental import pallas as pl
from jax.experimental.pallas import tpu as pltpu

F32 = jnp.float32
BF16 = jnp.bfloat16

HEAD_DIM = 64
ROT_DIM = HEAD_DIM // 4
ROT_HALF = ROT_DIM // 2
ROPE_THETA = 500000.0
EPS = 1e-6
A_HEADS = 4
IDX_HEADS = 8
IDX_DIM = 64
INDEX_TOPK = 256
B_HEADS = 4
MOBA_BLOCK = 256
MOBA_TOPK = 3
C_HEADS = 4
C_VDIM = 2 * HEAD_DIM

LANES = 128
SUBLANES = 8
MXU_DIM = 256
ATT_TILE = 256
MASKED = -1e30
VMEM_LIMIT = 56 * 1024 * 1024
LOG2E = 1.4426950408889634


def _nt_dot(a, b):
    return lax.dot_general(a, b, (((1,), (1,)), ((), ())), preferred_element_type=F32)


def _dot(a, b):
    return jnp.dot(a, b, preferred_element_type=F32)


def _resident(shape, index=()):
    block = (None,) * len(index) + tuple(shape[len(index):])
    at = tuple(index) + (0,) * (len(shape) - len(index))
    return pl.BlockSpec(block, lambda *_: at, pipeline_mode=pl.Buffered(1))


def _params(*sem):
    return pltpu.CompilerParams(dimension_semantics=sem, vmem_limit_bytes=VMEM_LIMIT)


def _rope_table_kernel(pos_ref, freq_ref, cos_ref, sin_ref):
    ang = pos_ref[...].astype(F32) * freq_ref[...]
    lane = lax.broadcasted_iota(jnp.int32, ang.shape, 1) % HEAD_DIM
    c = jnp.cos(ang)
    s = jnp.sin(ang)
    cos_ref[...] = jnp.where(lane < ROT_DIM, c, 1.0)
    sin_ref[...] = jnp.where(lane < ROT_HALF, -s, jnp.where(lane < ROT_DIM, s, 0.0))


def _rope_tables(positions, tm):
    n = positions.size
    pos = positions.reshape(n, 1)
    inv_freq = jnp.power(ROPE_THETA, -jnp.arange(0, ROT_DIM, 2, dtype=F32) / ROT_DIM)
    freq = jnp.tile(jnp.concatenate([inv_freq, inv_freq, jnp.zeros((HEAD_DIM - ROT_DIM,), F32)]), 2)[None, :]
    return pl.pallas_call(
        _rope_table_kernel,
        grid=(n // tm,),
        in_specs=[pl.BlockSpec((tm, 1), lambda i: (i, 0)), pl.BlockSpec((1, LANES), lambda i: (0, 0))],
        out_specs=[pl.BlockSpec((tm, LANES), lambda i: (i, 0))] * 2,
        out_shape=[jax.ShapeDtypeStruct((n, LANES), F32)] * 2,
        compiler_params=_params("parallel"),
        name="rope_tables",
    )(pos, freq)


def _rmsnorm_rows(x, g):
    return x * lax.rsqrt(jnp.mean(x * x, axis=-1, keepdims=True) + EPS) * g


def _ffn_kernel(x_ref, g_ref, wg_ref, wu_ref, wd_ref, o_ref, *, chunk):
    x = x_ref[...]
    h = _rmsnorm_rows(x, g_ref[...]).astype(BF16)
    acc = jnp.zeros(x.shape, F32)
    for c in range(0, wg_ref.shape[1], chunk):
        a = _dot(h, wg_ref[:, c:c + chunk])
        b = _dot(h, wu_ref[:, c:c + chunk])
        t = (a * jax.nn.sigmoid(a) * b).astype(BF16)
        acc = acc + _dot(t, wd_ref[c:c + chunk, :])
    o_ref[...] = x + 0.5 * acc


def _ffn(x, g, wg, wu, wd, which, tm):
    n, d = x.shape
    return pl.pallas_call(
        functools.partial(_ffn_kernel, chunk=MXU_DIM),
        grid=(n // tm,),
        in_specs=[pl.BlockSpec((tm, d), lambda i: (i, 0)), _resident((1, d)),
                  _resident(wg.shape, which), _resident(wu.shape, which), _resident(wd.shape, which)],
        out_specs=pl.BlockSpec((tm, d), lambda i: (i, 0)),
        out_shape=jax.ShapeDtypeStruct((n, d), F32),
        compiler_params=_params("parallel"),
        name="ffn",
    )(x, g, wg, wu, wd)


_W_IN_SPLITS = (("qa", 256), ("ka", 64), ("va", 64), ("qi", 512), ("ki", 64), ("wi", 8), ("qb", 256), ("kb", 256),
                ("vb", 256), ("qc", 512), ("kc", 512), ("vc", 512), ("ga", 1024), ("gb", 1024), ("gc", 1024))
_PROJ_COLUMNS = (
    ("qa", 256, "norm_rope_scale", 0),
    ("key_value", 128, "key_value", 1),
    ("qi", 512, "rope", None),
    ("index_key_weight", 128, "index_key_weight", None),
    ("qb", 256, "norm_rope_scale", 2),
    ("kb", 256, "norm_rope", 3),
    ("vb", 256, "plain", None),
    ("qc", 512, "norm_rope_scale", 4),
    ("kc", 512, "norm_rope", 5),
    ("vc", 512, "plain", None),
)
_GATE_COLUMNS = 3 * 1024
_PROJ_OUTPUTS = (
    ("qa", 256, BF16), ("ka", 256, BF16), ("va", 128, BF16), ("qi", 512, BF16), ("ki", 256, BF16), ("wi", 128, F32),
    ("qb", 256, BF16), ("kb", 256, BF16), ("vb", 256, BF16), ("qc", 512, BF16), ("kc", 512, BF16), ("vc", 512, BF16),
)
_Q_SCALE = HEAD_DIM ** -0.5 * LOG2E


def _proj_kernel(x_ref, g_ref, qkg_ref, cos_ref, sin_ref, w_ref, *o_refs):
    out = {name: ref for (name, _, _), ref in zip(_PROJ_OUTPUTS, o_refs)}
    tm = x_ref.shape[0]
    h = _rmsnorm_rows(x_ref[...], g_ref[...]).astype(BF16)
    cos = cos_ref[...]
    sin = sin_ref[...]
    lane = lax.broadcasted_iota(jnp.int32, (tm, LANES), 1)
    take_upper = (lane % HEAD_DIM) < ROT_HALF
    low_half = lane < HEAD_DIM
    r = lax.broadcasted_iota(jnp.int32, (LANES, LANES), 0) // HEAD_DIM
    c = lax.broadcasted_iota(jnp.int32, (LANES, LANES), 1) // HEAD_DIM
    same_head = (r == c).astype(BF16)

    def epilogue(y, kind, g_idx):
        if kind in ("norm_rope_scale", "norm_rope"):
            ssq = _dot((y * y).astype(BF16), same_head)
            y = y * lax.rsqrt(ssq * (1.0 / HEAD_DIM) + EPS) * qkg_ref[g_idx:g_idx + 1, :]
        if kind in ("norm_rope_scale", "norm_rope", "rope"):
            partner = jnp.where(take_upper, pltpu.roll(y, LANES - ROT_HALF, 1), pltpu.roll(y, ROT_HALF, 1))
            y = y * cos + partner * sin
        if kind == "norm_rope_scale":
            y = y * _Q_SCALE
        return y

    def on_both_halves(y):
        return jnp.where(low_half, y, pltpu.roll(y, HEAD_DIM, 1))

    col = 0
    for name, width, kind, g_idx in _PROJ_COLUMNS:
        step = min(width, MXU_DIM)
        for c0 in range(0, width, step):
            y = _dot(h, w_ref[:, col + c0:col + c0 + step])
            if kind == "key_value":
                ka = on_both_halves(epilogue(y, "norm_rope", g_idx)).astype(BF16)
                for s0 in range(0, out["ka"].shape[1], LANES):
                    out["ka"][:, s0:s0 + LANES] = ka
                out["va"][...] = jnp.where(low_half, pltpu.roll(y, HEAD_DIM, 1), 1.0).astype(BF16)
                continue
            if kind == "index_key_weight":
                ki = on_both_halves(epilogue(y, "rope", None)).astype(BF16)
                for s0 in range(0, out["ki"].shape[1], LANES):
                    out["ki"][:, s0:s0 + LANES] = ki
                out["wi"][...] = y * ((IDX_HEADS * IDX_DIM) ** -0.5)
                continue
            o_ref = out[name]
            for s0 in range(0, step, LANES):
                o_ref[:, c0 + s0:c0 + s0 + LANES] = epilogue(y[:, s0:s0 + LANES], kind, g_idx).astype(o_ref.dtype)
        col += width


def _proj_weights(w_in):
    w = w_in.astype(BF16)
    names = [name for name, _ in _W_IN_SPLITS]
    end_of_wi = sum(width for _, width in _W_IN_SPLITS[:names.index("wi") + 1])
    gates = w.shape[-1] - _GATE_COLUMNS
    pad = jnp.zeros(w.shape[:-1] + (-end_of_wi % LANES,), w.dtype)
    padded = jnp.concatenate([w[..., :end_of_wi], pad, w[..., end_of_wi:gates]], axis=-1)
    assert padded.shape[-1] == sum(width for _, width, _, _ in _PROJ_COLUMNS)
    return padded, w[..., gates:]


def _proj(x, g, qkg, cos, sin, w, layer, tm):
    n, d = x.shape
    row = lambda width: pl.BlockSpec((tm, width), lambda i: (i, 0))
    return pl.pallas_call(
        _proj_kernel,
        grid=(n // tm,),
        in_specs=[row(d), _resident((1, d)), _resident(qkg.shape), row(LANES), row(LANES),
                  _resident(w.shape, (layer,))],
        out_specs=[row(width) for _, width, _ in _PROJ_OUTPUTS],
        out_shape=[jax.ShapeDtypeStruct((n, width), dtype) for _, width, dtype in _PROJ_OUTPUTS],
        compiler_params=_params("parallel"),
        name="in_proj",
    )(x, g, qkg, cos, sin, w)


def _lane_max(x):
    out = x[:, :LANES]
    for c in range(LANES, x.shape[1], LANES):
        out = jnp.maximum(out, x[:, c:c + LANES])
    return out


def _lane_tile(x, width):
    return jnp.concatenate([x] * (width // LANES), axis=1)


def _reset_stats(max_ref, acc_ref):
    max_ref[...] = jnp.full(max_ref.shape, MASKED, F32)
    acc_ref[...] = jnp.zeros(acc_ref.shape, F32)


def _record_scores(s, j, r0, s_ref, max_ref):
    n = s.shape[0]
    s_ref[j, r0:r0 + n, :] = s
    max_ref[r0:r0 + n, :] = jnp.maximum(max_ref[r0:r0 + n, :], _lane_max(s))


def _finish_max(max_ref):
    m = max_ref[...]
    max_ref[...] = jnp.broadcast_to(jnp.max(m, axis=-1, keepdims=True), m.shape)


def _accumulate_probs(j, r0, n, v, s_ref, max_ref, acc_ref):
    s = s_ref[j, r0:r0 + n, :]
    p = jnp.exp2(s - _lane_tile(max_ref[r0:r0 + n, :], s.shape[1]))
    acc_ref[r0:r0 + n, :] += _dot(p.astype(BF16), v)


def _with_ones(v):
    return jnp.concatenate([v, jnp.ones(v.shape, v.dtype)], axis=1)


def _normalised(r0, n, acc_ref):
    return acc_ref[r0:r0 + n, :LANES] / acc_ref[r0:r0 + n, LANES:]


def _head_lane_masks(rows, width):
    lane = lax.broadcasted_iota(jnp.int32, (rows, width), 1)
    return [(lane >= h * HEAD_DIM) & (lane < (h + 1) * HEAD_DIM) for h in range(width // HEAD_DIM)]


def _stack_masked(q, masks):
    zero = jnp.zeros_like(q)
    return jnp.concatenate([jnp.where(m, q, zero) for m in masks], axis=0)


def _local_causal(t):
    row = lax.broadcasted_iota(jnp.int32, (t, t), 0)
    col = lax.broadcasted_iota(jnp.int32, (t, t), 1)
    return col <= row


def _tile_rows(j, t):
    return pl.ds(pl.multiple_of(j * t, t), t)


def _for_each_tile(n, body, group=2):
    @pl.loop(0, n // group)
    def _(i):
        for k in range(group):
            body(group * i + k)

    part = group // 2
    while part >= 1:
        @pl.when(n % (2 * part) >= part)
        def _():
            for k in range(part):
                body(n - n % (2 * part) + k)
        part //= 2


def _interleave_pairs(o, t):
    lane = lax.broadcasted_iota(jnp.int32, (t, LANES), 1)
    low = lane < HEAD_DIM
    return jnp.concatenate([jnp.where(low, o[0], o[1]), jnp.where(low, o[2], o[3])], axis=1)


_ATT_SCRATCH = lambda n_tiles, rows, t, acc_width: [
    pltpu.VMEM((n_tiles, rows, t), F32),
    pltpu.VMEM((rows, LANES), F32),
    pltpu.VMEM((rows, acc_width), F32),
]


def _diff_kernel(lam_ref, g_ref, q_ref, k_ref, v_ref, o_ref, s_ref, max_ref, acc_ref, *, lam_init, t):
    it = pl.program_id(1)
    lane = lax.broadcasted_iota(jnp.int32, (t, C_VDIM), 1)
    maps = [lane < HEAD_DIM, lane >= HEAD_DIM]
    q_all = q_ref[0]
    qs = [_stack_masked(q_all[:, h * C_VDIM:(h + 1) * C_VDIM], maps) for h in range(C_HEADS)]
    _reset_stats(max_ref, acc_ref)
    causal = jnp.where(_local_causal(t), 0.0, MASKED)
    causal2 = jnp.concatenate([causal, causal], axis=0)

    def scores(j):
        rows = _tile_rows(j, t)
        bias = jnp.where(j == it, causal2, 0.0)
        for h in range(C_HEADS):
            s = _nt_dot(qs[h], k_ref[0, rows, h * C_VDIM:(h + 1) * C_VDIM]) + bias
            _record_scores(s, j, h * 2 * t, s_ref, max_ref)

    _for_each_tile(it + 1, scores, group=4)
    _finish_max(max_ref)

    def probs(j):
        rows = _tile_rows(j, t)
        for h in range(C_HEADS):
            _accumulate_probs(j, h * 2 * t, 2 * t, _with_ones(v_ref[0, rows, h * C_VDIM:(h + 1) * C_VDIM]),
                              s_ref, max_ref, acc_ref)

    _for_each_tile(it + 1, probs, group=4)

    lp = lam_ref[...]
    lam = (jnp.exp(jnp.sum(lp[0:1] * lp[1:2], axis=-1, keepdims=True))
           - jnp.exp(jnp.sum(lp[2:3] * lp[3:4], axis=-1, keepdims=True)) + lam_init)
    for h in range(C_HEADS):
        o = _normalised(h * 2 * t, t, acc_ref) - lam * _normalised(h * 2 * t + t, t, acc_ref)
        o = o * lax.rsqrt(jnp.mean(o * o, axis=-1, keepdims=True) + EPS) * g_ref[...] * (1.0 - lam_init)
        o_ref[0, :, h * C_VDIM:(h + 1) * C_VDIM] = o.astype(o_ref.dtype)


def _diff_attention(qc, kc, vc, lam_p, subln_g, lam_init):
    b, s, w = qc.shape
    t = ATT_TILE
    qspec = pl.BlockSpec((1, t, w), lambda bi, i: (bi, i, 0))
    kvspec = pl.BlockSpec((1, s, w), lambda bi, i: (bi, 0, 0))
    return pl.pallas_call(
        functools.partial(_diff_kernel, lam_init=lam_init, t=t),
        grid=(b, s // t),
        in_specs=[pl.BlockSpec(lam_p.shape, lambda bi, i: (0, 0)),
                  pl.BlockSpec((1, C_VDIM), lambda bi, i: (0, 0)), qspec, kvspec, kvspec],
        out_specs=qspec,
        out_shape=jax.ShapeDtypeStruct((b, s, w), BF16),
        scratch_shapes=_ATT_SCRATCH(s // t, 2 * C_HEADS * t, t, 2 * LANES),
        compiler_params=_params("parallel", "arbitrary"),
        name="diff_attention",
    )(lam_p, subln_g, qc, kc, vc)


def _moba_kernel(q_ref, k_ref, v_ref, o_ref, kmean_hi, kmean_lo, bias_ref, s_ref, max_ref, acc_ref, *, t):
    it = pl.program_id(1)
    n_blk = k_ref.shape[1] // t
    width = q_ref.shape[2]

    @pl.when(it == 0)
    def _():
        kmean_hi[...] = jnp.zeros_like(kmean_hi)
        kmean_lo[...] = jnp.zeros_like(kmean_lo)
        for n in range(n_blk):
            km = jnp.mean(k_ref[0, n * t:(n + 1) * t, :].astype(F32), axis=0, keepdims=True)
            hi = km.astype(BF16)
            kmean_hi[n:n + 1, :] = hi
            kmean_lo[n:n + 1, :] = (km - hi.astype(F32)).astype(BF16)

    heads = _head_lane_masks(t, width)
    q_stack = _stack_masked(q_ref[0], heads)

    no_bias = jnp.zeros(bias_ref.shape[1:], F32)
    bias_ref[it] = no_bias

    @pl.when(it <= MOBA_TOPK)
    def _():
        for n in range(min(MOBA_TOPK, n_blk)):
            @pl.when(n < it)
            def _():
                bias_ref[n] = no_bias

    @pl.when(it > MOBA_TOPK)
    def _():
        blk = lax.broadcasted_iota(jnp.int32, (SUBLANES, t), 0)
        past = blk < it
        pad = jnp.full((LANES - SUBLANES, t), MASKED, F32)
        bias_q = []
        for h in range(B_HEADS):
            qh = q_stack[h * t:(h + 1) * t]
            gate = (_nt_dot(kmean_hi[...], qh) + _nt_dot(kmean_lo[...], qh))[:SUBLANES]
            gate = jnp.where(past, gate, -jnp.inf)
            rank = jnp.zeros((SUBLANES, t), jnp.int32)
            for n in range(n_blk):
                gn = gate[n:n + 1, :]
                rank = rank + jnp.where((gn > gate) | ((gn == gate) & (blk > n)), 1, 0)
            bias = jnp.where((rank < MOBA_TOPK) & past, 0.0, MASKED)
            bias_q.append(jnp.concatenate([bias, pad], axis=0).T.astype(BF16))
        bias_q = jnp.concatenate(bias_q, axis=0)
        lane_of = lax.broadcasted_iota(jnp.int32, (LANES, LANES), 0)
        for n in range(n_blk - 1):
            @pl.when(n < it)
            def _():
                pick = (lane_of == n).astype(BF16)
                bias_ref[n] = _dot(bias_q, pick)

    _reset_stats(max_ref, acc_ref)
    causal = jnp.where(_local_causal(t), 0.0, MASKED)
    causal4 = jnp.concatenate([causal] * B_HEADS, axis=0)

    def scores(j):
        s = _nt_dot(q_stack, k_ref[0, _tile_rows(j, t), :]) + _lane_tile(bias_ref[j], t)
        _record_scores(s + jnp.where(j == it, causal4, 0.0), j, 0, s_ref, max_ref)

    _for_each_tile(it + 1, scores, group=4)
    _finish_max(max_ref)

    def probs(j):
        rows = _tile_rows(j, t)
        for pair in range(B_HEADS // 2):
            _accumulate_probs(j, pair * 2 * t, 2 * t, _with_ones(v_ref[0, rows, pair * LANES:(pair + 1) * LANES]),
                              s_ref, max_ref, acc_ref)

    _for_each_tile(it + 1, probs, group=4)

    o = [_normalised(h * t, t, acc_ref) for h in range(B_HEADS)]
    o_ref[0] = _interleave_pairs(o, t).astype(o_ref.dtype)


def _moba_attention(qb, kb, vb):
    b, s, w = qb.shape
    t = MOBA_BLOCK
    assert s // t <= SUBLANES
    qspec = pl.BlockSpec((1, t, w), lambda bi, i: (bi, i, 0))
    kvspec = pl.BlockSpec((1, s, w), lambda bi, i: (bi, 0, 0))
    return pl.pallas_call(
        functools.partial(_moba_kernel, t=t),
        grid=(b, s // t),
        in_specs=[qspec, kvspec, kvspec],
        out_specs=qspec,
        out_shape=jax.ShapeDtypeStruct((b, s, w), BF16),
        scratch_shapes=[pltpu.VMEM((LANES, w), BF16), pltpu.VMEM((LANES, w), BF16),
                        pltpu.VMEM((s // t, B_HEADS * t, LANES), F32)]
                       + _ATT_SCRATCH(s // t, B_HEADS * t, t, 2 * LANES),
        compiler_params=_params("parallel", "arbitrary"),
        name="moba_attention",
    )(qb, kb, vb)


def _sublane_fold(x, rows):
    parts = [x[r:r + rows] for r in range(0, x.shape[0], rows)]
    while len(parts) > 1:
        parts = [parts[i] + parts[i + 1] for i in range(0, len(parts), 2)]
    return parts[0]


I16 = jnp.int16
I16_ROWS = 2 * SUBLANES
I16_MIN = -(2 ** 15)


def _dsa_kernel(q_ref, k_ref, v_ref, qi_ref, ki_ref, wi_ref, o_ref,
                keys_ref, hi_ref, lo_ref, ties_ref, s_ref, max_ref, acc_ref, *, t, topk):
    it = pl.program_id(1)
    width = q_ref.shape[2]
    heads = _head_lane_masks(t, width)
    causal = _local_causal(t)

    qidx = qi_ref[0]
    qi_stack = jnp.concatenate(
        [_stack_masked(qidx[:, g * width:(g + 1) * width], heads) for g in range(IDX_HEADS // 4)], axis=0)
    wi = wi_ref[0]
    w_rep = [jnp.broadcast_to(wi[:, HEAD_DIM + h:HEAD_DIM + h + 1], (t, LANES)) for h in range(IDX_HEADS)]

    def score_tile(j):
        logit = _nt_dot(qi_stack, ki_ref[0, _tile_rows(j, t), :])
        cols = []
        for c in range(0, t, LANES):
            acc = jnp.zeros((t, LANES), F32)
            for h in range(IDX_HEADS):
                acc = acc + jnp.maximum(logit[h * t:(h + 1) * t, c:c + LANES], 0.0) * w_rep[h]
            cols.append(acc)
        isc = jnp.concatenate(cols, axis=1) + 0.0
        isc = jnp.where(j == it, jnp.where(causal, isc, -jnp.inf), isc)
        bits = lax.bitcast_convert_type(isc, jnp.int32)
        key = bits ^ ((bits >> 31) & 0x7FFFFFFF)
        keys_ref[j] = key
        key_t = key.T
        hi_ref[j] = (key_t >> 16).astype(I16)
        lo_ref[j] = ((key_t & 0xFFFF) + I16_MIN).astype(I16)

    _for_each_tile(it + 1, score_tile, group=4)
    n_pairs = it // 2 + 1

    @pl.when(it % 2 == 0)
    def _():
        hi_ref[it + 1] = jnp.full((t, t), I16_MIN, I16)
        lo_ref[it + 1] = jnp.full((t, t), I16_MIN, I16)

    def count(ref, pred):
        def one(j, cnt):
            return cnt + _sublane_fold(jnp.where(pred(ref[j]), jnp.int16(1), jnp.int16(0)), I16_ROWS)
        cnt = lax.fori_loop(0, n_pairs, lambda i, cnt: one(2 * i + 1, one(2 * i, cnt)), jnp.zeros((I16_ROWS, t), I16))
        return jnp.sum(cnt.astype(jnp.int32), axis=0, keepdims=True)

    def kth_largest(ref, k):
        thr = jnp.where(count(ref, lambda x: x >= jnp.int16(0)) >= k, 0, jnp.full((1, t), I16_MIN, jnp.int32))

        def bit_step(i, thr):
            cand = thr | (1 << (14 - i))
            return jnp.where(count(ref, lambda x: x >= cand.astype(I16)) >= k, cand, thr)

        return lax.fori_loop(0, 15, bit_step, thr)

    thr_hi = kth_largest(hi_ref, topk)
    thr_hi16 = thr_hi.astype(I16)
    k_lo = topk - count(hi_ref, lambda x: x > thr_hi16)

    def keep_matching(j):
        lo_ref[j] = jnp.where(hi_ref[j] == thr_hi16, lo_ref[j], jnp.int16(I16_MIN))

    @pl.loop(0, n_pairs)
    def _(i):
        keep_matching(2 * i)
        keep_matching(2 * i + 1)

    thr_lo = kth_largest(lo_ref, k_lo)
    thr_lo16 = thr_lo.astype(I16)
    wanted = (k_lo - count(lo_ref, lambda x: x > thr_lo16)).astype(F32)
    thr = (thr_hi << 16) | (thr_lo - I16_MIN)
    thr_q = _lane_tile(jnp.broadcast_to(thr, (LANES, t)).T, t)
    wanted_q = _lane_tile(jnp.broadcast_to(wanted, (LANES, t)).T, t)

    q_stack = _stack_masked(q_ref[0], heads)
    r = lax.broadcasted_iota(jnp.int32, (t, t), 0)
    c = lax.broadcasted_iota(jnp.int32, (t, t), 1)
    before = (r < c).astype(BF16)
    ones = jnp.ones((t, LANES), BF16)
    _reset_stats(max_ref, acc_ref)

    ties_ref[...] = jnp.zeros(ties_ref.shape, F32)

    not_causal = jnp.where(causal, 0.0, MASKED)

    def select_tile(j):
        key = keys_ref[j]
        tie = key == thr_q
        tie_b = jnp.where(tie, 1.0, 0.0).astype(BF16)
        tie_rank = _lane_tile(ties_ref[...], t) + _dot(tie_b, before)
        bias = jnp.where(key > thr_q, 0.0, jnp.where(tie, jnp.where(tie_rank < wanted_q, 0.0, MASKED), MASKED))
        bias = bias + jnp.where(j == it, not_causal, 0.0)
        s = _nt_dot(q_stack, k_ref[0, _tile_rows(j, t), :]) + jnp.concatenate([bias] * A_HEADS, axis=0)
        _record_scores(s, j, 0, s_ref, max_ref)
        ties_ref[...] += _dot(tie_b, ones)

    _for_each_tile(it + 1, select_tile)
    _finish_max(max_ref)

    _for_each_tile(it + 1, lambda j: _accumulate_probs(
        j, 0, A_HEADS * t, v_ref[0, _tile_rows(j, t), :], s_ref, max_ref, acc_ref), group=4)

    sums_only = lax.broadcasted_iota(jnp.int32, (t, LANES), 1) >= HEAD_DIM
    o = []
    for h in range(A_HEADS):
        a = acc_ref[h * t:(h + 1) * t, :]
        inv = 1.0 / jnp.where(sums_only, a, 1.0)
        o.append(pltpu.roll(a, HEAD_DIM, 1) * inv if h % 2 else a * pltpu.roll(inv, HEAD_DIM, 1))
    o_ref[0] = _interleave_pairs(o, t).astype(o_ref.dtype)


def _dsa_attention(qa, ka, va, qi, ki, wi):
    b, s, w = qa.shape
    t = ATT_TILE
    topk = min(INDEX_TOPK, s // 4)
    assert t >= topk
    row = lambda width: pl.BlockSpec((1, t, width), lambda bi, i: (bi, i, 0))
    full = lambda width: pl.BlockSpec((1, s, width), lambda bi, i: (bi, 0, 0))
    return pl.pallas_call(
        functools.partial(_dsa_kernel, t=t, topk=topk),
        grid=(b, s // t),
        in_specs=[row(w), full(w), full(va.shape[2]), row(qi.shape[2]), full(w), row(LANES)],
        out_specs=row(w),
        out_shape=jax.ShapeDtypeStruct((b, s, w), BF16),
        scratch_shapes=[pltpu.VMEM((s // t, t, t), jnp.int32),
                        pltpu.VMEM((s // t + s // t % 2, t, t), I16), pltpu.VMEM((s // t + s // t % 2, t, t), I16),
                        pltpu.VMEM((t, LANES), F32)]
                       + _ATT_SCRATCH(s // t, A_HEADS * t, t, LANES),
        compiler_params=_params("parallel", "arbitrary"),
        name="dsa_attention",
    )(qa, ka, va, qi, ki, wi)


def _merge_kernel(x_ref, g_ref, oa_ref, ob_ref, oc_ref, wg_ref, wb_ref, wo_ref, o_ref, *, chunk):
    x = x_ref[...]
    d = x.shape[1]
    h = _rmsnorm_rows(x, g_ref[...]).astype(BF16)
    branches, row0 = [], 0
    for o_ref_i in (oa_ref, ob_ref, oc_ref):
        branches.append((o_ref_i[...], row0))
        row0 += o_ref_i.shape[1]
    merged = []
    for c in range(0, d, chunk):
        m = jnp.zeros((x.shape[0], chunk), F32)
        for i, (o_i, r0) in enumerate(branches):
            gate = jax.nn.sigmoid(_dot(h, wg_ref[:, i * d + c:i * d + c + chunk]))
            m = m + gate * _dot(o_i, wb_ref[r0:r0 + o_i.shape[1], c:c + chunk])
        merged.append(m.astype(BF16))
    o_ref[...] = x + _dot(jnp.concatenate(merged, axis=1), wo_ref[...])


def _merge(x, g, oa, ob, oc, wg, wb, wo, layer, tm):
    n, d = x.shape
    row = lambda w: pl.BlockSpec((tm, w), lambda i: (i, 0))
    return pl.pallas_call(
        functools.partial(_merge_kernel, chunk=MXU_DIM),
        grid=(n // tm,),
        in_specs=[row(d), _resident((1, d)), row(oa.shape[1]), row(ob.shape[1]), row(oc.shape[1]),
                  _resident(wg.shape, (layer,)), _resident(wb.shape, (layer,)), _resident(wo.shape, (layer,))],
        out_specs=row(d),
        out_shape=jax.ShapeDtypeStruct((n, d), F32),
        compiler_params=_params("parallel"),
        name="merge_out",
    )(x, g, oa, ob, oc, wg, wb, wo)


def kernel(x, positions, norm_g, w_in, qk_norm_g, lambda_params, diff_subln_g, w_branch, w_out,
           ffn_w_gate, ffn_w_up, ffn_w_down):
    b, s, d = x.shape
    n = b * s
    depth = norm_g.shape[0]
    tm = min(1024, n)
    assert n % tm == 0 and s % ATT_TILE == 0 and ATT_TILE == MOBA_BLOCK

    cos, sin = _rope_tables(positions, min(2 * tm, n))
    w_proj, w_gates = _proj_weights(w_in)
    ffn_w = tuple(w.astype(BF16) for w in (ffn_w_gate, ffn_w_up, ffn_w_down))
    w_mix, w_o = w_branch.astype(BF16), w_out.astype(BF16)
    xf = x.reshape(n, d)
    for layer in range(depth):
        lam_init = 0.8 - 0.6 * math.exp(-0.3 * layer)
        xf = _ffn(xf, norm_g[layer, 0][None, :], *ffn_w, (layer, 0), tm)

        qkg = jnp.tile(qk_norm_g[layer], (1, LANES // HEAD_DIM))
        outs = _proj(xf, norm_g[layer, 1][None, :], qkg, cos, sin, w_proj, layer, tm)
        p = {name: o.reshape(b, s, o.shape[1]) for (name, *_), o in zip(_PROJ_OUTPUTS, outs)}
        o_a = _dsa_attention(p["qa"], p["ka"], p["va"], p["qi"], p["ki"], p["wi"])
        o_b = _moba_attention(p["qb"], p["kb"], p["vb"])
        o_c = _diff_attention(p["qc"], p["kc"], p["vc"], lambda_params[layer], diff_subln_g[layer][None, :], lam_init)
        xf = _merge(xf, norm_g[layer, 1][None, :], o_a.reshape(n, -1), o_b.reshape(n, -1), o_c.reshape(n, -1),
                    w_gates, w_mix, w_o, layer, tm)

        xf = _ffn(xf, norm_g[layer, 2][None, :], *ffn_w, (layer, 1), tm)
    return xf.reshape(b, s, d)
```

```python
import functools
import math

import jax
import jax.numpy as jnp
from jax import lax
from jax.experimental import pallas as pl
from jax.experimental.pallas import tpu as pltpu

F32 = jnp.float32
BF16 = jnp.bfloat16

HEAD_DIM = 64
ROT_DIM = HEAD_DIM // 4
ROT_HALF = ROT_DIM // 2
ROPE_THETA = 500000.0
EPS = 1e-6
A_HEADS = 4
IDX_HEADS = 8
IDX_DIM = 64
INDEX_TOPK = 256
B_HEADS = 4
MOBA_BLOCK = 256
MOBA_TOPK = 3
C_HEADS = 4
C_VDIM = 2 * HEAD_DIM

LANES = 128
SUBLANES = 8
MXU_DIM = 256
ATT_TILE = 256
MASKED = -1e30
VMEM_LIMIT = 56 * 1024 * 1024
LOG2E = 1.4426950408889634


def _nt_dot(a, b):
    return lax.dot_general(a, b, (((1,), (1,)), ((), ())), preferred_element_type=F32)


def _dot(a, b):
    return jnp.dot(a, b, preferred_element_type=F32)


def _resident(shape, index=()):
    block = (None,) * len(index) + tuple(shape[len(index):])
    at = tuple(index) + (0,) * (len(shape) - len(index))
    return pl.BlockSpec(block, lambda *_: at, pipeline_mode=pl.Buffered(1))


def _params(*sem):
    return pltpu.CompilerParams(dimension_semantics=sem, vmem_limit_bytes=VMEM_LIMIT)


def _rope_table_kernel(pos_ref, freq_ref, cos_ref, sin_ref):
    ang = pos_ref[...].astype(F32) * freq_ref[...]
    lane = lax.broadcasted_iota(jnp.int32, ang.shape, 1) % HEAD_DIM
    c = jnp.cos(ang)
    s = jnp.sin(ang)
    cos_ref[...] = jnp.where(lane < ROT_DIM, c, 1.0)
    sin_ref[...] = jnp.where(lane < ROT_HALF, -s, jnp.where(lane < ROT_DIM, s, 0.0))


def _rope_tables(positions, tm):
    n = positions.size
    pos = positions.reshape(n, 1)
    inv_freq = jnp.power(ROPE_THETA, -jnp.arange(0, ROT_DIM, 2, dtype=F32) / ROT_DIM)
    freq = jnp.tile(jnp.concatenate([inv_freq, inv_freq, jnp.zeros((HEAD_DIM - ROT_DIM,), F32)]), 2)[None, :]
    return pl.pallas_call(
        _rope_table_kernel,
        grid=(n // tm,),
        in_specs=[pl.BlockSpec((tm, 1), lambda i: (i, 0)), pl.BlockSpec((1, LANES), lambda i: (0, 0))],
        out_specs=[pl.BlockSpec((tm, LANES), lambda i: (i, 0))] * 2,
        out_shape=[jax.ShapeDtypeStruct((n, LANES), F32)] * 2,
        compiler_params=_params("parallel"),
        name="rope_tables",
    )(pos, freq)


def _rmsnorm_rows(x, g):
    return x * lax.rsqrt(jnp.mean(x * x, axis=-1, keepdims=True) + EPS) * g


def _ffn_kernel(x_ref, g_ref, wg_ref, wu_ref, wd_ref, o_ref, *, chunk):
    x = x_ref[...]
    h = _rmsnorm_rows(x, g_ref[...]).astype(BF16)
    acc = jnp.zeros(x.shape, F32)
    for c in range(0, wg_ref.shape[1], chunk):
        a = _dot(h, wg_ref[:, c:c + chunk])
        b = _dot(h, wu_ref[:, c:c + chunk])
        t = (a * jax.nn.sigmoid(a) * b).astype(BF16)
        acc = acc + _dot(t, wd_ref[c:c + chunk, :])
    o_ref[...] = x + 0.5 * acc


def _ffn(x, g, wg, wu, wd, which, tm):
    n, d = x.shape
    return pl.pallas_call(
        functools.partial(_ffn_kernel, chunk=MXU_DIM),
        grid=(n // tm,),
        in_specs=[pl.BlockSpec((tm, d), lambda i: (i, 0)), _resident((1, d)),
                  _resident(wg.shape, which), _resident(wu.shape, which), _resident(wd.shape, which)],
        out_specs=pl.BlockSpec((tm, d), lambda i: (i, 0)),
        out_shape=jax.ShapeDtypeStruct((n, d), F32),
        compiler_params=_params("parallel"),
        name="ffn",
    )(x, g, wg, wu, wd)


_W_IN_SPLITS = (("qa", 256), ("ka", 64), ("va", 64), ("qi", 512), ("ki", 64), ("wi", 8), ("qb", 256), ("kb", 256),
                ("vb", 256), ("qc", 512), ("kc", 512), ("vc", 512), ("ga", 1024), ("gb", 1024), ("gc", 1024))
_PROJ_COLUMNS = (
    ("qa", 256, "norm_rope_scale", 0),
    ("key_value", 128, "key_value", 1),
    ("qi", 512, "rope", None),
    ("index_key_weight", 128, "index_key_weight", None),
    ("qb", 256, "norm_rope_scale", 2),
    ("kb", 256, "norm_rope", 3),
    ("vb", 256, "plain", None),
    ("qc", 512, "norm_rope_scale", 4),
    ("kc", 512, "norm_rope", 5),
    ("vc", 512, "plain", None),
)
_GATE_COLUMNS = 3 * 1024
_PROJ_OUTPUTS = (
    ("qa", 256, BF16), ("ka", 256, BF16), ("va", 128, BF16), ("qi", 512, BF16), ("ki", 256, BF16), ("wi", 128, F32),
    ("qb", 256, BF16), ("kb", 256, BF16), ("vb", 256, BF16), ("qc", 512, BF16), ("kc", 512, BF16), ("vc", 512, BF16),
)
_Q_SCALE = HEAD_DIM ** -0.5 * LOG2E


def _proj_kernel(x_ref, g_ref, qkg_ref, cos_ref, sin_ref, w_ref, *o_refs):
    out = {name: ref for (name, _, _), ref in zip(_PROJ_OUTPUTS, o_refs)}
    tm = x_ref.shape[0]
    h = _rmsnorm_rows(x_ref[...], g_ref[...]).astype(BF16)
    cos = cos_ref[...]
    sin = sin_ref[...]
    lane = lax.broadcasted_iota(jnp.int32, (tm, LANES), 1)
    take_upper = (lane % HEAD_DIM) < ROT_HALF
    low_half = lane < HEAD_DIM
    r = lax.broadcasted_iota(jnp.int32, (LANES, LANES), 0) // HEAD_DIM
    c = lax.broadcasted_iota(jnp.int32, (LANES, LANES), 1) // HEAD_DIM
    same_head = (r == c).astype(BF16)

    def epilogue(y, kind, g_idx):
        if kind in ("norm_rope_scale", "norm_rope"):
            ssq = _dot((y * y).astype(BF16), same_head)
            y = y * lax.rsqrt(ssq * (1.0 / HEAD_DIM) + EPS) * qkg_ref[g_idx:g_idx + 1, :]
        if kind in ("norm_rope_scale", "norm_rope", "rope"):
            partner = jnp.where(take_upper, pltpu.roll(y, LANES - ROT_HALF, 1), pltpu.roll(y, ROT_HALF, 1))
            y = y * cos + partner * sin
        if kind == "norm_rope_scale":
            y = y * _Q_SCALE
        return y

    def on_both_halves(y):
        return jnp.where(low_half, y, pltpu.roll(y, HEAD_DIM, 1))

    col = 0
    for name, width, kind, g_idx in _PROJ_COLUMNS:
        step = min(width, MXU_DIM)
        for c0 in range(0, width, step):
            y = _dot(h, w_ref[:, col + c0:col + c0 + step])
            if kind == "key_value":
                ka = on_both_halves(epilogue(y, "norm_rope", g_idx)).astype(BF16)
                for s0 in range(0, out["ka"].shape[1], LANES):
                    out["ka"][:, s0:s0 + LANES] = ka
                out["va"][...] = jnp.where(low_half, pltpu.roll(y, HEAD_DIM, 1), 1.0).astype(BF16)
                continue
            if kind == "index_key_weight":
                ki = on_both_halves(epilogue(y, "rope", None)).astype(BF16)
                for s0 in range(0, out["ki"].shape[1], LANES):
                    out["ki"][:, s0:s0 + LANES] = ki
                out["wi"][...] = y * ((IDX_HEADS * IDX_DIM) ** -0.5)
                continue
            o_ref = out[name]
            for s0 in range(0, step, LANES):
                o_ref[:, c0 + s0:c0 + s0 + LANES] = epilogue(y[:, s0:s0 + LANES], kind, g_idx).astype(o_ref.dtype)
        col += width


def _proj_weights(w_in):
    w = w_in.astype(BF16)
    names = [name for name, _ in _W_IN_SPLITS]
    end_of_wi = sum(width for _, width in _W_IN_SPLITS[:names.index("wi") + 1])
    gates = w.shape[-1] - _GATE_COLUMNS
    pad = jnp.zeros(w.shape[:-1] + (-end_of_wi % LANES,), w.dtype)
    padded = jnp.concatenate([w[..., :end_of_wi], pad, w[..., end_of_wi:gates]], axis=-1)
    assert padded.shape[-1] == sum(width for _, width, _, _ in _PROJ_COLUMNS)
    return padded, w[..., gates:]


def _proj(x, g, qkg, cos, sin, w, layer, tm):
    n, d = x.shape
    row = lambda width: pl.BlockSpec((tm, width), lambda i: (i, 0))
    return pl.pallas_call(
        _proj_kernel,
        grid=(n // tm,),
        in_specs=[row(d), _resident((1, d)), _resident(qkg.shape), row(LANES), row(LANES),
                  _resident(w.shape, (layer,))],
        out_specs=[row(width) for _, width, _ in _PROJ_OUTPUTS],
        out_shape=[jax.ShapeDtypeStruct((n, width), dtype) for _, width, dtype in _PROJ_OUTPUTS],
        compiler_params=_params("parallel"),
        name="in_proj",
    )(x, g, qkg, cos, sin, w)


def _lane_max(x):
    out = x[:, :LANES]
    for c in range(LANES, x.shape[1], LANES):
        out = jnp.maximum(out, x[:, c:c + LANES])
    return out


def _lane_tile(x, width):
    return jnp.concatenate([x] * (width // LANES), axis=1)


def _reset_stats(max_ref, acc_ref):
    max_ref[...] = jnp.full(max_ref.shape, MASKED, F32)
    acc_ref[...] = jnp.zeros(acc_ref.shape, F32)


def _record_scores(s, j, r0, s_ref, max_ref):
    n = s.shape[0]
    s_ref[j, r0:r0 + n, :] = s
    max_ref[r0:r0 + n, :] = jnp.maximum(max_ref[r0:r0 + n, :], _lane_max(s))


def _finish_max(max_ref):
    m = max_ref[...]
    max_ref[...] = jnp.broadcast_to(jnp.max(m, axis=-1, keepdims=True), m.shape)


def _accumulate_probs(j, r0, n, v, s_ref, max_ref, acc_ref):
    s = s_ref[j, r0:r0 + n, :]
    p = jnp.exp2(s - _lane_tile(max_ref[r0:r0 + n, :], s.shape[1]))
    acc_ref[r0:r0 + n, :] += _dot(p.astype(BF16), v)


def _with_ones(v):
    return jnp.concatenate([v, jnp.ones(v.shape, v.dtype)], axis=1)


def _normalised(r0, n, acc_ref):
    return acc_ref[r0:r0 + n, :LANES] / acc_ref[r0:r0 + n, LANES:]


def _head_lane_masks(rows, width):
    lane = lax.broadcasted_iota(jnp.int32, (rows, width), 1)
    return [(lane >= h * HEAD_DIM) & (lane < (h + 1) * HEAD_DIM) for h in range(width // HEAD_DIM)]


def _stack_masked(q, masks):
    zero = jnp.zeros_like(q)
    return jnp.concatenate([jnp.where(m, q, zero) for m in masks], axis=0)


def _local_causal(t):
    row = lax.broadcasted_iota(jnp.int32, (t, t), 0)
    col = lax.broadcasted_iota(jnp.int32, (t, t), 1)
    return col <= row


def _tile_rows(j, t):
    return pl.ds(pl.multiple_of(j * t, t), t)


def _for_each_tile(n, body, group=2):
    @pl.loop(0, n // group)
    def _(i):
        for k in range(group):
            body(group * i + k)

    part = group // 2
    while part >= 1:
        @pl.when(n % (2 * part) >= part)
        def _():
            for k in range(part):
                body(n - n % (2 * part) + k)
        part //= 2


def _interleave_pairs(o, t):
    lane = lax.broadcasted_iota(jnp.int32, (t, LANES), 1)
    low = lane < HEAD_DIM
    return jnp.concatenate([jnp.where(low, o[0], o[1]), jnp.where(low, o[2], o[3])], axis=1)


_ATT_SCRATCH = lambda n_tiles, rows, t, acc_width: [
    pltpu.VMEM((n_tiles, rows, t), F32),
    pltpu.VMEM((rows, LANES), F32),
    pltpu.VMEM((rows, acc_width), F32),
]


def _diff_kernel(lam_ref, g_ref, q_ref, k_ref, v_ref, o_ref, s_ref, max_ref, acc_ref, *, lam_init, t):
    it = pl.program_id(1)
    lane = lax.broadcasted_iota(jnp.int32, (t, C_VDIM), 1)
    maps = [lane < HEAD_DIM, lane >= HEAD_DIM]
    q_all = q_ref[0]
    qs = [_stack_masked(q_all[:, h * C_VDIM:(h + 1) * C_VDIM], maps) for h in range(C_HEADS)]
    _reset_stats(max_ref, acc_ref)
    causal = jnp.where(_local_causal(t), 0.0, MASKED)
    causal2 = jnp.concatenate([causal, causal], axis=0)

    def scores(j):
        rows = _tile_rows(j, t)
        bias = jnp.where(j == it, causal2, 0.0)
        for h in range(C_HEADS):
            s = _nt_dot(qs[h], k_ref[0, rows, h * C_VDIM:(h + 1) * C_VDIM]) + bias
            _record_scores(s, j, h * 2 * t, s_ref, max_ref)

    _for_each_tile(it + 1, scores, group=4)
    _finish_max(max_ref)

    def probs(j):
        rows = _tile_rows(j, t)
        for h in range(C_HEADS):
            _accumulate_probs(j, h * 2 * t, 2 * t, _with_ones(v_ref[0, rows, h * C_VDIM:(h + 1) * C_VDIM]),
                              s_ref, max_ref, acc_ref)

    _for_each_tile(it + 1, probs, group=4)

    lp = lam_ref[...]
    lam = (jnp.exp(jnp.sum(lp[0:1] * lp[1:2], axis=-1, keepdims=True))
           - jnp.exp(jnp.sum(lp[2:3] * lp[3:4], axis=-1, keepdims=True)) + lam_init)
    for h in range(C_HEADS):
        o = _normalised(h * 2 * t, t, acc_ref) - lam * _normalised(h * 2 * t + t, t, acc_ref)
        o = o * lax.rsqrt(jnp.mean(o * o, axis=-1, keepdims=True) + EPS) * g_ref[...] * (1.0 - lam_init)
        o_ref[0, :, h * C_VDIM:(h + 1) * C_VDIM] = o.astype(o_ref.dtype)


def _diff_attention(qc, kc, vc, lam_p, subln_g, lam_init):
    b, s, w = qc.shape
    t = ATT_TILE
    qspec = pl.BlockSpec((1, t, w), lambda bi, i: (bi, i, 0))
    kvspec = pl.BlockSpec((1, s, w), lambda bi, i: (bi, 0, 0))
    return pl.pallas_call(
        functools.partial(_diff_kernel, lam_init=lam_init, t=t),
        grid=(b, s // t),
        in_specs=[pl.BlockSpec(lam_p.shape, lambda bi, i: (0, 0)),
                  pl.BlockSpec((1, C_VDIM), lambda bi, i: (0, 0)), qspec, kvspec, kvspec],
        out_specs=qspec,
        out_shape=jax.ShapeDtypeStruct((b, s, w), BF16),
        scratch_shapes=_ATT_SCRATCH(s // t, 2 * C_HEADS * t, t, 2 * LANES),
        compiler_params=_params("parallel", "arbitrary"),
        name="diff_attention",
    )(lam_p, subln_g, qc, kc, vc)


def _moba_kernel(q_ref, k_ref, v_ref, o_ref, kmean_hi, kmean_lo, bias_ref, s_ref, max_ref, acc_ref, *, t):
    it = pl.program_id(1)
    n_blk = k_ref.shape[1] // t
    width = q_ref.shape[2]

    @pl.when(it == 0)
    def _():
        kmean_hi[...] = jnp.zeros_like(kmean_hi)
        kmean_lo[...] = jnp.zeros_like(kmean_lo)
        for n in range(n_blk):
            km = jnp.mean(k_ref[0, n * t:(n + 1) * t, :].astype(F32), axis=0, keepdims=True)
            hi = km.astype(BF16)
            kmean_hi[n:n + 1, :] = hi
            kmean_lo[n:n + 1, :] = (km - hi.astype(F32)).astype(BF16)

    heads = _head_lane_masks(t, width)
    q_stack = _stack_masked(q_ref[0], heads)

    no_bias = jnp.zeros(bias_ref.shape[1:], F32)
    bias_ref[it] = no_bias

    @pl.when(it <= MOBA_TOPK)
    def _():
        for n in range(min(MOBA_TOPK, n_blk)):
            @pl.when(n < it)
            def _():
                bias_ref[n] = no_bias

    @pl.when(it > MOBA_TOPK)
    def _():
        blk = lax.broadcasted_iota(jnp.int32, (SUBLANES, t), 0)
        past = blk < it
        pad = jnp.full((LANES - SUBLANES, t), MASKED, F32)
        bias_q = []
        for h in range(B_HEADS):
            qh = q_stack[h * t:(h + 1) * t]
            gate = (_nt_dot(kmean_hi[...], qh) + _nt_dot(kmean_lo[...], qh))[:SUBLANES]
            gate = jnp.where(past, gate, -jnp.inf)
            rank = jnp.zeros((SUBLANES, t), jnp.int32)
            for n in range(n_blk):
                gn = gate[n:n + 1, :]
                rank = rank + jnp.where((gn > gate) | ((gn == gate) & (blk > n)), 1, 0)
            bias = jnp.where((rank < MOBA_TOPK) & past, 0.0, MASKED)
            bias_q.append(jnp.concatenate([bias, pad], axis=0).T.astype(BF16))
        bias_q = jnp.concatenate(bias_q, axis=0)
        lane_of = lax.broadcasted_iota(jnp.int32, (LANES, LANES), 0)
        for n in range(n_blk - 1):
            @pl.when(n < it)
            def _():
                pick = (lane_of == n).astype(BF16)
                bias_ref[n] = _dot(bias_q, pick)

    _reset_stats(max_ref, acc_ref)
    causal = jnp.where(_local_causal(t), 0.0, MASKED)
    causal4 = jnp.concatenate([causal] * B_HEADS, axis=0)

    def scores(j):
        s = _nt_dot(q_stack, k_ref[0, _tile_rows(j, t), :]) + _lane_tile(bias_ref[j], t)
        _record_scores(s + jnp.where(j == it, causal4, 0.0), j, 0, s_ref, max_ref)

    _for_each_tile(it + 1, scores, group=4)
    _finish_max(max_ref)

    def probs(j):
        rows = _tile_rows(j, t)
        for pair in range(B_HEADS // 2):
            _accumulate_probs(j, pair * 2 * t, 2 * t, _with_ones(v_ref[0, rows, pair * LANES:(pair + 1) * LANES]),
                              s_ref, max_ref, acc_ref)

    _for_each_tile(it + 1, probs, group=4)

    o = [_normalised(h * t, t, acc_ref) for h in range(B_HEADS)]
    o_ref[0] = _interleave_pairs(o, t).astype(o_ref.dtype)


def _moba_attention(qb, kb, vb):
    b, s, w = qb.shape
    t = MOBA_BLOCK
    assert s // t <= SUBLANES
    qspec = pl.BlockSpec((1, t, w), lambda bi, i: (bi, i, 0))
    kvspec = pl.BlockSpec((1, s, w), lambda bi, i: (bi, 0, 0))
    return pl.pallas_call(
        functools.partial(_moba_kernel, t=t),
        grid=(b, s // t),
        in_specs=[qspec, kvspec, kvspec],
        out_specs=qspec,
        out_shape=jax.ShapeDtypeStruct((b, s, w), BF16),
        scratch_shapes=[pltpu.VMEM((LANES, w), BF16), pltpu.VMEM((LANES, w), BF16),
                        pltpu.VMEM((s // t, B_HEADS * t, LANES), F32)]
                       + _ATT_SCRATCH(s // t, B_HEADS * t, t, 2 * LANES),
        compiler_params=_params("parallel", "arbitrary"),
        name="moba_attention",
    )(qb, kb, vb)


def _sublane_fold(x, rows):
    parts = [x[r:r + rows] for r in range(0, x.shape[0], rows)]
    while len(parts) > 1:
        parts = [parts[i] + parts[i + 1] for i in range(0, len(parts), 2)]
    return parts[0]


I16 = jnp.int16
I16_ROWS = 2 * SUBLANES
I16_MIN = -(2 ** 15)


def _dsa_kernel(q_ref, k_ref, v_ref, qi_ref, ki_ref, wi_ref, o_ref,
                keys_ref, hi_ref, lo_ref, ties_ref, s_ref, max_ref, acc_ref, *, t, topk):
    it = pl.program_id(1)
    width = q_ref.shape[2]
    heads = _head_lane_masks(t, width)
    causal = _local_causal(t)

    qidx = qi_ref[0]
    qi_stack = jnp.concatenate(
        [_stack_masked(qidx[:, g * width:(g + 1) * width], heads) for g in range(IDX_HEADS // 4)], axis=0)
    wi = wi_ref[0]
    w_rep = [jnp.broadcast_to(wi[:, HEAD_DIM + h:HEAD_DIM + h + 1], (t, LANES)) for h in range(IDX_HEADS)]

    def score_tile(j):
        logit = _nt_dot(qi_stack, ki_ref[0, _tile_rows(j, t), :])
        cols = []
        for c in range(0, t, LANES):
            acc = jnp.zeros((t, LANES), F32)
            for h in range(IDX_HEADS):
                acc = acc + jnp.maximum(logit[h * t:(h + 1) * t, c:c + LANES], 0.0) * w_rep[h]
            cols.append(acc)
        isc = jnp.concatenate(cols, axis=1) + 0.0
        isc = jnp.where(j == it, jnp.where(causal, isc, -jnp.inf), isc)
        bits = lax.bitcast_convert_type(isc, jnp.int32)
        key = bits ^ ((bits >> 31) & 0x7FFFFFFF)
        keys_ref[j] = key
        key_t = key.T
        hi_ref[j] = (key_t >> 16).astype(I16)
        lo_ref[j] = ((key_t & 0xFFFF) + I16_MIN).astype(I16)

    _for_each_tile(it + 1, score_tile, group=4)
    n_pairs = it // 2 + 1

    @pl.when(it % 2 == 0)
    def _():
        hi_ref[it + 1] = jnp.full((t, t), I16_MIN, I16)
        lo_ref[it + 1] = jnp.full((t, t), I16_MIN, I16)

    def count(ref, pred):
        def one(j, cnt):
            return cnt + _sublane_fold(jnp.where(pred(ref[j]), jnp.int16(1), jnp.int16(0)), I16_ROWS)
        cnt = lax.fori_loop(0, n_pairs, lambda i, cnt: one(2 * i + 1, one(2 * i, cnt)), jnp.zeros((I16_ROWS, t), I16))
        return jnp.sum(cnt.astype(jnp.int32), axis=0, keepdims=True)

    def kth_largest(ref, k):
        thr = jnp.where(count(ref, lambda x: x >= jnp.int16(0)) >= k, 0, jnp.full((1, t), I16_MIN, jnp.int32))

        def bit_step(i, thr):
            cand = thr | (1 << (14 - i))
            return jnp.where(count(ref, lambda x: x >= cand.astype(I16)) >= k, cand, thr)

        return lax.fori_loop(0, 15, bit_step, thr)

    thr_hi = kth_largest(hi_ref, topk)
    thr_hi16 = thr_hi.astype(I16)
    k_lo = topk - count(hi_ref, lambda x: x > thr_hi16)

    def keep_matching(j):
        lo_ref[j] = jnp.where(hi_ref[j] == thr_hi16, lo_ref[j], jnp.int16(I16_MIN))

    @pl.loop(0, n_pairs)
    def _(i):
        keep_matching(2 * i)
        keep_matching(2 * i + 1)

    thr_lo = kth_largest(lo_ref, k_lo)
    thr_lo16 = thr_lo.astype(I16)
    wanted = (k_lo - count(lo_ref, lambda x: x > thr_lo16)).astype(F32)
    thr = (thr_hi << 16) | (thr_lo - I16_MIN)
    thr_q = _lane_tile(jnp.broadcast_to(thr, (LANES, t)).T, t)
    wanted_q = _lane_tile(jnp.broadcast_to(wanted, (LANES, t)).T, t)

    q_stack = _stack_masked(q_ref[0], heads)
    r = lax.broadcasted_iota(jnp.int32, (t, t), 0)
    c = lax.broadcasted_iota(jnp.int32, (t, t), 1)
    before = (r < c).astype(BF16)
    _reset_stats(max_ref, acc_ref)

    ties_ref[...] = jnp.zeros(ties_ref.shape, F32)

    not_causal = jnp.where(causal, 0.0, MASKED)

    def select_tile(j):
        key = keys_ref[j]
        tie = key == thr_q
        tie_b = jnp.where(tie, 1.0, 0.0).astype(BF16)
        ties_before = _dot(tie_b, before)
        seen = ties_ref[...]
        tie_rank = _lane_tile(seen, t) + ties_before
        bias = jnp.where(key > thr_q, 0.0, jnp.where(tie, jnp.where(tie_rank < wanted_q, 0.0, MASKED), MASKED))
        bias = bias + jnp.where(j == it, not_causal, 0.0)
        s = _nt_dot(q_stack, k_ref[0, _tile_rows(j, t), :]) + jnp.concatenate([bias] * A_HEADS, axis=0)
        _record_scores(s, j, 0, s_ref, max_ref)
        in_tile = ties_before[:, t - 1:] + jnp.where(tie[:, t - 1:], 1.0, 0.0)
        ties_ref[...] = seen + jnp.broadcast_to(in_tile, seen.shape)

    _for_each_tile(it + 1, select_tile, group=4)
    _finish_max(max_ref)

    _for_each_tile(it + 1, lambda j: _accumulate_probs(
        j, 0, A_HEADS * t, v_ref[0, _tile_rows(j, t), :], s_ref, max_ref, acc_ref), group=4)

    sums_only = lax.broadcasted_iota(jnp.int32, (t, LANES), 1) >= HEAD_DIM
    o = []
    for h in range(A_HEADS):
        a = acc_ref[h * t:(h + 1) * t, :]
        inv = 1.0 / jnp.where(sums_only, a, 1.0)
        o.append(pltpu.roll(a, HEAD_DIM, 1) * inv if h % 2 else a * pltpu.roll(inv, HEAD_DIM, 1))
    o_ref[0] = _interleave_pairs(o, t).astype(o_ref.dtype)


def _dsa_attention(qa, ka, va, qi, ki, wi):
    b, s, w = qa.shape
    t = ATT_TILE
    topk = min(INDEX_TOPK, s // 4)
    assert t >= topk
    row = lambda width: pl.BlockSpec((1, t, width), lambda bi, i: (bi, i, 0))
    full = lambda width: pl.BlockSpec((1, s, width), lambda bi, i: (bi, 0, 0))
    return pl.pallas_call(
        functools.partial(_dsa_kernel, t=t, topk=topk),
        grid=(b, s // t),
        in_specs=[row(w), full(w), full(va.shape[2]), row(qi.shape[2]), full(w), row(LANES)],
        out_specs=row(w),
        out_shape=jax.ShapeDtypeStruct((b, s, w), BF16),
        scratch_shapes=[pltpu.VMEM((s // t, t, t), jnp.int32),
                        pltpu.VMEM((s // t + s // t % 2, t, t), I16), pltpu.VMEM((s // t + s // t % 2, t, t), I16),
                        pltpu.VMEM((t, LANES), F32)]
                       + _ATT_SCRATCH(s // t, A_HEADS * t, t, LANES),
        compiler_params=_params("parallel", "arbitrary"),
        name="dsa_attention",
    )(qa, ka, va, qi, ki, wi)


def _merge_kernel(x_ref, g_ref, oa_ref, ob_ref, oc_ref, wg_ref, wb_ref, wo_ref, o_ref, *, chunk):
    x = x_ref[...]
    d = x.shape[1]
    h = _rmsnorm_rows(x, g_ref[...]).astype(BF16)
    branches, row0 = [], 0
    for o_ref_i in (oa_ref, ob_ref, oc_ref):
        branches.append((o_ref_i[...], row0))
        row0 += o_ref_i.shape[1]
    merged = []
    for c in range(0, d, chunk):
        m = jnp.zeros((x.shape[0], chunk), F32)
        for i, (o_i, r0) in enumerate(branches):
            gate = jax.nn.sigmoid(_dot(h, wg_ref[:, i * d + c:i * d + c + chunk]))
            m = m + gate * _dot(o_i, wb_ref[r0:r0 + o_i.shape[1], c:c + chunk])
        merged.append(m.astype(BF16))
    o_ref[...] = x + _dot(jnp.concatenate(merged, axis=1), wo_ref[...])


def _merge(x, g, oa, ob, oc, wg, wb, wo, layer, tm):
    n, d = x.shape
    row = lambda w: pl.BlockSpec((tm, w), lambda i: (i, 0))
    return pl.pallas_call(
        functools.partial(_merge_kernel, chunk=MXU_DIM),
        grid=(n // tm,),
        in_specs=[row(d), _resident((1, d)), row(oa.shape[1]), row(ob.shape[1]), row(oc.shape[1]),
                  _resident(wg.shape, (layer,)), _resident(wb.shape, (layer,)), _resident(wo.shape, (layer,))],
        out_specs=row(d),
        out_shape=jax.ShapeDtypeStruct((n, d), F32),
        compiler_params=_params("parallel"),
        name="merge_out",
    )(x, g, oa, ob, oc, wg, wb, wo)


def kernel(x, positions, norm_g, w_in, qk_norm_g, lambda_params, diff_subln_g, w_branch, w_out,
           ffn_w_gate, ffn_w_up, ffn_w_down):
    b, s, d = x.shape
    n = b * s
    depth = norm_g.shape[0]
    tm = min(1024, n)
    assert n % tm == 0 and s % ATT_TILE == 0 and ATT_TILE == MOBA_BLOCK

    cos, sin = _rope_tables(positions, min(2 * tm, n))
    w_proj, w_gates = _proj_weights(w_in)
    ffn_w = tuple(w.astype(BF16) for w in (ffn_w_gate, ffn_w_up, ffn_w_down))
    w_mix, w_o = w_branch.astype(BF16), w_out.astype(BF16)
    xf = x.reshape(n, d)
    for layer in range(depth):
        lam_init = 0.8 - 0.6 * math.exp(-0.3 * layer)
        xf = _ffn(xf, norm_g[layer, 0][None, :], *ffn_w, (layer, 0), tm)

        qkg = jnp.tile(qk_norm_g[layer], (1, LANES // HEAD_DIM))
        outs = _proj(xf, norm_g[layer, 1][None, :], qkg, cos, sin, w_proj, layer, tm)
        p = {name: o.reshape(b, s, o.shape[1]) for (name, *_), o in zip(_PROJ_OUTPUTS, outs)}
        o_a = _dsa_attention(p["qa"], p["ka"], p["va"], p["qi"], p["ki"], p["wi"])
        o_b = _moba_attention(p["qb"], p["kb"], p["vb"])
        o_c = _diff_attention(p["qc"], p["kc"], p["vc"], lambda_params[layer], diff_subln_g[layer][None, :], lam_init)
        xf = _merge(xf, norm_g[layer, 1][None, :], o_a.reshape(n, -1), o_b.reshape(n, -1), o_c.reshape(n, -1),
                    w_gates, w_mix, w_o, layer, tm)

        xf = _ffn(xf, norm_g[layer, 2][None, :], *ffn_w, (layer, 1), tm)
    return xf.reshape(b, s, d)
```

```python
import functools
import math

import jax
import jax.numpy as jnp
from jax import lax
from jax.experimental import pallas as pl
from jax.experimental.pallas import tpu as pltpu

F32 = jnp.float32
BF16 = jnp.bfloat16

HEAD_DIM = 64
ROT_DIM = HEAD_DIM // 4
ROT_HALF = ROT_DIM // 2
ROPE_THETA = 500000.0
EPS = 1e-6
A_HEADS = 4
IDX_HEADS = 8
IDX_DIM = 64
INDEX_TOPK = 256
B_HEADS = 4
MOBA_BLOCK = 256
MOBA_TOPK = 3
C_HEADS = 4
C_VDIM = 2 * HEAD_DIM

LANES = 128
SUBLANES = 8
MXU_DIM = 256
ATT_TILE = 256
MASKED = -1e30
VMEM_LIMIT = 56 * 1024 * 1024
LOG2E = 1.4426950408889634


def _nt_dot(a, b):
    return lax.dot_general(a, b, (((1,), (1,)), ((), ())), preferred_element_type=F32)


def _dot(a, b):
    return jnp.dot(a, b, preferred_element_type=F32)


def _resident(shape, index=()):
    block = (None,) * len(index) + tuple(shape[len(index):])
    at = tuple(index) + (0,) * (len(shape) - len(index))
    return pl.BlockSpec(block, lambda *_: at, pipeline_mode=pl.Buffered(1))


def _params(*sem):
    return pltpu.CompilerParams(dimension_semantics=sem, vmem_limit_bytes=VMEM_LIMIT)


def _rope_table_kernel(pos_ref, freq_ref, cos_ref, sin_ref):
    ang = pos_ref[...].astype(F32) * freq_ref[...]
    lane = lax.broadcasted_iota(jnp.int32, ang.shape, 1) % HEAD_DIM
    c = jnp.cos(ang)
    s = jnp.sin(ang)
    cos_ref[...] = jnp.where(lane < ROT_DIM, c, 1.0)
    sin_ref[...] = jnp.where(lane < ROT_HALF, -s, jnp.where(lane < ROT_DIM, s, 0.0))


def _rope_tables(positions, tm):
    n = positions.size
    pos = positions.reshape(n, 1)
    inv_freq = jnp.power(ROPE_THETA, -jnp.arange(0, ROT_DIM, 2, dtype=F32) / ROT_DIM)
    freq = jnp.tile(jnp.concatenate([inv_freq, inv_freq, jnp.zeros((HEAD_DIM - ROT_DIM,), F32)]), 2)[None, :]
    return pl.pallas_call(
        _rope_table_kernel,
        grid=(n // tm,),
        in_specs=[pl.BlockSpec((tm, 1), lambda i: (i, 0)), pl.BlockSpec((1, LANES), lambda i: (0, 0))],
        out_specs=[pl.BlockSpec((tm, LANES), lambda i: (i, 0))] * 2,
        out_shape=[jax.ShapeDtypeStruct((n, LANES), F32)] * 2,
        compiler_params=_params("parallel"),
        name="rope_tables",
    )(pos, freq)


def _rmsnorm_rows(x, g):
    return x * lax.rsqrt(jnp.mean(x * x, axis=-1, keepdims=True) + EPS) * g


def _ffn_rows(x, g, wg_ref, wu_ref, wd_ref, chunk):
    h = _rmsnorm_rows(x, g).astype(BF16)
    acc = jnp.zeros(x.shape, F32)
    for c in range(0, wg_ref.shape[1], chunk):
        a = _dot(h, wg_ref[:, c:c + chunk])
        b = _dot(h, wu_ref[:, c:c + chunk])
        t = (a * jax.nn.sigmoid(a) * b).astype(BF16)
        acc = acc + _dot(t, wd_ref[c:c + chunk, :])
    return x + 0.5 * acc


def _ffn_kernel(x_ref, g_ref, wg_ref, wu_ref, wd_ref, o_ref, *, chunk):
    o_ref[...] = _ffn_rows(x_ref[...], g_ref[...], wg_ref, wu_ref, wd_ref, chunk)


def _ffn(x, g, wg, wu, wd, which, tm):
    n, d = x.shape
    return pl.pallas_call(
        functools.partial(_ffn_kernel, chunk=MXU_DIM),
        grid=(n // tm,),
        in_specs=[pl.BlockSpec((tm, d), lambda i: (i, 0)), _resident((1, d)),
                  _resident(wg.shape, which), _resident(wu.shape, which), _resident(wd.shape, which)],
        out_specs=pl.BlockSpec((tm, d), lambda i: (i, 0)),
        out_shape=jax.ShapeDtypeStruct((n, d), F32),
        compiler_params=_params("parallel"),
        name="ffn",
    )(x, g, wg, wu, wd)


_W_IN_SPLITS = (("qa", 256), ("ka", 64), ("va", 64), ("qi", 512), ("ki", 64), ("wi", 8), ("qb", 256), ("kb", 256),
                ("vb", 256), ("qc", 512), ("kc", 512), ("vc", 512), ("ga", 1024), ("gb", 1024), ("gc", 1024))
_PROJ_COLUMNS = (
    ("qa", 256, "norm_rope_scale", 0),
    ("key_value", 128, "key_value", 1),
    ("qi", 512, "rope", None),
    ("index_key_weight", 128, "index_key_weight", None),
    ("qb", 256, "norm_rope_scale", 2),
    ("kb", 256, "norm_rope", 3),
    ("vb", 256, "plain", None),
    ("qc", 512, "norm_rope_scale", 4),
    ("kc", 512, "norm_rope", 5),
    ("vc", 512, "plain", None),
)
_GATE_COLUMNS = 3 * 1024
_PROJ_OUTPUTS = (
    ("qa", 256, BF16), ("ka", 256, BF16), ("va", 128, BF16), ("qi", 512, BF16), ("ki", 256, BF16), ("wi", 128, F32),
    ("qb", 256, BF16), ("kb", 256, BF16), ("vb", 256, BF16), ("qc", 512, BF16), ("kc", 512, BF16), ("vc", 512, BF16),
)
_Q_SCALE = HEAD_DIM ** -0.5 * LOG2E


def _proj_kernel(x_ref, g_ref, qkg_ref, cos_ref, sin_ref, w_ref, *o_refs):
    out = {name: ref for (name, _, _), ref in zip(_PROJ_OUTPUTS, o_refs)}
    tm = x_ref.shape[0]
    h = _rmsnorm_rows(x_ref[...], g_ref[...]).astype(BF16)
    cos = cos_ref[...]
    sin = sin_ref[...]
    lane = lax.broadcasted_iota(jnp.int32, (tm, LANES), 1)
    take_upper = (lane % HEAD_DIM) < ROT_HALF
    low_half = lane < HEAD_DIM
    r = lax.broadcasted_iota(jnp.int32, (LANES, LANES), 0) // HEAD_DIM
    c = lax.broadcasted_iota(jnp.int32, (LANES, LANES), 1) // HEAD_DIM
    same_head = (r == c).astype(BF16)

    def epilogue(y, kind, g_idx):
        if kind in ("norm_rope_scale", "norm_rope"):
            ssq = _dot((y * y).astype(BF16), same_head)
            y = y * lax.rsqrt(ssq * (1.0 / HEAD_DIM) + EPS) * qkg_ref[g_idx:g_idx + 1, :]
        if kind in ("norm_rope_scale", "norm_rope", "rope"):
            partner = jnp.where(take_upper, pltpu.roll(y, LANES - ROT_HALF, 1), pltpu.roll(y, ROT_HALF, 1))
            y = y * cos + partner * sin
        if kind == "norm_rope_scale":
            y = y * _Q_SCALE
        return y

    def on_both_halves(y):
        return jnp.where(low_half, y, pltpu.roll(y, HEAD_DIM, 1))

    col = 0
    for name, width, kind, g_idx in _PROJ_COLUMNS:
        step = min(width, MXU_DIM)
        for c0 in range(0, width, step):
            y = _dot(h, w_ref[:, col + c0:col + c0 + step])
            if kind == "key_value":
                ka = on_both_halves(epilogue(y, "norm_rope", g_idx)).astype(BF16)
                for s0 in range(0, out["ka"].shape[1], LANES):
                    out["ka"][:, s0:s0 + LANES] = ka
                out["va"][...] = jnp.where(low_half, pltpu.roll(y, HEAD_DIM, 1), 1.0).astype(BF16)
                continue
            if kind == "index_key_weight":
                ki = on_both_halves(epilogue(y, "rope", None)).astype(BF16)
                for s0 in range(0, out["ki"].shape[1], LANES):
                    out["ki"][:, s0:s0 + LANES] = ki
                out["wi"][...] = y * ((IDX_HEADS * IDX_DIM) ** -0.5)
                continue
            o_ref = out[name]
            for s0 in range(0, step, LANES):
                o_ref[:, c0 + s0:c0 + s0 + LANES] = epilogue(y[:, s0:s0 + LANES], kind, g_idx).astype(o_ref.dtype)
        col += width


def _proj_weights(w_in):
    w = w_in.astype(BF16)
    names = [name for name, _ in _W_IN_SPLITS]
    end_of_wi = sum(width for _, width in _W_IN_SPLITS[:names.index("wi") + 1])
    gates = w.shape[-1] - _GATE_COLUMNS
    pad = jnp.zeros(w.shape[:-1] + (-end_of_wi % LANES,), w.dtype)
    padded = jnp.concatenate([w[..., :end_of_wi], pad, w[..., end_of_wi:gates]], axis=-1)
    assert padded.shape[-1] == sum(width for _, width, _, _ in _PROJ_COLUMNS)
    return padded, w[..., gates:]


def _proj(x, g, qkg, cos, sin, w, layer, tm):
    n, d = x.shape
    row = lambda width: pl.BlockSpec((tm, width), lambda i: (i, 0))
    return pl.pallas_call(
        _proj_kernel,
        grid=(n // tm,),
        in_specs=[row(d), _resident((1, d)), _resident(qkg.shape), row(LANES), row(LANES),
                  _resident(w.shape, (layer,))],
        out_specs=[row(width) for _, width, _ in _PROJ_OUTPUTS],
        out_shape=[jax.ShapeDtypeStruct((n, width), dtype) for _, width, dtype in _PROJ_OUTPUTS],
        compiler_params=_params("parallel"),
        name="in_proj",
    )(x, g, qkg, cos, sin, w)


def _lane_max(x):
    out = x[:, :LANES]
    for c in range(LANES, x.shape[1], LANES):
        out = jnp.maximum(out, x[:, c:c + LANES])
    return out


def _lane_tile(x, width):
    return jnp.concatenate([x] * (width // LANES), axis=1)


def _reset_stats(max_ref, acc_ref):
    max_ref[...] = jnp.full(max_ref.shape, MASKED, F32)
    acc_ref[...] = jnp.zeros(acc_ref.shape, F32)


def _record_scores(s, j, r0, s_ref, max_ref):
    n = s.shape[0]
    s_ref[j, r0:r0 + n, :] = s
    max_ref[r0:r0 + n, :] = jnp.maximum(max_ref[r0:r0 + n, :], _lane_max(s))


def _finish_max(max_ref):
    m = max_ref[...]
    max_ref[...] = jnp.broadcast_to(jnp.max(m, axis=-1, keepdims=True), m.shape)


def _accumulate_probs(j, r0, n, v, s_ref, max_ref, acc_ref):
    s = s_ref[j, r0:r0 + n, :]
    p = jnp.exp2(s - _lane_tile(max_ref[r0:r0 + n, :], s.shape[1]))
    acc_ref[r0:r0 + n, :] += _dot(p.astype(BF16), v)


def _with_ones(v):
    return jnp.concatenate([v, jnp.ones(v.shape, v.dtype)], axis=1)


def _normalised(r0, n, acc_ref):
    return acc_ref[r0:r0 + n, :LANES] / acc_ref[r0:r0 + n, LANES:]


def _head_lane_masks(rows, width):
    lane = lax.broadcasted_iota(jnp.int32, (rows, width), 1)
    return [(lane >= h * HEAD_DIM) & (lane < (h + 1) * HEAD_DIM) for h in range(width // HEAD_DIM)]


def _stack_masked(q, masks):
    zero = jnp.zeros_like(q)
    return jnp.concatenate([jnp.where(m, q, zero) for m in masks], axis=0)


def _local_causal(t):
    row = lax.broadcasted_iota(jnp.int32, (t, t), 0)
    col = lax.broadcasted_iota(jnp.int32, (t, t), 1)
    return col <= row


def _tile_rows(j, t):
    return pl.ds(pl.multiple_of(j * t, t), t)


def _for_each_tile(n, body, group=2):
    @pl.loop(0, n // group)
    def _(i):
        for k in range(group):
            body(group * i + k)

    part = group // 2
    while part >= 1:
        @pl.when(n % (2 * part) >= part)
        def _():
            for k in range(part):
                body(n - n % (2 * part) + k)
        part //= 2


def _interleave_pairs(o, t):
    lane = lax.broadcasted_iota(jnp.int32, (t, LANES), 1)
    low = lane < HEAD_DIM
    return jnp.concatenate([jnp.where(low, o[0], o[1]), jnp.where(low, o[2], o[3])], axis=1)


_ATT_SCRATCH = lambda n_tiles, rows, t, acc_width: [
    pltpu.VMEM((n_tiles, rows, t), F32),
    pltpu.VMEM((rows, LANES), F32),
    pltpu.VMEM((rows, acc_width), F32),
]


def _diff_kernel(lam_ref, g_ref, q_ref, k_ref, v_ref, o_ref, s_ref, max_ref, acc_ref, *, lam_init, t):
    it = pl.program_id(1)
    lane = lax.broadcasted_iota(jnp.int32, (t, C_VDIM), 1)
    maps = [lane < HEAD_DIM, lane >= HEAD_DIM]
    q_all = q_ref[0]
    qs = [_stack_masked(q_all[:, h * C_VDIM:(h + 1) * C_VDIM], maps) for h in range(C_HEADS)]
    _reset_stats(max_ref, acc_ref)
    causal = jnp.where(_local_causal(t), 0.0, MASKED)
    causal2 = jnp.concatenate([causal, causal], axis=0)

    def scores(j):
        rows = _tile_rows(j, t)
        bias = jnp.where(j == it, causal2, 0.0)
        for h in range(C_HEADS):
            s = _nt_dot(qs[h], k_ref[0, rows, h * C_VDIM:(h + 1) * C_VDIM]) + bias
            _record_scores(s, j, h * 2 * t, s_ref, max_ref)

    _for_each_tile(it + 1, scores, group=4)
    _finish_max(max_ref)

    def probs(j):
        rows = _tile_rows(j, t)
        for h in range(C_HEADS):
            _accumulate_probs(j, h * 2 * t, 2 * t, _with_ones(v_ref[0, rows, h * C_VDIM:(h + 1) * C_VDIM]),
                              s_ref, max_ref, acc_ref)

    _for_each_tile(it + 1, probs, group=4)

    lp = lam_ref[...]
    lam = (jnp.exp(jnp.sum(lp[0:1] * lp[1:2], axis=-1, keepdims=True))
           - jnp.exp(jnp.sum(lp[2:3] * lp[3:4], axis=-1, keepdims=True)) + lam_init)
    for h in range(C_HEADS):
        o = _normalised(h * 2 * t, t, acc_ref) - lam * _normalised(h * 2 * t + t, t, acc_ref)
        o = o * lax.rsqrt(jnp.mean(o * o, axis=-1, keepdims=True) + EPS) * g_ref[...] * (1.0 - lam_init)
        o_ref[0, :, h * C_VDIM:(h + 1) * C_VDIM] = o.astype(o_ref.dtype)


def _diff_attention(qc, kc, vc, lam_p, subln_g, lam_init):
    b, s, w = qc.shape
    t = ATT_TILE
    qspec = pl.BlockSpec((1, t, w), lambda bi, i: (bi, i, 0))
    kvspec = pl.BlockSpec((1, s, w), lambda bi, i: (bi, 0, 0))
    return pl.pallas_call(
        functools.partial(_diff_kernel, lam_init=lam_init, t=t),
        grid=(b, s // t),
        in_specs=[pl.BlockSpec(lam_p.shape, lambda bi, i: (0, 0)),
                  pl.BlockSpec((1, C_VDIM), lambda bi, i: (0, 0)), qspec, kvspec, kvspec],
        out_specs=qspec,
        out_shape=jax.ShapeDtypeStruct((b, s, w), BF16),
        scratch_shapes=_ATT_SCRATCH(s // t, 2 * C_HEADS * t, t, 2 * LANES),
        compiler_params=_params("parallel", "arbitrary"),
        name="diff_attention",
    )(lam_p, subln_g, qc, kc, vc)


def _moba_kernel(q_ref, k_ref, v_ref, o_ref, kmean_hi, kmean_lo, bias_ref, s_ref, max_ref, acc_ref, *, t):
    it = pl.program_id(1)
    n_blk = k_ref.shape[1] // t
    width = q_ref.shape[2]

    @pl.when(it == 0)
    def _():
        kmean_hi[...] = jnp.zeros_like(kmean_hi)
        kmean_lo[...] = jnp.zeros_like(kmean_lo)
        for n in range(n_blk):
            km = jnp.mean(k_ref[0, n * t:(n + 1) * t, :].astype(F32), axis=0, keepdims=True)
            hi = km.astype(BF16)
            kmean_hi[n:n + 1, :] = hi
            kmean_lo[n:n + 1, :] = (km - hi.astype(F32)).astype(BF16)

    heads = _head_lane_masks(t, width)
    q_stack = _stack_masked(q_ref[0], heads)

    no_bias = jnp.zeros(bias_ref.shape[1:], F32)
    bias_ref[it] = no_bias

    @pl.when(it <= MOBA_TOPK)
    def _():
        for n in range(min(MOBA_TOPK, n_blk)):
            @pl.when(n < it)
            def _():
                bias_ref[n] = no_bias

    @pl.when(it > MOBA_TOPK)
    def _():
        blk = lax.broadcasted_iota(jnp.int32, (SUBLANES, t), 0)
        past = blk < it
        pad = jnp.full((LANES - SUBLANES, t), MASKED, F32)
        bias_q = []
        for h in range(B_HEADS):
            qh = q_stack[h * t:(h + 1) * t]
            gate = (_nt_dot(kmean_hi[...], qh) + _nt_dot(kmean_lo[...], qh))[:SUBLANES]
            gate = jnp.where(past, gate, -jnp.inf)
            rank = jnp.zeros((SUBLANES, t), jnp.int32)
            for n in range(n_blk):
                gn = gate[n:n + 1, :]
                rank = rank + jnp.where((gn > gate) | ((gn == gate) & (blk > n)), 1, 0)
            bias = jnp.where((rank < MOBA_TOPK) & past, 0.0, MASKED)
            bias_q.append(jnp.concatenate([bias, pad], axis=0).T.astype(BF16))
        bias_q = jnp.concatenate(bias_q, axis=0)
        lane_of = lax.broadcasted_iota(jnp.int32, (LANES, LANES), 0)
        for n in range(n_blk - 1):
            @pl.when(n < it)
            def _():
                pick = (lane_of == n).astype(BF16)
                bias_ref[n] = _dot(bias_q, pick)

    _reset_stats(max_ref, acc_ref)
    causal = jnp.where(_local_causal(t), 0.0, MASKED)
    causal4 = jnp.concatenate([causal] * B_HEADS, axis=0)

    def scores(j):
        s = _nt_dot(q_stack, k_ref[0, _tile_rows(j, t), :]) + _lane_tile(bias_ref[j], t)
        _record_scores(s + jnp.where(j == it, causal4, 0.0), j, 0, s_ref, max_ref)

    _for_each_tile(it + 1, scores, group=4)
    _finish_max(max_ref)

    def probs(j):
        rows = _tile_rows(j, t)
        for pair in range(B_HEADS // 2):
            _accumulate_probs(j, pair * 2 * t, 2 * t, _with_ones(v_ref[0, rows, pair * LANES:(pair + 1) * LANES]),
                              s_ref, max_ref, acc_ref)

    _for_each_tile(it + 1, probs, group=4)

    o = [_normalised(h * t, t, acc_ref) for h in range(B_HEADS)]
    o_ref[0] = _interleave_pairs(o, t).astype(o_ref.dtype)


def _moba_attention(qb, kb, vb):
    b, s, w = qb.shape
    t = MOBA_BLOCK
    assert s // t <= SUBLANES
    qspec = pl.BlockSpec((1, t, w), lambda bi, i: (bi, i, 0))
    kvspec = pl.BlockSpec((1, s, w), lambda bi, i: (bi, 0, 0))
    return pl.pallas_call(
        functools.partial(_moba_kernel, t=t),
        grid=(b, s // t),
        in_specs=[qspec, kvspec, kvspec],
        out_specs=qspec,
        out_shape=jax.ShapeDtypeStruct((b, s, w), BF16),
        scratch_shapes=[pltpu.VMEM((LANES, w), BF16), pltpu.VMEM((LANES, w), BF16),
                        pltpu.VMEM((s // t, B_HEADS * t, LANES), F32)]
                       + _ATT_SCRATCH(s // t, B_HEADS * t, t, 2 * LANES),
        compiler_params=_params("parallel", "arbitrary"),
        name="moba_attention",
    )(qb, kb, vb)


def _sublane_fold(x, rows):
    parts = [x[r:r + rows] for r in range(0, x.shape[0], rows)]
    while len(parts) > 1:
        parts = [parts[i] + parts[i + 1] for i in range(0, len(parts), 2)]
    return parts[0]


I16 = jnp.int16
I16_ROWS = 2 * SUBLANES
I16_MIN = -(2 ** 15)


def _dsa_kernel(q_ref, k_ref, v_ref, qi_ref, ki_ref, wi_ref, o_ref,
                keys_ref, hi_ref, lo_ref, ties_ref, s_ref, max_ref, acc_ref, *, t, topk):
    it = pl.program_id(1)
    width = q_ref.shape[2]
    heads = _head_lane_masks(t, width)
    causal = _local_causal(t)

    qidx = qi_ref[0]
    qi_stack = jnp.concatenate(
        [_stack_masked(qidx[:, g * width:(g + 1) * width], heads) for g in range(IDX_HEADS // 4)], axis=0)
    wi = wi_ref[0]
    w_rep = [jnp.broadcast_to(wi[:, HEAD_DIM + h:HEAD_DIM + h + 1], (t, LANES)) for h in range(IDX_HEADS)]

    def score_tile(j):
        logit = _nt_dot(qi_stack, ki_ref[0, _tile_rows(j, t), :])
        cols = []
        for c in range(0, t, LANES):
            acc = jnp.zeros((t, LANES), F32)
            for h in range(IDX_HEADS):
                acc = acc + jnp.maximum(logit[h * t:(h + 1) * t, c:c + LANES], 0.0) * w_rep[h]
            cols.append(acc)
        isc = jnp.concatenate(cols, axis=1) + 0.0
        isc = jnp.where(j == it, jnp.where(causal, isc, -jnp.inf), isc)
        bits = lax.bitcast_convert_type(isc, jnp.int32)
        key = bits ^ ((bits >> 31) & 0x7FFFFFFF)
        keys_ref[j] = key
        key_t = key.T
        hi_ref[j] = (key_t >> 16).astype(I16)
        lo_ref[j] = ((key_t & 0xFFFF) + I16_MIN).astype(I16)

    _for_each_tile(it + 1, score_tile, group=4)
    n_pairs = it // 2 + 1

    @pl.when(it % 2 == 0)
    def _():
        hi_ref[it + 1] = jnp.full((t, t), I16_MIN, I16)
        lo_ref[it + 1] = jnp.full((t, t), I16_MIN, I16)

    def count(ref, pred):
        def one(j, cnt):
            return cnt + _sublane_fold(jnp.where(pred(ref[j]), jnp.int16(1), jnp.int16(0)), I16_ROWS)
        cnt = lax.fori_loop(0, n_pairs, lambda i, cnt: one(2 * i + 1, one(2 * i, cnt)), jnp.zeros((I16_ROWS, t), I16))
        return jnp.sum(cnt.astype(jnp.int32), axis=0, keepdims=True)

    def kth_largest(ref, k):
        thr = jnp.where(count(ref, lambda x: x >= jnp.int16(0)) >= k, 0, jnp.full((1, t), I16_MIN, jnp.int32))

        def bit_step(i, thr):
            cand = thr | (1 << (14 - i))
            return jnp.where(count(ref, lambda x: x >= cand.astype(I16)) >= k, cand, thr)

        return lax.fori_loop(0, 15, bit_step, thr)

    thr_hi = kth_largest(hi_ref, topk)
    thr_hi16 = thr_hi.astype(I16)
    k_lo = topk - count(hi_ref, lambda x: x > thr_hi16)

    def keep_matching(j):
        lo_ref[j] = jnp.where(hi_ref[j] == thr_hi16, lo_ref[j], jnp.int16(I16_MIN))

    @pl.loop(0, n_pairs)
    def _(i):
        keep_matching(2 * i)
        keep_matching(2 * i + 1)

    thr_lo = kth_largest(lo_ref, k_lo)
    thr_lo16 = thr_lo.astype(I16)
    wanted = (k_lo - count(lo_ref, lambda x: x > thr_lo16)).astype(F32)
    thr = (thr_hi << 16) | (thr_lo - I16_MIN)
    thr_q = _lane_tile(jnp.broadcast_to(thr, (LANES, t)).T, t)
    wanted_q = _lane_tile(jnp.broadcast_to(wanted, (LANES, t)).T, t)

    q_stack = _stack_masked(q_ref[0], heads)
    r = lax.broadcasted_iota(jnp.int32, (t, t), 0)
    c = lax.broadcasted_iota(jnp.int32, (t, t), 1)
    before = (r < c).astype(BF16)
    _reset_stats(max_ref, acc_ref)

    ties_ref[...] = jnp.zeros(ties_ref.shape, F32)

    not_causal = jnp.where(causal, 0.0, MASKED)

    def select_tile(j):
        key = keys_ref[j]
        tie = key == thr_q
        tie_b = jnp.where(tie, 1.0, 0.0).astype(BF16)
        ties_before = _dot(tie_b, before)
        seen = ties_ref[...]
        tie_rank = _lane_tile(seen, t) + ties_before
        bias = jnp.where(key > thr_q, 0.0, jnp.where(tie, jnp.where(tie_rank < wanted_q, 0.0, MASKED), MASKED))
        bias = bias + jnp.where(j == it, not_causal, 0.0)
        s = _nt_dot(q_stack, k_ref[0, _tile_rows(j, t), :]) + jnp.concatenate([bias] * A_HEADS, axis=0)
        _record_scores(s, j, 0, s_ref, max_ref)
        in_tile = ties_before[:, t - 1:] + jnp.where(tie[:, t - 1:], 1.0, 0.0)
        ties_ref[...] = seen + jnp.broadcast_to(in_tile, seen.shape)

    _for_each_tile(it + 1, select_tile, group=4)
    _finish_max(max_ref)

    _for_each_tile(it + 1, lambda j: _accumulate_probs(
        j, 0, A_HEADS * t, v_ref[0, _tile_rows(j, t), :], s_ref, max_ref, acc_ref), group=4)

    sums_only = lax.broadcasted_iota(jnp.int32, (t, LANES), 1) >= HEAD_DIM
    o = []
    for h in range(A_HEADS):
        a = acc_ref[h * t:(h + 1) * t, :]
        inv = 1.0 / jnp.where(sums_only, a, 1.0)
        o.append(pltpu.roll(a, HEAD_DIM, 1) * inv if h % 2 else a * pltpu.roll(inv, HEAD_DIM, 1))
    o_ref[0] = _interleave_pairs(o, t).astype(o_ref.dtype)


def _dsa_attention(qa, ka, va, qi, ki, wi):
    b, s, w = qa.shape
    t = ATT_TILE
    topk = min(INDEX_TOPK, s // 4)
    assert t >= topk
    row = lambda width: pl.BlockSpec((1, t, width), lambda bi, i: (bi, i, 0))
    full = lambda width: pl.BlockSpec((1, s, width), lambda bi, i: (bi, 0, 0))
    return pl.pallas_call(
        functools.partial(_dsa_kernel, t=t, topk=topk),
        grid=(b, s // t),
        in_specs=[row(w), full(w), full(va.shape[2]), row(qi.shape[2]), full(w), row(LANES)],
        out_specs=row(w),
        out_shape=jax.ShapeDtypeStruct((b, s, w), BF16),
        scratch_shapes=[pltpu.VMEM((s // t, t, t), jnp.int32),
                        pltpu.VMEM((s // t + s // t % 2, t, t), I16), pltpu.VMEM((s // t + s // t % 2, t, t), I16),
                        pltpu.VMEM((t, LANES), F32)]
                       + _ATT_SCRATCH(s // t, A_HEADS * t, t, LANES),
        compiler_params=_params("parallel", "arbitrary"),
        name="dsa_attention",
    )(qa, ka, va, qi, ki, wi)


def _merge_ffn_kernel(x_ref, g_ref, oa_ref, ob_ref, oc_ref, wg_ref, wb_ref, wo_ref,
                      g2_ref, fg_ref, fu_ref, fd_ref, o_ref, *, chunk):
    x = x_ref[...]
    d = x.shape[1]
    h = _rmsnorm_rows(x, g_ref[...]).astype(BF16)
    branches, row0 = [], 0
    for o_ref_i in (oa_ref, ob_ref, oc_ref):
        branches.append((o_ref_i[...], row0))
        row0 += o_ref_i.shape[1]
    merged = []
    for c in range(0, d, chunk):
        m = jnp.zeros((x.shape[0], chunk), F32)
        for i, (o_i, r0) in enumerate(branches):
            gate = jax.nn.sigmoid(_dot(h, wg_ref[:, i * d + c:i * d + c + chunk]))
            m = m + gate * _dot(o_i, wb_ref[r0:r0 + o_i.shape[1], c:c + chunk])
        merged.append(m.astype(BF16))
    x1 = x + _dot(jnp.concatenate(merged, axis=1), wo_ref[...])
    o_ref[...] = _ffn_rows(x1, g2_ref[...], fg_ref, fu_ref, fd_ref, chunk)


def _merge_ffn(x, g, oa, ob, oc, wg, wb, wo, g2, fg, fu, fd, layer, tm):
    n, d = x.shape
    row = lambda w: pl.BlockSpec((tm, w), lambda i: (i, 0))
    return pl.pallas_call(
        functools.partial(_merge_ffn_kernel, chunk=MXU_DIM),
        grid=(n // tm,),
        in_specs=[row(d), _resident((1, d)), row(oa.shape[1]), row(ob.shape[1]), row(oc.shape[1]),
                  _resident(wg.shape, (layer,)), _resident(wb.shape, (layer,)), _resident(wo.shape, (layer,)),
                  _resident((1, d)), _resident(fg.shape, (layer, 1)), _resident(fu.shape, (layer, 1)),
                  _resident(fd.shape, (layer, 1))],
        out_specs=row(d),
        out_shape=jax.ShapeDtypeStruct((n, d), F32),
        compiler_params=_params("parallel"),
        name="merge_ffn",
    )(x, g, oa, ob, oc, wg, wb, wo, g2, fg, fu, fd)


def kernel(x, positions, norm_g, w_in, qk_norm_g, lambda_params, diff_subln_g, w_branch, w_out,
           ffn_w_gate, ffn_w_up, ffn_w_down):
    b, s, d = x.shape
    n = b * s
    depth = norm_g.shape[0]
    tm = min(1024, n)
    assert n % tm == 0 and s % ATT_TILE == 0 and ATT_TILE == MOBA_BLOCK

    cos, sin = _rope_tables(positions, min(2 * tm, n))
    w_proj, w_gates = _proj_weights(w_in)
    ffn_w = tuple(w.astype(BF16) for w in (ffn_w_gate, ffn_w_up, ffn_w_down))
    w_mix, w_o = w_branch.astype(BF16), w_out.astype(BF16)
    xf = x.reshape(n, d)
    for layer in range(depth):
        lam_init = 0.8 - 0.6 * math.exp(-0.3 * layer)
        xf = _ffn(xf, norm_g[layer, 0][None, :], *ffn_w, (layer, 0), tm)

        qkg = jnp.tile(qk_norm_g[layer], (1, LANES // HEAD_DIM))
        outs = _proj(xf, norm_g[layer, 1][None, :], qkg, cos, sin, w_proj, layer, tm)
        p = {name: o.reshape(b, s, o.shape[1]) for (name, *_), o in zip(_PROJ_OUTPUTS, outs)}
        o_a = _dsa_attention(p["qa"], p["ka"], p["va"], p["qi"], p["ki"], p["wi"])
        o_b = _moba_attention(p["qb"], p["kb"], p["vb"])
        o_c = _diff_attention(p["qc"], p["kc"], p["vc"], lambda_params[layer], diff_subln_g[layer][None, :], lam_init)
        xf = _merge_ffn(xf, norm_g[layer, 1][None, :], o_a.reshape(n, -1), o_b.reshape(n, -1), o_c.reshape(n, -1),
                        w_gates, w_mix, w_o, norm_g[layer, 2][None, :], *ffn_w, layer, tm // 2)
    return xf.reshape(b, s, d)
```

```python
import functools
import math

import jax
import jax.numpy as jnp
from jax import lax
from jax.experimental import pallas as pl
from jax.experimental.pallas import tpu as pltpu

F32 = jnp.float32
BF16 = jnp.bfloat16

HEAD_DIM = 64
ROT_DIM = HEAD_DIM // 4
ROT_HALF = ROT_DIM // 2
ROPE_THETA = 500000.0
EPS = 1e-6
A_HEADS = 4
IDX_HEADS = 8
IDX_DIM = 64
INDEX_TOPK = 256
B_HEADS = 4
MOBA_BLOCK = 256
MOBA_TOPK = 3
C_HEADS = 4
C_VDIM = 2 * HEAD_DIM

LANES = 128
SUBLANES = 8
MXU_DIM = 256
ATT_TILE = 256
MASKED = -1e30
VMEM_LIMIT = 56 * 1024 * 1024
LOG2E = 1.4426950408889634


def _nt_dot(a, b):
    return lax.dot_general(a, b, (((1,), (1,)), ((), ())), preferred_element_type=F32)


def _dot(a, b):
    return jnp.dot(a, b, preferred_element_type=F32)


def _resident(shape, index=()):
    block = (None,) * len(index) + tuple(shape[len(index):])
    at = tuple(index) + (0,) * (len(shape) - len(index))
    return pl.BlockSpec(block, lambda *_: at, pipeline_mode=pl.Buffered(1))


def _params(*sem):
    return pltpu.CompilerParams(dimension_semantics=sem, vmem_limit_bytes=VMEM_LIMIT)


def _rope_table_kernel(pos_ref, freq_ref, cos_ref, sin_ref):
    ang = pos_ref[...].astype(F32) * freq_ref[...]
    lane = lax.broadcasted_iota(jnp.int32, ang.shape, 1) % HEAD_DIM
    c = jnp.cos(ang)
    s = jnp.sin(ang)
    cos_ref[...] = jnp.where(lane < ROT_DIM, c, 1.0)
    sin_ref[...] = jnp.where(lane < ROT_HALF, -s, jnp.where(lane < ROT_DIM, s, 0.0))


def _rope_tables(positions, tm):
    n = positions.size
    pos = positions.reshape(n, 1)
    inv_freq = jnp.power(ROPE_THETA, -jnp.arange(0, ROT_DIM, 2, dtype=F32) / ROT_DIM)
    freq = jnp.tile(jnp.concatenate([inv_freq, inv_freq, jnp.zeros((HEAD_DIM - ROT_DIM,), F32)]), 2)[None, :]
    return pl.pallas_call(
        _rope_table_kernel,
        grid=(n // tm,),
        in_specs=[pl.BlockSpec((tm, 1), lambda i: (i, 0)), pl.BlockSpec((1, LANES), lambda i: (0, 0))],
        out_specs=[pl.BlockSpec((tm, LANES), lambda i: (i, 0))] * 2,
        out_shape=[jax.ShapeDtypeStruct((n, LANES), F32)] * 2,
        compiler_params=_params("parallel"),
        name="rope_tables",
    )(pos, freq)


def _rmsnorm_rows(x, g):
    return x * lax.rsqrt(jnp.mean(x * x, axis=-1, keepdims=True) + EPS) * g


def _ffn_kernel(x_ref, g_ref, wg_ref, wu_ref, wd_ref, o_ref, *, chunk):
    x = x_ref[...]
    h = _rmsnorm_rows(x, g_ref[...]).astype(BF16)
    acc = jnp.zeros(x.shape, F32)
    for c in range(0, wg_ref.shape[1], chunk):
        a = _dot(h, wg_ref[:, c:c + chunk])
        b = _dot(h, wu_ref[:, c:c + chunk])
        t = (a * jax.nn.sigmoid(a) * b).astype(BF16)
        acc = acc + _dot(t, wd_ref[c:c + chunk, :])
    o_ref[...] = x + 0.5 * acc


def _ffn(x, g, wg, wu, wd, which, tm):
    n, d = x.shape
    return pl.pallas_call(
        functools.partial(_ffn_kernel, chunk=MXU_DIM),
        grid=(n // tm,),
        in_specs=[pl.BlockSpec((tm, d), lambda i: (i, 0)), _resident((1, d)),
                  _resident(wg.shape, which), _resident(wu.shape, which), _resident(wd.shape, which)],
        out_specs=pl.BlockSpec((tm, d), lambda i: (i, 0)),
        out_shape=jax.ShapeDtypeStruct((n, d), F32),
        compiler_params=_params("parallel"),
        name="ffn",
    )(x, g, wg, wu, wd)


_W_IN_SPLITS = (("qa", 256), ("ka", 64), ("va", 64), ("qi", 512), ("ki", 64), ("wi", 8), ("qb", 256), ("kb", 256),
                ("vb", 256), ("qc", 512), ("kc", 512), ("vc", 512), ("ga", 1024), ("gb", 1024), ("gc", 1024))
_PROJ_COLUMNS = (
    ("qa", 256, "norm_rope_scale", 0),
    ("key_value", 128, "key_value", 1),
    ("qi", 512, "rope", None),
    ("index_key_weight", 128, "index_key_weight", None),
    ("qb", 256, "norm_rope_scale", 2),
    ("kb", 256, "norm_rope", 3),
    ("vb", 256, "plain", None),
    ("qc", 512, "norm_rope_scale", 4),
    ("kc", 512, "norm_rope", 5),
    ("vc", 512, "plain", None),
)
_GATE_COLUMNS = 3 * 1024
_PROJ_OUTPUTS = (
    ("qa", 256, BF16), ("ka", 256, BF16), ("va", 128, BF16), ("qi", 512, BF16), ("ki", 256, BF16), ("wi", 128, F32),
    ("qb", 256, BF16), ("kb", 256, BF16), ("vb", 256, BF16), ("qc", 512, BF16), ("kc", 512, BF16), ("vc", 512, BF16),
)
_Q_SCALE = HEAD_DIM ** -0.5 * LOG2E


def _ffn_proj_kernel(x_ref, g0_ref, fg_ref, fu_ref, fd_ref, g_ref, qkg_ref, cos_ref, sin_ref, w_ref, x1_ref, *o_refs):
    x = x_ref[...]
    h = _rmsnorm_rows(x, g0_ref[...]).astype(BF16)
    acc = jnp.zeros(x.shape, F32)
    for c in range(0, fg_ref.shape[1], MXU_DIM):
        a = _dot(h, fg_ref[:, c:c + MXU_DIM])
        b = _dot(h, fu_ref[:, c:c + MXU_DIM])
        acc = acc + _dot((a * jax.nn.sigmoid(a) * b).astype(BF16), fd_ref[c:c + MXU_DIM, :])
    x1 = x + 0.5 * acc
    x1_ref[...] = x1
    _proj_rows(x1, g_ref, qkg_ref, cos_ref, sin_ref, w_ref, o_refs)


def _proj_kernel(x_ref, g_ref, qkg_ref, cos_ref, sin_ref, w_ref, *o_refs):
    _proj_rows(x_ref[...], g_ref, qkg_ref, cos_ref, sin_ref, w_ref, o_refs)


def _proj_rows(x, g_ref, qkg_ref, cos_ref, sin_ref, w_ref, o_refs):
    out = {name: ref for (name, _, _), ref in zip(_PROJ_OUTPUTS, o_refs)}
    tm = x.shape[0]
    h = _rmsnorm_rows(x, g_ref[...]).astype(BF16)
    cos = cos_ref[...]
    sin = sin_ref[...]
    lane = lax.broadcasted_iota(jnp.int32, (tm, LANES), 1)
    take_upper = (lane % HEAD_DIM) < ROT_HALF
    low_half = lane < HEAD_DIM
    r = lax.broadcasted_iota(jnp.int32, (LANES, LANES), 0) // HEAD_DIM
    c = lax.broadcasted_iota(jnp.int32, (LANES, LANES), 1) // HEAD_DIM
    same_head = (r == c).astype(BF16)

    def epilogue(y, kind, g_idx):
        if kind in ("norm_rope_scale", "norm_rope"):
            ssq = _dot((y * y).astype(BF16), same_head)
            y = y * lax.rsqrt(ssq * (1.0 / HEAD_DIM) + EPS) * qkg_ref[g_idx:g_idx + 1, :]
        if kind in ("norm_rope_scale", "norm_rope", "rope"):
            partner = jnp.where(take_upper, pltpu.roll(y, LANES - ROT_HALF, 1), pltpu.roll(y, ROT_HALF, 1))
            y = y * cos + partner * sin
        if kind == "norm_rope_scale":
            y = y * _Q_SCALE
        return y

    def on_both_halves(y):
        return jnp.where(low_half, y, pltpu.roll(y, HEAD_DIM, 1))

    col = 0
    for name, width, kind, g_idx in _PROJ_COLUMNS:
        step = min(width, MXU_DIM)
        for c0 in range(0, width, step):
            y = _dot(h, w_ref[:, col + c0:col + c0 + step])
            if kind == "key_value":
                ka = on_both_halves(epilogue(y, "norm_rope", g_idx)).astype(BF16)
                for s0 in range(0, out["ka"].shape[1], LANES):
                    out["ka"][:, s0:s0 + LANES] = ka
                out["va"][...] = jnp.where(low_half, pltpu.roll(y, HEAD_DIM, 1), 1.0).astype(BF16)
                continue
            if kind == "index_key_weight":
                ki = on_both_halves(epilogue(y, "rope", None)).astype(BF16)
                for s0 in range(0, out["ki"].shape[1], LANES):
                    out["ki"][:, s0:s0 + LANES] = ki
                out["wi"][...] = y * ((IDX_HEADS * IDX_DIM) ** -0.5)
                continue
            o_ref = out[name]
            for s0 in range(0, step, LANES):
                o_ref[:, c0 + s0:c0 + s0 + LANES] = epilogue(y[:, s0:s0 + LANES], kind, g_idx).astype(o_ref.dtype)
        col += width


def _proj_weights(w_in):
    w = w_in.astype(BF16)
    names = [name for name, _ in _W_IN_SPLITS]
    end_of_wi = sum(width for _, width in _W_IN_SPLITS[:names.index("wi") + 1])
    gates = w.shape[-1] - _GATE_COLUMNS
    pad = jnp.zeros(w.shape[:-1] + (-end_of_wi % LANES,), w.dtype)
    padded = jnp.concatenate([w[..., :end_of_wi], pad, w[..., end_of_wi:gates]], axis=-1)
    assert padded.shape[-1] == sum(width for _, width, _, _ in _PROJ_COLUMNS)
    return padded, w[..., gates:]


def _ffn_proj(x, g0, fg, fu, fd, g, qkg, cos, sin, w, layer, tm):
    n, d = x.shape
    row = lambda width: pl.BlockSpec((tm, width), lambda i: (i, 0))
    return pl.pallas_call(
        _ffn_proj_kernel,
        grid=(n // tm,),
        in_specs=[row(d), _resident((1, d)), _resident(fg.shape, (layer, 0)), _resident(fu.shape, (layer, 0)),
                  _resident(fd.shape, (layer, 0)), _resident((1, d)), _resident(qkg.shape), row(LANES), row(LANES),
                  _resident(w.shape, (layer,))],
        out_specs=[row(d)] + [row(width) for _, width, _ in _PROJ_OUTPUTS],
        out_shape=[jax.ShapeDtypeStruct((n, d), F32)]
                  + [jax.ShapeDtypeStruct((n, width), dtype) for _, width, dtype in _PROJ_OUTPUTS],
        compiler_params=_params("parallel"),
        name="ffn_in_proj",
    )(x, g0, fg, fu, fd, g, qkg, cos, sin, w)


def _proj(x, g, qkg, cos, sin, w, layer, tm):
    n, d = x.shape
    row = lambda width: pl.BlockSpec((tm, width), lambda i: (i, 0))
    return pl.pallas_call(
        _proj_kernel,
        grid=(n // tm,),
        in_specs=[row(d), _resident((1, d)), _resident(qkg.shape), row(LANES), row(LANES),
                  _resident(w.shape, (layer,))],
        out_specs=[row(width) for _, width, _ in _PROJ_OUTPUTS],
        out_shape=[jax.ShapeDtypeStruct((n, width), dtype) for _, width, dtype in _PROJ_OUTPUTS],
        compiler_params=_params("parallel"),
        name="in_proj",
    )(x, g, qkg, cos, sin, w)


def _lane_max(x):
    out = x[:, :LANES]
    for c in range(LANES, x.shape[1], LANES):
        out = jnp.maximum(out, x[:, c:c + LANES])
    return out


def _lane_tile(x, width):
    return jnp.concatenate([x] * (width // LANES), axis=1)


def _reset_stats(max_ref, acc_ref):
    max_ref[...] = jnp.full(max_ref.shape, MASKED, F32)
    acc_ref[...] = jnp.zeros(acc_ref.shape, F32)


def _record_scores(s, j, r0, s_ref, max_ref):
    n = s.shape[0]
    s_ref[j, r0:r0 + n, :] = s
    max_ref[r0:r0 + n, :] = jnp.maximum(max_ref[r0:r0 + n, :], _lane_max(s))


def _finish_max(max_ref):
    m = max_ref[...]
    max_ref[...] = jnp.broadcast_to(jnp.max(m, axis=-1, keepdims=True), m.shape)


def _accumulate_probs(j, r0, n, v, s_ref, max_ref, acc_ref):
    s = s_ref[j, r0:r0 + n, :]
    p = jnp.exp2(s - _lane_tile(max_ref[r0:r0 + n, :], s.shape[1]))
    acc_ref[r0:r0 + n, :] += _dot(p.astype(BF16), v)


def _with_ones(v):
    return jnp.concatenate([v, jnp.ones(v.shape, v.dtype)], axis=1)


def _normalised(r0, n, acc_ref):
    return acc_ref[r0:r0 + n, :LANES] / acc_ref[r0:r0 + n, LANES:]


def _head_lane_masks(rows, width):
    lane = lax.broadcasted_iota(jnp.int32, (rows, width), 1)
    return [(lane >= h * HEAD_DIM) & (lane < (h + 1) * HEAD_DIM) for h in range(width // HEAD_DIM)]


def _stack_masked(q, masks):
    zero = jnp.zeros_like(q)
    return jnp.concatenate([jnp.where(m, q, zero) for m in masks], axis=0)


def _local_causal(t):
    row = lax.broadcasted_iota(jnp.int32, (t, t), 0)
    col = lax.broadcasted_iota(jnp.int32, (t, t), 1)
    return col <= row


def _tile_rows(j, t):
    return pl.ds(pl.multiple_of(j * t, t), t)


def _for_each_tile(n, body, group=2):
    @pl.loop(0, n // group)
    def _(i):
        for k in range(group):
            body(group * i + k)

    part = group // 2
    while part >= 1:
        @pl.when(n % (2 * part) >= part)
        def _():
            for k in range(part):
                body(n - n % (2 * part) + k)
        part //= 2


def _interleave_pairs(o, t):
    lane = lax.broadcasted_iota(jnp.int32, (t, LANES), 1)
    low = lane < HEAD_DIM
    return jnp.concatenate([jnp.where(low, o[0], o[1]), jnp.where(low, o[2], o[3])], axis=1)


_ATT_SCRATCH = lambda n_tiles, rows, t, acc_width: [
    pltpu.VMEM((n_tiles, rows, t), F32),
    pltpu.VMEM((rows, LANES), F32),
    pltpu.VMEM((rows, acc_width), F32),
]


def _diff_kernel(lam_ref, g_ref, q_ref, k_ref, v_ref, o_ref, s_ref, max_ref, acc_ref, *, lam_init, t):
    it = pl.program_id(1)
    lane = lax.broadcasted_iota(jnp.int32, (t, C_VDIM), 1)
    maps = [lane < HEAD_DIM, lane >= HEAD_DIM]
    q_all = q_ref[0]
    qs = [_stack_masked(q_all[:, h * C_VDIM:(h + 1) * C_VDIM], maps) for h in range(C_HEADS)]
    _reset_stats(max_ref, acc_ref)
    causal = jnp.where(_local_causal(t), 0.0, MASKED)
    causal2 = jnp.concatenate([causal, causal], axis=0)

    def scores(j):
        rows = _tile_rows(j, t)
        bias = jnp.where(j == it, causal2, 0.0)
        for h in range(C_HEADS):
            s = _nt_dot(qs[h], k_ref[0, rows, h * C_VDIM:(h + 1) * C_VDIM]) + bias
            _record_scores(s, j, h * 2 * t, s_ref, max_ref)

    _for_each_tile(it + 1, scores, group=4)
    _finish_max(max_ref)

    def probs(j):
        rows = _tile_rows(j, t)
        for h in range(C_HEADS):
            _accumulate_probs(j, h * 2 * t, 2 * t, _with_ones(v_ref[0, rows, h * C_VDIM:(h + 1) * C_VDIM]),
                              s_ref, max_ref, acc_ref)

    _for_each_tile(it + 1, probs, group=4)

    lp = lam_ref[...]
    lam = (jnp.exp(jnp.sum(lp[0:1] * lp[1:2], axis=-1, keepdims=True))
           - jnp.exp(jnp.sum(lp[2:3] * lp[3:4], axis=-1, keepdims=True)) + lam_init)
    for h in range(C_HEADS):
        o = _normalised(h * 2 * t, t, acc_ref) - lam * _normalised(h * 2 * t + t, t, acc_ref)
        o = o * lax.rsqrt(jnp.mean(o * o, axis=-1, keepdims=True) + EPS) * g_ref[...] * (1.0 - lam_init)
        o_ref[0, :, h * C_VDIM:(h + 1) * C_VDIM] = o.astype(o_ref.dtype)


def _diff_attention(qc, kc, vc, lam_p, subln_g, lam_init):
    b, s, w = qc.shape
    t = ATT_TILE
    qspec = pl.BlockSpec((1, t, w), lambda bi, i: (bi, i, 0))
    kvspec = pl.BlockSpec((1, s, w), lambda bi, i: (bi, 0, 0))
    return pl.pallas_call(
        functools.partial(_diff_kernel, lam_init=lam_init, t=t),
        grid=(b, s // t),
        in_specs=[pl.BlockSpec(lam_p.shape, lambda bi, i: (0, 0)),
                  pl.BlockSpec((1, C_VDIM), lambda bi, i: (0, 0)), qspec, kvspec, kvspec],
        out_specs=qspec,
        out_shape=jax.ShapeDtypeStruct((b, s, w), BF16),
        scratch_shapes=_ATT_SCRATCH(s // t, 2 * C_HEADS * t, t, 2 * LANES),
        compiler_params=_params("parallel", "arbitrary"),
        name="diff_attention",
    )(lam_p, subln_g, qc, kc, vc)


def _moba_kernel(q_ref, k_ref, v_ref, o_ref, kmean_hi, kmean_lo, bias_ref, s_ref, max_ref, acc_ref, *, t):
    it = pl.program_id(1)
    n_blk = k_ref.shape[1] // t
    width = q_ref.shape[2]

    @pl.when(it == 0)
    def _():
        kmean_hi[...] = jnp.zeros_like(kmean_hi)
        kmean_lo[...] = jnp.zeros_like(kmean_lo)
        for n in range(n_blk):
            km = jnp.mean(k_ref[0, n * t:(n + 1) * t, :].astype(F32), axis=0, keepdims=True)
            hi = km.astype(BF16)
            kmean_hi[n:n + 1, :] = hi
            kmean_lo[n:n + 1, :] = (km - hi.astype(F32)).astype(BF16)

    heads = _head_lane_masks(t, width)
    q_stack = _stack_masked(q_ref[0], heads)

    no_bias = jnp.zeros(bias_ref.shape[1:], F32)
    bias_ref[it] = no_bias

    @pl.when(it <= MOBA_TOPK)
    def _():
        for n in range(min(MOBA_TOPK, n_blk)):
            @pl.when(n < it)
            def _():
                bias_ref[n] = no_bias

    @pl.when(it > MOBA_TOPK)
    def _():
        blk = lax.broadcasted_iota(jnp.int32, (SUBLANES, t), 0)
        past = blk < it
        pad = jnp.full((LANES - SUBLANES, t), MASKED, F32)
        bias_q = []
        for h in range(B_HEADS):
            qh = q_stack[h * t:(h + 1) * t]
            gate = (_nt_dot(kmean_hi[...], qh) + _nt_dot(kmean_lo[...], qh))[:SUBLANES]
            gate = jnp.where(past, gate, -jnp.inf)
            rank = jnp.zeros((SUBLANES, t), jnp.int32)
            for n in range(n_blk):
                gn = gate[n:n + 1, :]
                rank = rank + jnp.where((gn > gate) | ((gn == gate) & (blk > n)), 1, 0)
            bias = jnp.where((rank < MOBA_TOPK) & past, 0.0, MASKED)
            bias_q.append(jnp.concatenate([bias, pad], axis=0).T.astype(BF16))
        bias_q = jnp.concatenate(bias_q, axis=0)
        lane_of = lax.broadcasted_iota(jnp.int32, (LANES, LANES), 0)
        for n in range(n_blk - 1):
            @pl.when(n < it)
            def _():
                pick = (lane_of == n).astype(BF16)
                bias_ref[n] = _dot(bias_q, pick)

    _reset_stats(max_ref, acc_ref)
    causal = jnp.where(_local_causal(t), 0.0, MASKED)
    causal4 = jnp.concatenate([causal] * B_HEADS, axis=0)

    def scores(j):
        s = _nt_dot(q_stack, k_ref[0, _tile_rows(j, t), :]) + _lane_tile(bias_ref[j], t)
        _record_scores(s + jnp.where(j == it, causal4, 0.0), j, 0, s_ref, max_ref)

    _for_each_tile(it + 1, scores, group=4)
    _finish_max(max_ref)

    def probs(j):
        rows = _tile_rows(j, t)
        for pair in range(B_HEADS // 2):
            _accumulate_probs(j, pair * 2 * t, 2 * t, _with_ones(v_ref[0, rows, pair * LANES:(pair + 1) * LANES]),
                              s_ref, max_ref, acc_ref)

    _for_each_tile(it + 1, probs, group=4)

    o = [_normalised(h * t, t, acc_ref) for h in range(B_HEADS)]
    o_ref[0] = _interleave_pairs(o, t).astype(o_ref.dtype)


def _moba_attention(qb, kb, vb):
    b, s, w = qb.shape
    t = MOBA_BLOCK
    assert s // t <= SUBLANES
    qspec = pl.BlockSpec((1, t, w), lambda bi, i: (bi, i, 0))
    kvspec = pl.BlockSpec((1, s, w), lambda bi, i: (bi, 0, 0))
    return pl.pallas_call(
        functools.partial(_moba_kernel, t=t),
        grid=(b, s // t),
        in_specs=[qspec, kvspec, kvspec],
        out_specs=qspec,
        out_shape=jax.ShapeDtypeStruct((b, s, w), BF16),
        scratch_shapes=[pltpu.VMEM((LANES, w), BF16), pltpu.VMEM((LANES, w), BF16),
                        pltpu.VMEM((s // t, B_HEADS * t, LANES), F32)]
                       + _ATT_SCRATCH(s // t, B_HEADS * t, t, 2 * LANES),
        compiler_params=_params("parallel", "arbitrary"),
        name="moba_attention",
    )(qb, kb, vb)


def _sublane_fold(x, rows):
    parts = [x[r:r + rows] for r in range(0, x.shape[0], rows)]
    while len(parts) > 1:
        parts = [parts[i] + parts[i + 1] for i in range(0, len(parts), 2)]
    return parts[0]


I16 = jnp.int16
I16_ROWS = 2 * SUBLANES
I16_MIN = -(2 ** 15)


def _dsa_kernel(q_ref, k_ref, v_ref, qi_ref, ki_ref, wi_ref, o_ref,
                keys_ref, hi_ref, lo_ref, ties_ref, s_ref, max_ref, acc_ref, *, t, topk):
    it = pl.program_id(1)
    width = q_ref.shape[2]
    heads = _head_lane_masks(t, width)
    causal = _local_causal(t)

    qidx = qi_ref[0]
    qi_stack = jnp.concatenate(
        [_stack_masked(qidx[:, g * width:(g + 1) * width], heads) for g in range(IDX_HEADS // 4)], axis=0)
    wi = wi_ref[0]
    w_rep = [jnp.broadcast_to(wi[:, HEAD_DIM + h:HEAD_DIM + h + 1], (t, LANES)) for h in range(IDX_HEADS)]

    def score_tile(j):
        logit = _nt_dot(qi_stack, ki_ref[0, _tile_rows(j, t), :])
        cols = []
        for c in range(0, t, LANES):
            acc = jnp.zeros((t, LANES), F32)
            for h in range(IDX_HEADS):
                acc = acc + jnp.maximum(logit[h * t:(h + 1) * t, c:c + LANES], 0.0) * w_rep[h]
            cols.append(acc)
        isc = jnp.concatenate(cols, axis=1) + 0.0
        isc = jnp.where(j == it, jnp.where(causal, isc, -jnp.inf), isc)
        bits = lax.bitcast_convert_type(isc, jnp.int32)
        key = bits ^ ((bits >> 31) & 0x7FFFFFFF)
        keys_ref[j] = key
        key_t = key.T
        hi_ref[j] = (key_t >> 16).astype(I16)
        lo_ref[j] = ((key_t & 0xFFFF) + I16_MIN).astype(I16)

    _for_each_tile(it + 1, score_tile, group=4)
    n_pairs = it // 2 + 1

    @pl.when(it % 2 == 0)
    def _():
        hi_ref[it + 1] = jnp.full((t, t), I16_MIN, I16)
        lo_ref[it + 1] = jnp.full((t, t), I16_MIN, I16)

    def count(ref, pred):
        def one(j, cnt):
            return cnt + _sublane_fold(jnp.where(pred(ref[j]), jnp.int16(1), jnp.int16(0)), I16_ROWS)
        cnt = lax.fori_loop(0, n_pairs, lambda i, cnt: one(2 * i + 1, one(2 * i, cnt)), jnp.zeros((I16_ROWS, t), I16))
        return jnp.sum(cnt.astype(jnp.int32), axis=0, keepdims=True)

    def kth_largest(ref, k):
        thr = jnp.where(count(ref, lambda x: x >= jnp.int16(0)) >= k, 0, jnp.full((1, t), I16_MIN, jnp.int32))

        def bit_step(i, thr):
            cand = thr | (1 << (14 - i))
            return jnp.where(count(ref, lambda x: x >= cand.astype(I16)) >= k, cand, thr)

        return lax.fori_loop(0, 15, bit_step, thr)

    thr_hi = kth_largest(hi_ref, topk)
    thr_hi16 = thr_hi.astype(I16)
    k_lo = topk - count(hi_ref, lambda x: x > thr_hi16)

    def keep_matching(j):
        lo_ref[j] = jnp.where(hi_ref[j] == thr_hi16, lo_ref[j], jnp.int16(I16_MIN))

    @pl.loop(0, n_pairs)
    def _(i):
        keep_matching(2 * i)
        keep_matching(2 * i + 1)

    thr_lo = kth_largest(lo_ref, k_lo)
    thr_lo16 = thr_lo.astype(I16)
    wanted = (k_lo - count(lo_ref, lambda x: x > thr_lo16)).astype(F32)
    thr = (thr_hi << 16) | (thr_lo - I16_MIN)
    thr_q = _lane_tile(jnp.broadcast_to(thr, (LANES, t)).T, t)
    wanted_q = _lane_tile(jnp.broadcast_to(wanted, (LANES, t)).T, t)

    q_stack = _stack_masked(q_ref[0], heads)
    r = lax.broadcasted_iota(jnp.int32, (t, t), 0)
    c = lax.broadcasted_iota(jnp.int32, (t, t), 1)
    before = (r < c).astype(BF16)
    _reset_stats(max_ref, acc_ref)

    ties_ref[...] = jnp.zeros(ties_ref.shape, F32)

    not_causal = jnp.where(causal, 0.0, MASKED)

    def select_tile(j):
        key = keys_ref[j]
        tie = key == thr_q
        tie_b = jnp.where(tie, 1.0, 0.0).astype(BF16)
        ties_before = _dot(tie_b, before)
        seen = ties_ref[...]
        tie_rank = _lane_tile(seen, t) + ties_before
        bias = jnp.where(key > thr_q, 0.0, jnp.where(tie, jnp.where(tie_rank < wanted_q, 0.0, MASKED), MASKED))
        bias = bias + jnp.where(j == it, not_causal, 0.0)
        s = _nt_dot(q_stack, k_ref[0, _tile_rows(j, t), :]) + jnp.concatenate([bias] * A_HEADS, axis=0)
        _record_scores(s, j, 0, s_ref, max_ref)
        in_tile = ties_before[:, t - 1:] + jnp.where(tie[:, t - 1:], 1.0, 0.0)
        ties_ref[...] = seen + jnp.broadcast_to(in_tile, seen.shape)

    _for_each_tile(it + 1, select_tile, group=4)
    _finish_max(max_ref)

    _for_each_tile(it + 1, lambda j: _accumulate_probs(
        j, 0, A_HEADS * t, v_ref[0, _tile_rows(j, t), :], s_ref, max_ref, acc_ref), group=4)

    sums_only = lax.broadcasted_iota(jnp.int32, (t, LANES), 1) >= HEAD_DIM
    o = []
    for h in range(A_HEADS):
        a = acc_ref[h * t:(h + 1) * t, :]
        inv = 1.0 / jnp.where(sums_only, a, 1.0)
        o.append(pltpu.roll(a, HEAD_DIM, 1) * inv if h % 2 else a * pltpu.roll(inv, HEAD_DIM, 1))
    o_ref[0] = _interleave_pairs(o, t).astype(o_ref.dtype)


def _dsa_attention(qa, ka, va, qi, ki, wi):
    b, s, w = qa.shape
    t = ATT_TILE
    topk = min(INDEX_TOPK, s // 4)
    assert t >= topk
    row = lambda width: pl.BlockSpec((1, t, width), lambda bi, i: (bi, i, 0))
    full = lambda width: pl.BlockSpec((1, s, width), lambda bi, i: (bi, 0, 0))
    return pl.pallas_call(
        functools.partial(_dsa_kernel, t=t, topk=topk),
        grid=(b, s // t),
        in_specs=[row(w), full(w), full(va.shape[2]), row(qi.shape[2]), full(w), row(LANES)],
        out_specs=row(w),
        out_shape=jax.ShapeDtypeStruct((b, s, w), BF16),
        scratch_shapes=[pltpu.VMEM((s // t, t, t), jnp.int32),
                        pltpu.VMEM((s // t + s // t % 2, t, t), I16), pltpu.VMEM((s // t + s // t % 2, t, t), I16),
                        pltpu.VMEM((t, LANES), F32)]
                       + _ATT_SCRATCH(s // t, A_HEADS * t, t, LANES),
        compiler_params=_params("parallel", "arbitrary"),
        name="dsa_attention",
    )(qa, ka, va, qi, ki, wi)


def _merge_kernel(x_ref, g_ref, oa_ref, ob_ref, oc_ref, wg_ref, wb_ref, wo_ref, o_ref, *, chunk):
    x = x_ref[...]
    d = x.shape[1]
    h = _rmsnorm_rows(x, g_ref[...]).astype(BF16)
    branches, row0 = [], 0
    for o_ref_i in (oa_ref, ob_ref, oc_ref):
        branches.append((o_ref_i[...], row0))
        row0 += o_ref_i.shape[1]
    merged = []
    for c in range(0, d, chunk):
        m = jnp.zeros((x.shape[0], chunk), F32)
        for i, (o_i, r0) in enumerate(branches):
            gate = jax.nn.sigmoid(_dot(h, wg_ref[:, i * d + c:i * d + c + chunk]))
            m = m + gate * _dot(o_i, wb_ref[r0:r0 + o_i.shape[1], c:c + chunk])
        merged.append(m.astype(BF16))
    o_ref[...] = x + _dot(jnp.concatenate(merged, axis=1), wo_ref[...])


def _merge(x, g, oa, ob, oc, wg, wb, wo, layer, tm):
    n, d = x.shape
    row = lambda w: pl.BlockSpec((tm, w), lambda i: (i, 0))
    return pl.pallas_call(
        functools.partial(_merge_kernel, chunk=MXU_DIM),
        grid=(n // tm,),
        in_specs=[row(d), _resident((1, d)), row(oa.shape[1]), row(ob.shape[1]), row(oc.shape[1]),
                  _resident(wg.shape, (layer,)), _resident(wb.shape, (layer,)), _resident(wo.shape, (layer,))],
        out_specs=row(d),
        out_shape=jax.ShapeDtypeStruct((n, d), F32),
        compiler_params=_params("parallel"),
        name="merge_out",
    )(x, g, oa, ob, oc, wg, wb, wo)


def kernel(x, positions, norm_g, w_in, qk_norm_g, lambda_params, diff_subln_g, w_branch, w_out,
           ffn_w_gate, ffn_w_up, ffn_w_down):
    b, s, d = x.shape
    n = b * s
    depth = norm_g.shape[0]
    tm = min(1024, n)
    assert n % tm == 0 and s % ATT_TILE == 0 and ATT_TILE == MOBA_BLOCK

    cos, sin = _rope_tables(positions, min(2 * tm, n))
    w_proj, w_gates = _proj_weights(w_in)
    ffn_w = tuple(w.astype(BF16) for w in (ffn_w_gate, ffn_w_up, ffn_w_down))
    w_mix, w_o = w_branch.astype(BF16), w_out.astype(BF16)
    xf = x.reshape(n, d)
    for layer in range(depth):
        lam_init = 0.8 - 0.6 * math.exp(-0.3 * layer)
        qkg = jnp.tile(qk_norm_g[layer], (1, LANES // HEAD_DIM))
        xf, *outs = _ffn_proj(xf, norm_g[layer, 0][None, :], *ffn_w, norm_g[layer, 1][None, :], qkg, cos, sin,
                              w_proj, layer, tm // 2)
        p = {name: o.reshape(b, s, o.shape[1]) for (name, *_), o in zip(_PROJ_OUTPUTS, outs)}
        o_a = _dsa_attention(p["qa"], p["ka"], p["va"], p["qi"], p["ki"], p["wi"])
        o_b = _moba_attention(p["qb"], p["kb"], p["vb"])
        o_c = _diff_attention(p["qc"], p["kc"], p["vc"], lambda_params[layer], diff_subln_g[layer][None, :], lam_init)
        xf = _merge(xf, norm_g[layer, 1][None, :], o_a.reshape(n, -1), o_b.reshape(n, -1), o_c.reshape(n, -1),
                    w_gates, w_mix, w_o, layer, tm)

        xf = _ffn(xf, norm_g[layer, 2][None, :], *ffn_w, (layer, 1), tm)
    return xf.reshape(b, s, d)
```
